```python
import math
import jax, jax.numpy as jnp
from jax import lax
import numpy as np

D_MODEL = 1024
BATCH = 8
SEQ = 2048
DEPTH = 2
DEC_BATCH = 32
DEC_SEQ = 1
PAST_LEN = 16384
PAGE_SIZE = 128

HEAD_DIM = 64
ROT_DIM = HEAD_DIM // 4
ROPE_THETA = 500000.0
MOBA_HEADS = 8
MOBA_BLOCK = 256
MOBA_TOPK = 3
MOBA_QBLK = 64
DIFF_HEADS = 4
DIFF_DIM = 2 * HEAD_DIM
ATTN_QBLK = 128
MOBA_W = MOBA_HEADS * HEAD_DIM
DIFF_QK_W = DIFF_HEADS * 2 * HEAD_DIM
DIFF_V_W = DIFF_HEADS * DIFF_DIM
ATTN_IN_W = 3 * MOBA_W + 2 * DIFF_QK_W + DIFF_V_W
ATTN_OUT_W = MOBA_W + DIFF_V_W
D_RNN = D_MODEL
RNN_BLOCKS = 4
RNN_BLOCK_W = D_RNN // RNN_BLOCKS
CONV_W = 4
LRU_C = 8.0
D_FF = 4 * D_MODEL
EPS = 1e-6
N_ATTN_LAYERS = (DEPTH + 1) // 2
N_RNN_LAYERS = DEPTH // 2

kernel_name = "moba_diffattn_rglru_hybrid_step"


def rms_norm(x, g):
    xf = x.astype(jnp.float32)
    y = xf * lax.rsqrt(jnp.mean(xf * xf, axis=-1, keepdims=True) + EPS)
    return (y * g.astype(jnp.float32)).astype(x.dtype)


def rope_partial(x, pos):
    half = ROT_DIM // 2
    inv = jnp.exp(-math.log(ROPE_THETA) * jnp.arange(half, dtype=jnp.float32) * (2.0 / ROT_DIM))
    ang = pos.astype(jnp.float32)[:, None] * inv[None, :]
    ang = ang.reshape((ang.shape[0],) + (1,) * (x.ndim - 3) + (half,))
    cos, sin = jnp.cos(ang), jnp.sin(ang)
    xr = x[..., :ROT_DIM].astype(jnp.float32)
    x1, x2 = xr[..., :half], xr[..., half:]
    rot = jnp.concatenate([x1 * cos - x2 * sin, x2 * cos + x1 * sin], axis=-1).astype(x.dtype)
    return jnp.concatenate([rot, x[..., ROT_DIM:]], axis=-1)


def moba_attention(q, k, v, q_pos):
    B, Q, H, dh = q.shape
    T = k.shape[1]
    nb = -(-T // MOBA_BLOCK)
    pad = nb * MOBA_BLOCK - T
    kb = jnp.pad(k, ((0, 0), (0, pad), (0, 0), (0, 0))).reshape(B, nb, MOBA_BLOCK, H, dh)
    vb = jnp.pad(v, ((0, 0), (0, pad), (0, 0), (0, 0))).reshape(B, nb, MOBA_BLOCK, H, dh)
    kmean = jnp.mean(kb, axis=2, dtype=jnp.float32)
    k_eff = min(MOBA_TOPK, nb)
    qblk = math.gcd(Q, MOBA_QBLK)
    nqb = Q // qblk
    qs = q.reshape(B * nqb, qblk, H, dh)
    ps = jnp.broadcast_to(q_pos.reshape(1, nqb, qblk), (B, nqb, qblk)).reshape(B * nqb, qblk)
    bs = jnp.repeat(jnp.arange(B, dtype=jnp.int32), nqb)
    scale = dh ** -0.5
    blk_ids = jnp.arange(nb, dtype=jnp.int32)
    tok = jnp.arange(MOBA_BLOCK, dtype=jnp.int32)
    hidx = jnp.arange(H, dtype=jnp.int32)[:, None, None]

    def one(args):
        qi, pi, bi = args
        qf = qi.astype(jnp.float32)
        own = pi // MOBA_BLOCK
        gate = jnp.einsum('qhd,nhd->hqn', qf, kmean[bi])
        gate = jnp.where((blk_ids[None, :] < own[:, None])[None], gate, -jnp.inf)
        _, top = lax.top_k(gate, k_eff)
        sel = jnp.concatenate([top, jnp.broadcast_to(own[None, :, None], (H, qblk, 1))], axis=-1)
        slot_ok = jnp.concatenate([
            jnp.arange(k_eff)[None, :] < jnp.minimum(own, k_eff)[:, None],
            jnp.ones((qblk, 1), dtype=bool)], axis=-1)
        ksel = kb[bi, sel, :, hidx].astype(jnp.float32)
        vsel = vb[bi, sel, :, hidx].astype(jnp.float32)
        kpos = sel[..., None] * MOBA_BLOCK + tok
        mask = slot_ok[None, :, :, None] & (kpos <= pi[None, :, None, None])
        s = jnp.einsum('qhd,hqnkd->hqnk', qf, ksel) * scale
        s = jnp.where(mask, s, -jnp.inf)
        p = jax.nn.softmax(s.reshape(H, qblk, -1), axis=-1).reshape(s.shape)
        o = jnp.einsum('hqnk,hqnkd->qhd', p, vsel)
        return o.astype(q.dtype)

    out = lax.map(one, (qs, ps, bs))
    return out.reshape(B, Q, H, dh)


def diff_attention(q, k, v, q_pos, lam, lam_init, g_sub):
    B, Q, H, _, dh = q.shape
    T = k.shape[1]
    qblk = math.gcd(Q, ATTN_QBLK)
    nqb = Q // qblk
    qs = jnp.moveaxis(q.reshape(B, nqb, qblk, H, 2, dh), 1, 0)
    ps = q_pos.reshape(nqb, qblk)
    kpos = jnp.arange(T, dtype=jnp.int32)
    kf = k.astype(jnp.float32)
    vf = v.astype(jnp.float32)
    scale = dh ** -0.5

    def one(args):
        qi, pi = args
        s = jnp.einsum('bqhcd,bkhcd->bhcqk', qi.astype(jnp.float32), kf) * scale
        s = jnp.where(kpos[None, None, None, None, :] <= pi[None, None, None, :, None], s, -jnp.inf)
        p = jax.nn.softmax(s, axis=-1)
        a = p[:, :, 0] - lam * p[:, :, 1]
        return jnp.einsum('bhqk,bkhe->bqhe', a, vf)

    out = lax.map(one, (qs, ps))
    out = jnp.moveaxis(out, 0, 1).reshape(B, Q, H, 2 * dh)
    out = rms_norm(out, g_sub) * (1.0 - lam_init)
    return out.astype(q.dtype)


def attn_block(x, pos, past, g_pre, w_in, lq1, lk1, lq2, lk2, g_sub, w_out, g_post, lam_init):
    B, S, _ = x.shape
    u = rms_norm(x, g_pre) @ w_in
    mq, mk, mv, dq, dk, dv = jnp.split(
        u, [MOBA_W, 2 * MOBA_W, 3 * MOBA_W, 3 * MOBA_W + DIFF_QK_W, 3 * MOBA_W + 2 * DIFF_QK_W], axis=-1)
    mq = rope_partial(mq.reshape(B, S, MOBA_HEADS, HEAD_DIM), pos)
    mk = rope_partial(mk.reshape(B, S, MOBA_HEADS, HEAD_DIM), pos)
    mv = mv.reshape(B, S, MOBA_HEADS, HEAD_DIM)
    dq = rope_partial(dq.reshape(B, S, DIFF_HEADS, 2, HEAD_DIM), pos)
    dk = rope_partial(dk.reshape(B, S, DIFF_HEADS, 2, HEAD_DIM), pos)
    dv = dv.reshape(B, S, DIFF_HEADS, DIFF_DIM)
    if past is None:
        mk_all, mv_all, dk_all, dv_all = mk, mv, dk, dv
    else:
        pmk, pmv, pdk, pdv = past
        mk_all = jnp.concatenate([pmk.astype(x.dtype), mk], axis=1)
        mv_all = jnp.concatenate([pmv.astype(x.dtype), mv], axis=1)
        dk_all = jnp.concatenate([pdk.astype(x.dtype), dk], axis=1)
        dv_all = jnp.concatenate([pdv.astype(x.dtype), dv], axis=1)
    lam = (jnp.exp(jnp.sum(lq1 * lk1, dtype=jnp.float32))
           - jnp.exp(jnp.sum(lq2 * lk2, dtype=jnp.float32)) + lam_init)
    o_m = moba_attention(mq, mk_all, mv_all, pos)
    o_d = diff_attention(dq, dk_all, dv_all, pos, lam, lam_init, g_sub)
    o = jnp.concatenate([o_m.reshape(B, S, MOBA_W), o_d.reshape(B, S, DIFF_V_W)], axis=-1) @ w_out
    return x + rms_norm(o, g_post), mk, mv, dk, dv


def _lin_comb(left, right):
    a_l, b_l = left
    a_r, b_r = right
    return a_l * a_r, a_r * b_l + b_r


def rglru_block(x, conv_buf, h0, g_pre, w_in, conv_w, conv_b, w_a, b_a, w_x, b_x, lam, w_out, g_post):
    B, S, _ = x.shape
    u = rms_norm(x, g_pre) @ w_in
    gate, xb = jnp.split(u, 2, axis=-1)
    if conv_buf is None:
        conv_buf = jnp.zeros((B, CONV_W - 1, D_RNN), x.dtype)
    xpad = jnp.concatenate([conv_buf.astype(x.dtype), xb], axis=1)
    xc = conv_b + xpad[:, 0:S] * conv_w[0]
    for i in range(1, CONV_W):
        xc = xc + xpad[:, i:i + S] * conv_w[i]
    new_buf = xpad[:, S:]
    xr = xc.reshape(B, S, RNN_BLOCKS, RNN_BLOCK_W)
    r = jax.nn.sigmoid((jnp.einsum('bsnc,ncd->bsnd', xr, w_a) + b_a).astype(jnp.float32)).reshape(B, S, D_RNN)
    ig = jax.nn.sigmoid((jnp.einsum('bsnc,ncd->bsnd', xr, w_x) + b_x).astype(jnp.float32)).reshape(B, S, D_RNN)
    log_a = LRU_C * r * jax.nn.log_sigmoid(lam.astype(jnp.float32))
    a = jnp.exp(log_a)
    b = jnp.sqrt(-jnp.expm1(2.0 * log_a)) * (ig * xc.astype(jnp.float32))
    if h0 is not None:
        b = b.at[:, 0].add(a[:, 0] * h0.astype(jnp.float32))
    _, h = lax.associative_scan(_lin_comb, (a, b), axis=1)
    y = (h * jax.nn.gelu(gate.astype(jnp.float32))).astype(x.dtype) @ w_out
    return x + rms_norm(y, g_post), new_buf, h[:, -1].astype(x.dtype)


def mlp_block(x, g_pre, w_up, w_down, g_post):
    hdn = jnp.square(jax.nn.relu(rms_norm(x, g_pre) @ w_up))
    return x + rms_norm(hdn @ w_down, g_post)


def setup_inputs(seed: int = 0) -> dict:
    key = jax.random.key(seed)
    ks = iter(jax.random.split(key, 64))
    f32 = jnp.float32
    n_pages = PAST_LEN // PAGE_SIZE
    n_used = DEC_BATCH * n_pages
    n_pool = n_used + (n_used + 3) // 4

    def nrm(shape, scale=1.0):
        return jax.random.normal(next(ks), shape, f32) * scale

    def gain(shape):
        return 1.0 + nrm(shape, 0.05)

    NA, NR = N_ATTN_LAYERS, N_RNN_LAYERS
    inp = {}
    inp["x_prompt"] = nrm((BATCH, SEQ, D_MODEL))
    inp["x_sample"] = nrm((DEC_BATCH, DEC_SEQ, D_MODEL))
    inp["cache_moba_k"] = nrm((NA, n_pool, PAGE_SIZE, MOBA_HEADS, HEAD_DIM))
    inp["cache_moba_v"] = nrm((NA, n_pool, PAGE_SIZE, MOBA_HEADS, HEAD_DIM))
    inp["cache_diff_k"] = nrm((NA, n_pool, PAGE_SIZE, DIFF_HEADS, 2, HEAD_DIM))
    inp["cache_diff_v"] = nrm((NA, n_pool, PAGE_SIZE, DIFF_HEADS, DIFF_DIM))
    inp["state_conv"] = nrm((NR, DEC_BATCH, CONV_W - 1, D_RNN))
    inp["state_rnn"] = nrm((NR, DEC_BATCH, D_RNN), 0.5)
    inp["page_table"] = jax.random.permutation(next(ks), n_pool)[:n_used].reshape(DEC_BATCH, n_pages).astype(jnp.int32)
    inp["attn_g_pre"] = gain((NA, D_MODEL))
    inp["attn_w_in"] = nrm((NA, D_MODEL, ATTN_IN_W), D_MODEL ** -0.5)
    inp["diff_lambda_q1"] = nrm((NA, HEAD_DIM), 0.1)
    inp["diff_lambda_k1"] = nrm((NA, HEAD_DIM), 0.1)
    inp["diff_lambda_q2"] = nrm((NA, HEAD_DIM), 0.1)
    inp["diff_lambda_k2"] = nrm((NA, HEAD_DIM), 0.1)
    inp["diff_g_sub"] = gain((NA, DIFF_DIM))
    inp["attn_w_out"] = nrm((NA, ATTN_OUT_W, D_MODEL), ATTN_OUT_W ** -0.5)
    inp["attn_g_post"] = gain((NA, D_MODEL))
    inp["rnn_g_pre"] = gain((NR, D_MODEL))
    inp["rnn_w_in"] = nrm((NR, D_MODEL, 2 * D_RNN), D_MODEL ** -0.5)
    inp["rnn_conv_w"] = nrm((NR, CONV_W, D_RNN), CONV_W ** -0.5)
    inp["rnn_conv_b"] = nrm((NR, D_RNN), 0.1)
    inp["rnn_w_a"] = nrm((NR, RNN_BLOCKS, RNN_BLOCK_W, RNN_BLOCK_W), RNN_BLOCK_W ** -0.5)
    inp["rnn_b_a"] = nrm((NR, RNN_BLOCKS, RNN_BLOCK_W), 0.1)
    inp["rnn_w_x"] = nrm((NR, RNN_BLOCKS, RNN_BLOCK_W, RNN_BLOCK_W), RNN_BLOCK_W ** -0.5)
    inp["rnn_b_x"] = nrm((NR, RNN_BLOCKS, RNN_BLOCK_W), 0.1)
    a_base = jax.random.uniform(next(ks), (NR, D_RNN), f32, 0.9, 0.999)
    p = a_base ** (1.0 / LRU_C)
    inp["rnn_lambda"] = jnp.log(p) - jnp.log1p(-p)
    inp["rnn_w_out"] = nrm((NR, D_RNN, D_MODEL), D_RNN ** -0.5)
    inp["rnn_g_post"] = gain((NR, D_MODEL))
    inp["mlp_g_pre"] = gain((DEPTH, D_MODEL))
    inp["mlp_w_up"] = nrm((DEPTH, D_MODEL, D_FF), D_MODEL ** -0.5)
    inp["mlp_w_down"] = nrm((DEPTH, D_FF, D_MODEL), D_FF ** -0.5)
    inp["mlp_g_post"] = gain((DEPTH, D_MODEL))
    return inp


def reference(x_prompt, x_sample, cache_moba_k, cache_moba_v, cache_diff_k, cache_diff_v,
              state_conv, state_rnn, page_table,
              attn_g_pre, attn_w_in, diff_lambda_q1, diff_lambda_k1, diff_lambda_q2, diff_lambda_k2,
              diff_g_sub, attn_w_out, attn_g_post,
              rnn_g_pre, rnn_w_in, rnn_conv_w, rnn_conv_b, rnn_w_a, rnn_b_a, rnn_w_x, rnn_b_x,
              rnn_lambda, rnn_w_out, rnn_g_post,
              mlp_g_pre, mlp_w_up, mlp_w_down, mlp_g_post):
    past_len = page_table.shape[1] * cache_moba_k.shape[2]
    pos_p = jnp.arange(x_prompt.shape[1], dtype=jnp.int32)
    pos_s = past_len + jnp.arange(x_sample.shape[1], dtype=jnp.int32)

    def gather_past(cache, li):
        rows = cache[li, page_table]
        return rows.reshape((rows.shape[0], past_len) + rows.shape[3:])

    xp, xs = x_prompt, x_sample
    pm_k, pm_v, pd_k, pd_v, p_conv, p_rnn = [], [], [], [], [], []
    sm_k, sm_v, sd_k, sd_v, s_conv, s_rnn = [], [], [], [], [], []
    for layer in range(DEPTH):
        li = layer // 2
        if layer % 2 == 0:
            lam_init = 0.8 - 0.6 * math.exp(-0.3 * layer)
            w = (attn_g_pre[li], attn_w_in[li], diff_lambda_q1[li], diff_lambda_k1[li],
                 diff_lambda_q2[li], diff_lambda_k2[li], diff_g_sub[li], attn_w_out[li], attn_g_post[li])
            xp, mk, mv, dk, dv = attn_block(xp, pos_p, None, *w, lam_init)
            pm_k.append(mk); pm_v.append(mv); pd_k.append(dk); pd_v.append(dv)
            past = (gather_past(cache_moba_k, li), gather_past(cache_moba_v, li),
                    gather_past(cache_diff_k, li), gather_past(cache_diff_v, li))
            xs, mk, mv, dk, dv = attn_block(xs, pos_s, past, *w, lam_init)
            sm_k.append(mk); sm_v.append(mv); sd_k.append(dk); sd_v.append(dv)
        else:
            w = (rnn_g_pre[li], rnn_w_in[li], rnn_conv_w[li], rnn_conv_b[li], rnn_w_a[li], rnn_b_a[li],
                 rnn_w_x[li], rnn_b_x[li], rnn_lambda[li], rnn_w_out[li], rnn_g_post[li])
            xp, cbuf, hlast = rglru_block(xp, None, None, *w)
            p_conv.append(cbuf); p_rnn.append(hlast)
            xs, cbuf, hlast = rglru_block(xs, state_conv[li], state_rnn[li], *w)
            s_conv.append(cbuf); s_rnn.append(hlast)
        xp = mlp_block(xp, mlp_g_pre[layer], mlp_w_up[layer], mlp_w_down[layer], mlp_g_post[layer])
        xs = mlp_block(xs, mlp_g_pre[layer], mlp_w_up[layer], mlp_w_down[layer], mlp_g_post[layer])

    new_moba_k_prompt = jnp.stack(pm_k)
    new_moba_v_prompt = jnp.stack(pm_v)
    new_diff_k_prompt = jnp.stack(pd_k)
    new_diff_v_prompt = jnp.stack(pd_v)
    new_conv_prompt = jnp.stack(p_conv)
    new_rnn_prompt = jnp.stack(p_rnn)
    new_moba_k_sample = jnp.stack(sm_k)
    new_moba_v_sample = jnp.stack(sm_v)
    new_diff_k_sample = jnp.stack(sd_k)
    new_diff_v_sample = jnp.stack(sd_v)
    new_conv_sample = jnp.stack(s_conv)
    new_rnn_sample = jnp.stack(s_rnn)
    return (xp, xs,
            new_moba_k_prompt, new_moba_v_prompt, new_diff_k_prompt, new_diff_v_prompt,
            new_conv_prompt, new_rnn_prompt,
            new_moba_k_sample, new_moba_v_sample, new_diff_k_sample, new_diff_v_sample,
            new_conv_sample, new_rnn_sample)
```

```python
import functools
import math

import jax
import jax.numpy as jnp
from jax import lax
from jax.experimental import pallas as pl
from jax.experimental.pallas import tpu as pltpu

F32 = jnp.float32
BF16 = jnp.bfloat16
HIGHEST = lax.Precision.HIGHEST

HEAD_DIM = 64
ROT_DIM = HEAD_DIM // 4
ROPE_THETA = 500000.0
MOBA_HEADS = 8
MOBA_BLOCK = 256
MOBA_TOPK = 3
DIFF_HEADS = 4
DIFF_DIM = 2 * HEAD_DIM
RNN_BLOCKS = 4
CONV_W = 4
LRU_C = 8.0
EPS = 1e-6
LANES = 128
NEG = -1e30
SCALE = HEAD_DIM ** -0.5
NT_DIMS = (((1,), (1,)), ((), ()))
VMEM_LIMIT = 52 * 1024 * 1024


def _cparams(sem):
    return pltpu.CompilerParams(dimension_semantics=sem, vmem_limit_bytes=VMEM_LIMIT)


def _rms(x, g):
    ms = jnp.mean(x * x, axis=-1, keepdims=True)
    return x * lax.rsqrt(ms + EPS) * g


def _rope_tables(pos):
    half = ROT_DIM // 2
    inv = jnp.exp(-math.log(ROPE_THETA) * jnp.arange(half, dtype=F32) * (2.0 / ROT_DIM))
    ang = pos.astype(F32)[:, None] * inv[None, :]
    cos, sin = jnp.cos(ang), jnp.sin(ang)
    n = pos.shape[0]
    pad = jnp.zeros((n, HEAD_DIM - ROT_DIM), F32)
    c64 = jnp.concatenate([cos, cos, pad + 1.0], axis=1)
    sa64 = jnp.concatenate([-sin, jnp.zeros((n, half), F32), pad], axis=1)
    sb64 = jnp.concatenate([jnp.zeros((n, half), F32), sin, pad], axis=1)
    rep = LANES // HEAD_DIM
    return tuple(jnp.tile(t, (1, rep)) for t in (c64, sa64, sb64))


def _in_proj_body(x_ref, g_ref, w_ref, c_ref, sa_ref, sb_ref, *rest, n_out, rope_flags, tn):
    outs = rest[:n_out]
    xn_sc = rest[n_out]
    j = pl.program_id(1)

    @pl.when(j == 0)
    def _():
        xn_sc[...] = _rms(x_ref[...], g_ref[...]).astype(BF16)

    u = jnp.dot(xn_sc[...], w_ref[...], preferred_element_type=F32)
    for jj in range(n_out):
        @pl.when(j == jj)
        def _(jj=jj):
            if rope_flags[jj]:
                c, sa, sb = c_ref[...], sa_ref[...], sb_ref[...]
                for k in range(tn // LANES):
                    uk = u[:, k * LANES:(k + 1) * LANES]
                    outs[jj][:, k * LANES:(k + 1) * LANES] = (
                        uk * c + pltpu.roll(uk, LANES - ROT_DIM // 2, 1) * sa
                        + pltpu.roll(uk, ROT_DIM // 2, 1) * sb)
            else:
                outs[jj][...] = u


def _in_proj(x, g, w_bf, tables, rope_flags, tm):
    m, d = x.shape
    n_out = len(rope_flags)
    tn = w_bf.shape[1] // n_out
    npos = tables[0].shape[0] // tm
    tab_spec = pl.BlockSpec((tm, LANES), lambda i, j: (i % npos, 0))
    return pl.pallas_call(
        functools.partial(_in_proj_body, n_out=n_out, rope_flags=rope_flags, tn=tn),
        grid=(m // tm, n_out),
        in_specs=[pl.BlockSpec((tm, d), lambda i, j: (i, 0)),
                  pl.BlockSpec((1, d), lambda i, j: (0, 0)),
                  pl.BlockSpec((d, tn), lambda i, j: (0, j)),
                  tab_spec, tab_spec, tab_spec],
        out_specs=[pl.BlockSpec((tm, tn), lambda i, j: (i, 0))] * n_out,
        out_shape=[jax.ShapeDtypeStruct((m, tn), F32)] * n_out,
        scratch_shapes=[pltpu.VMEM((tm, d), BF16)],
        compiler_params=_cparams(("arbitrary", "arbitrary")),
        name="in_proj",
    )(x, g.reshape(1, d), w_bf, *tables)


def _attn_out_body(om_ref, od_ref, w_ref, g_ref, x_ref, y_ref):
    km = om_ref.shape[1]
    o = jnp.dot(om_ref[...].astype(BF16), w_ref[:km, :], preferred_element_type=F32)
    o = o + jnp.dot(od_ref[...].astype(BF16), w_ref[km:, :], preferred_element_type=F32)
    y_ref[...] = x_ref[...] + _rms(o, g_ref[...])


def _attn_out(om, od, w_bf, g, x, tm):
    m, d = x.shape
    km, kd = om.shape[1], od.shape[1]
    return pl.pallas_call(
        _attn_out_body,
        grid=(m // tm,),
        in_specs=[pl.BlockSpec((tm, km), lambda i: (i, 0)),
                  pl.BlockSpec((tm, kd), lambda i: (i, 0)),
                  pl.BlockSpec((km + kd, d), lambda i: (0, 0)),
                  pl.BlockSpec((1, d), lambda i: (0, 0)),
                  pl.BlockSpec((tm, d), lambda i: (i, 0))],
        out_specs=pl.BlockSpec((tm, d), lambda i: (i, 0)),
        out_shape=jax.ShapeDtypeStruct((m, d), F32),
        compiler_params=_cparams(("arbitrary",)),
        name="attn_out",
    )(om, od, w_bf, g.reshape(1, d), x)


def _mlp_body(x_ref, g1_ref, wu_ref, wd_ref, g2_ref, y_ref, xn_sc, acc_sc):
    f = pl.program_id(1)

    @pl.when(f == 0)
    def _():
        xn_sc[...] = _rms(x_ref[...], g1_ref[...]).astype(BF16)
        acc_sc[...] = jnp.zeros_like(acc_sc)

    h = jnp.dot(xn_sc[...], wu_ref[...], preferred_element_type=F32)
    h = jnp.square(jnp.maximum(h, 0.0))
    acc_sc[...] += jnp.dot(h.astype(BF16), wd_ref[...], preferred_element_type=F32)

    @pl.when(f == pl.num_programs(1) - 1)
    def _():
        y_ref[...] = x_ref[...] + _rms(acc_sc[...], g2_ref[...])


def _mlp(x, g1, wu_bf, wd_bf, g2, tm, tf):
    m, d = x.shape
    ff = wu_bf.shape[1]
    return pl.pallas_call(
        _mlp_body,
        grid=(m // tm, ff // tf),
        in_specs=[pl.BlockSpec((tm, d), lambda i, f: (i, 0)),
                  pl.BlockSpec((1, d), lambda i, f: (0, 0)),
                  pl.BlockSpec((d, tf), lambda i, f: (0, f)),
                  pl.BlockSpec((tf, d), lambda i, f: (f, 0)),
                  pl.BlockSpec((1, d), lambda i, f: (0, 0))],
        out_specs=pl.BlockSpec((tm, d), lambda i, f: (i, 0)),
        out_shape=jax.ShapeDtypeStruct((m, d), F32),
        scratch_shapes=[pltpu.VMEM((tm, d), BF16), pltpu.VMEM((tm, d), F32)],
        compiler_params=_cparams(("arbitrary", "arbitrary")),
        name="mlp",
    )(x, g1.reshape(1, d), wu_bf, wd_bf, g2.reshape(1, d))


def _softmax_block(st, m, l, acc, vt):
    m_new = jnp.maximum(m, jnp.max(st, axis=0, keepdims=True))
    alpha = jnp.exp(m - m_new)
    p = jnp.exp(st - m_new)
    l = alpha * l + jnp.sum(p, axis=0, keepdims=True)
    acc = alpha * acc + jnp.dot(vt, p.astype(BF16), preferred_element_type=F32)
    return m_new, l, acc


def _first_block(st, vt):
    m = jnp.max(st, axis=0, keepdims=True)
    p = jnp.exp(st - m)
    l = jnp.sum(p, axis=0, keepdims=True)
    acc = jnp.dot(vt, p.astype(BF16), preferred_element_type=F32)
    return m, l, acc


def _causal_mask(st):
    kidx = lax.broadcasted_iota(jnp.int32, st.shape, 0)
    qidx = lax.broadcasted_iota(jnp.int32, st.shape, 1)
    return jnp.where(kidx <= qidx, st, NEG)


def _moba_body(q_ref, k_ref, v_ref, o_ref, kmean_sc, kh_sc, vt_sc, sb_sc, ot_sc, *, nb):
    qi = pl.program_id(1)
    blk = MOBA_BLOCK
    hd = HEAD_DIM

    @pl.when(qi == 0)
    def _prep():
        for n in range(nb):
            kblk = k_ref[n * blk:(n + 1) * blk, :]
            kmean_sc[n:n + 1, :] = jnp.sum(kblk, axis=0, keepdims=True) * (1.0 / blk)
            vtb = v_ref[n * blk:(n + 1) * blk, :].T
            for h in range(MOBA_HEADS):
                kh_sc[h, n] = kblk[:, h * hd:(h + 1) * hd].astype(BF16)
                vt_sc[h, n] = vtb[h * hd:(h + 1) * hd, :].astype(BF16)

    k_eff = min(MOBA_TOPK, nb)
    for h in range(MOBA_HEADS):
        hs = slice(h * hd, (h + 1) * hd)
        qh = q_ref[:, hs]
        gt = lax.dot_general(kmean_sc[:, hs], qh, NT_DIMS, precision=HIGHEST,
                             preferred_element_type=F32)
        rid = lax.broadcasted_iota(jnp.int32, gt.shape, 0)
        valid = rid < qi
        g = jnp.where(valid, gt, -jnp.inf)
        rank = jnp.zeros(gt.shape, jnp.int32)
        for mm in range(nb):
            gm = g[mm:mm + 1, :]
            beats = (gm > g) | ((gm == g) & (rid > mm))
            rank = rank + beats.astype(jnp.int32)
        sel = valid & (rank < k_eff)
        sb_sc[...] = jnp.where(sel, 0.0, NEG)

        qs = (qh * SCALE).astype(BF16)
        st = lax.dot_general(kh_sc[h, qi], qs, NT_DIMS, preferred_element_type=F32)
        carry = _first_block(_causal_mask(st), vt_sc[h, qi])

        def body(n, carry, h=h, qs=qs):
            st = lax.dot_general(kh_sc[h, n], qs, NT_DIMS, preferred_element_type=F32)
            st = st + sb_sc[pl.ds(n, 1), :]
            return _softmax_block(st, *carry, vt_sc[h, n])

        m, l, acc = lax.fori_loop(0, qi, body, carry)
        ot_sc[hs, :] = acc / l
    o_ref[...] = ot_sc[...].T


def _moba_prompt(q, k, v, batch, seq):
    w = q.shape[1]
    blk = MOBA_BLOCK
    nb = seq // blk
    return pl.pallas_call(
        functools.partial(_moba_body, nb=nb),
        grid=(batch, nb),
        in_specs=[pl.BlockSpec((blk, w), lambda b, i: (b * nb + i, 0)),
                  pl.BlockSpec((seq, w), lambda b, i: (b, 0)),
                  pl.BlockSpec((seq, w), lambda b, i: (b, 0))],
        out_specs=pl.BlockSpec((blk, w), lambda b, i: (b * nb + i, 0)),
        out_shape=jax.ShapeDtypeStruct(q.shape, F32),
        scratch_shapes=[pltpu.VMEM((nb, w), F32),
                        pltpu.VMEM((MOBA_HEADS, nb, blk, HEAD_DIM), BF16),
                        pltpu.VMEM((MOBA_HEADS, nb, HEAD_DIM, blk), BF16),
                        pltpu.VMEM((nb, blk), F32),
                        pltpu.VMEM((w, blk), F32)],
        compiler_params=_cparams(("arbitrary", "arbitrary")),
        name="moba_prompt",
    )(q, k, v)


def _diff_lambda(lam_ref, lam_init):
    lv = lam_ref[...]
    s1 = jnp.sum(lv[0:1, :] * lv[1:2, :], axis=-1, keepdims=True)
    s2 = jnp.sum(lv[2:3, :] * lv[3:4, :], axis=-1, keepdims=True)
    return jnp.exp(s1) - jnp.exp(s2) + lam_init


def _diff_body(q_ref, k_ref, v_ref, lam_ref, gsub_ref, o_ref, kh_sc, vt_sc, ot_sc, *, nb, tq, lam_init):
    qi = pl.program_id(1)
    hd = HEAD_DIM

    @pl.when(qi == 0)
    def _prep():
        for n in range(nb):
            kblk = k_ref[n * tq:(n + 1) * tq, :]
            vtb = v_ref[n * tq:(n + 1) * tq, :].T
            for r in range(2 * DIFF_HEADS):
                kh_sc[r, n] = kblk[:, r * hd:(r + 1) * hd].astype(BF16)
            for h in range(DIFF_HEADS):
                vt_sc[h, n] = vtb[h * DIFF_DIM:(h + 1) * DIFF_DIM, :].astype(BF16)

    lam = _diff_lambda(lam_ref, lam_init)
    for h in range(DIFF_HEADS):
        qs = [(q_ref[:, (2 * h + c) * hd:(2 * h + c + 1) * hd] * SCALE).astype(BF16) for c in range(2)]
        vtd = vt_sc[h, qi]
        carry = ()
        for c in range(2):
            st = lax.dot_general(kh_sc[2 * h + c, qi], qs[c], NT_DIMS, preferred_element_type=F32)
            carry += _first_block(_causal_mask(st), vtd)

        def body(n, carry, h=h, qs=qs):
            vt = vt_sc[h, n]
            out = ()
            for c in range(2):
                st = lax.dot_general(kh_sc[2 * h + c, n], qs[c], NT_DIMS, preferred_element_type=F32)
                out += _softmax_block(st, *carry[3 * c:3 * c + 3], vt)
            return out

        m0, l0, a0, m1, l1, a1 = lax.fori_loop(0, qi, body, carry)
        od = a0 / l0 - lam * (a1 / l1)
        ms = jnp.mean(od * od, axis=0, keepdims=True)
        od = od * lax.rsqrt(ms + EPS) * gsub_ref[...]
        ot_sc[h * DIFF_DIM:(h + 1) * DIFF_DIM, :] = od * (1.0 - lam_init)
    o_ref[...] = ot_sc[...].T


def _diff_prompt(q, k, v, lam4, gsub, batch, seq, lam_init, tq=256):
    w = q.shape[1]
    nb = seq // tq
    return pl.pallas_call(
        functools.partial(_diff_body, nb=nb, tq=tq, lam_init=lam_init),
        grid=(batch, nb),
        in_specs=[pl.BlockSpec((tq, w), lambda b, i: (b * nb + i, 0)),
                  pl.BlockSpec((seq, w), lambda b, i: (b, 0)),
                  pl.BlockSpec((seq, w), lambda b, i: (b, 0)),
                  pl.BlockSpec((4, HEAD_DIM), lambda b, i: (0, 0)),
                  pl.BlockSpec((DIFF_DIM, 1), lambda b, i: (0, 0))],
        out_specs=pl.BlockSpec((tq, w), lambda b, i: (b * nb + i, 0)),
        out_shape=jax.ShapeDtypeStruct(q.shape, F32),
        scratch_shapes=[pltpu.VMEM((2 * DIFF_HEADS, nb, tq, HEAD_DIM), BF16),
                        pltpu.VMEM((DIFF_HEADS, nb, DIFF_DIM, tq), BF16),
                        pltpu.VMEM((w, tq), F32)],
        compiler_params=_cparams(("arbitrary", "arbitrary")),
        name="diff_prompt",
    )(q, k, v, lam4, gsub.reshape(DIFF_DIM, 1))


def _gelu_tanh(x):
    return 0.5 * x * (1.0 + jnp.tanh(math.sqrt(2.0 / math.pi) * (x + 0.044715 * (x * x * x))))


def _log_sigmoid(x):
    return jnp.minimum(x, 0.0) - jnp.log1p(jnp.exp(-jnp.abs(x)))


def _lru_gates(xc, wa_ref, ba_ref, wx_ref, bx_ref, lam_ref):
    bw = wa_ref.shape[1]
    xcb = xc.astype(BF16)
    ra, rx = [], []
    for n in range(RNN_BLOCKS):
        xn = xcb[:, n * bw:(n + 1) * bw]
        ra.append(jnp.dot(xn, wa_ref[n], preferred_element_type=F32))
        rx.append(jnp.dot(xn, wx_ref[n], preferred_element_type=F32))
    r = jax.nn.sigmoid(jnp.concatenate(ra, axis=1) + ba_ref[...])
    ig = jax.nn.sigmoid(jnp.concatenate(rx, axis=1) + bx_ref[...])
    log_a = LRU_C * r * _log_sigmoid(lam_ref[...])
    a = jnp.exp(log_a)
    b = jnp.sqrt(-jnp.tanh(log_a) * (1.0 + a * a)) * (ig * xc)
    return a, b


def _rglru_body(gate_ref, xb_ref, x_ref, cw_ref, cb_ref, wa_ref, ba_ref, wx_ref, bx_ref, lam_ref,
                wo_ref, g_ref, y_ref, buf_ref, hl_ref, xpad_sc, a_sc, h_sc, hc_sc, *, tc):
    c = pl.program_id(1)
    hist = CONV_W - 1

    @pl.when(c == 0)
    def _():
        xpad_sc[0:8, :] = jnp.zeros((8, xpad_sc.shape[1]), F32)
        hc_sc[...] = jnp.zeros_like(hc_sc)

    @pl.when(c > 0)
    def _():
        xpad_sc[0:8, :] = xpad_sc[tc:tc + 8, :]

    xpad_sc[8:8 + tc, :] = xb_ref[...]
    xc = cb_ref[...] + xpad_sc[8 - hist:8 - hist + tc, :] * cw_ref[0:1, :]
    for i in range(1, CONV_W):
        xc = xc + xpad_sc[8 - hist + i:8 - hist + i + tc, :] * cw_ref[i:i + 1, :]
    a, b = _lru_gates(xc, wa_ref, ba_ref, wx_ref, bx_ref, lam_ref)
    a_sc[...] = a
    h_sc[...] = b

    def step(t, h):
        h = a_sc[pl.ds(t, 1), :] * h + h_sc[pl.ds(t, 1), :]
        h_sc[pl.ds(t, 1), :] = h
        return h

    h = lax.fori_loop(0, tc, step, hc_sc[...], unroll=8)
    hc_sc[...] = h
    y = (h_sc[...] * _gelu_tanh(gate_ref[...])).astype(BF16)
    y = jnp.dot(y, wo_ref[...], preferred_element_type=F32)
    y_ref[...] = x_ref[...] + _rms(y, g_ref[...])

    @pl.when(c == pl.num_programs(1) - 1)
    def _():
        buf_ref[...] = xpad_sc[8 + tc - hist:8 + tc, :]
        hl_ref[...] = h


def _rglru_prompt(gate, xb, x, cw, cb, wa_bf, ba, wx_bf, bx, lam, wo_bf, g, batch, seq, tc=256):
    m, d = x.shape
    nc = seq // tc
    bw = d // RNN_BLOCKS
    row = lambda b, c: (b * nc + c, 0)
    const2 = lambda b, c: (0, 0)
    const3 = lambda b, c: (0, 0, 0)
    vec = pl.BlockSpec((1, d), const2)
    return pl.pallas_call(
        functools.partial(_rglru_body, tc=tc),
        grid=(batch, nc),
        in_specs=[pl.BlockSpec((tc, d), row), pl.BlockSpec((tc, d), row), pl.BlockSpec((tc, d), row),
                  pl.BlockSpec((CONV_W, d), const2), vec,
                  pl.BlockSpec((RNN_BLOCKS, bw, bw), const3), vec,
                  pl.BlockSpec((RNN_BLOCKS, bw, bw), const3), vec, vec,
                  pl.BlockSpec((d, d), const2), vec],
        out_specs=[pl.BlockSpec((tc, d), row),
                   pl.BlockSpec((None, CONV_W - 1, d), lambda b, c: (b, 0, 0)),
                   pl.BlockSpec((None, 1, d), lambda b, c: (b, 0, 0))],
        out_shape=[jax.ShapeDtypeStruct((m, d), F32),
                   jax.ShapeDtypeStruct((batch, CONV_W - 1, d), F32),
                   jax.ShapeDtypeStruct((batch, 1, d), F32)],
        scratch_shapes=[pltpu.VMEM((tc + 8, d), F32), pltpu.VMEM((tc, d), F32),
                        pltpu.VMEM((tc, d), F32), pltpu.VMEM((1, d), F32)],
        compiler_params=_cparams(("arbitrary", "arbitrary")),
        name="rglru_prompt",
    )(gate, xb, x, cw, cb.reshape(1, d), wa_bf, ba.reshape(1, d), wx_bf, bx.reshape(1, d),
      lam.reshape(1, d), wo_bf, g.reshape(1, d))


def _rglru_step_body(gate_ref, xb_ref, x_ref, conv_ref, h0_ref, cw_ref, cb_ref, wa_ref, ba_ref,
                     wx_ref, bx_ref, lam_ref, wo_ref, g_ref, y_ref, h_ref):
    xc = cb_ref[...] + xb_ref[...] * cw_ref[CONV_W - 1:CONV_W, :]
    for i in range(CONV_W - 1):
        xc = xc + conv_ref[i] * cw_ref[i:i + 1, :]
    a, b = _lru_gates(xc, wa_ref, ba_ref, wx_ref, bx_ref, lam_ref)
    h = a * h0_ref[...] + b
    h_ref[...] = h
    y = (h * _gelu_tanh(gate_ref[...])).astype(BF16)
    y = jnp.dot(y, wo_ref[...], preferred_element_type=F32)
    y_ref[...] = x_ref[...] + _rms(y, g_ref[...])


def _rglru_step(gate, xb, x, conv_t, h0, cw, cb, wa_bf, ba, wx_bf, bx, lam, wo_bf, g):
    m, d = x.shape
    return pl.pallas_call(
        _rglru_step_body,
        out_shape=[jax.ShapeDtypeStruct((m, d), F32), jax.ShapeDtypeStruct((m, d), F32)],
        compiler_params=pltpu.CompilerParams(vmem_limit_bytes=VMEM_LIMIT),
        name="rglru_step",
    )(gate, xb, x, conv_t, h0, cw, cb.reshape(1, d), wa_bf, ba.reshape(1, d), wx_bf, bx.reshape(1, d),
      lam.reshape(1, d), wo_bf, g.reshape(1, d))


def _kmean_body(pt_ref, *refs, pp, ppb):
    del pt_ref
    k_refs, o_ref = refs[:pp], refs[pp]
    rows = ppb * k_refs[0].shape[0]
    for j in range(pp // ppb):
        s = jnp.sum(k_refs[ppb * j][...], axis=0, keepdims=True)
        for t in range(1, ppb):
            s = s + jnp.sum(k_refs[ppb * j + t][...], axis=0, keepdims=True)
        o_ref[j:j + 1, :] = s * (1.0 / rows)


def _moba_kmean(cache_k, pt_flat, li, batch, n_pages, pp=16):
    _, _, page, w = cache_k.shape
    ppb = MOBA_BLOCK // page
    nblk = n_pages // ppb
    in_specs = [pl.BlockSpec((None, None, page, w),
                             functools.partial(lambda b, s, pt, j: (li, pt[b * n_pages + s * pp + j], 0, 0), j=j))
                for j in range(pp)]
    return pl.pallas_call(
        functools.partial(_kmean_body, pp=pp, ppb=ppb),
        grid_spec=pltpu.PrefetchScalarGridSpec(
            num_scalar_prefetch=1, grid=(batch, n_pages // pp), in_specs=in_specs,
            out_specs=pl.BlockSpec((None, pp // ppb, w), lambda b, s, pt: (b, s, 0))),
        out_shape=jax.ShapeDtypeStruct((batch, nblk, w), F32),
        compiler_params=_cparams(("arbitrary", "arbitrary")),
        name="moba_kmean",
    )(pt_flat, *([cache_k] * pp))


def _head_rows(row, n_groups, width):
    w = row.shape[1]
    rid = lax.broadcasted_iota(jnp.int32, (n_groups, w), 0)
    lid = lax.broadcasted_iota(jnp.int32, (n_groups, w), 1)
    return jnp.where(lid // width == rid, jnp.broadcast_to(row, (n_groups, w)), 0.0)


def _moba_select_body(q_ref, km_ref, sel_ref):
    qm = _head_rows(q_ref[...], MOBA_HEADS, HEAD_DIM)
    g = lax.dot_general(qm, km_ref[...], NT_DIMS, precision=HIGHEST, preferred_element_type=F32)
    nblk = g.shape[1]
    lid = lax.broadcasted_iota(jnp.int32, g.shape, 1)
    out = jnp.zeros(sel_ref.shape, jnp.int32)
    olid = lax.broadcasted_iota(jnp.int32, sel_ref.shape, 1)
    for t in range(MOBA_TOPK):
        mx = jnp.max(g, axis=1, keepdims=True)
        idx = jnp.min(jnp.where(g == mx, lid, nblk), axis=1, keepdims=True)
        out = jnp.where(olid == t, idx, out)
        g = jnp.where(lid == idx, -jnp.inf, g)
    sel_ref[...] = out


def _moba_select(q3, kmean):
    batch, _, w = q3.shape
    nblk = kmean.shape[1]
    return pl.pallas_call(
        _moba_select_body,
        grid=(batch,),
        in_specs=[pl.BlockSpec((None, 1, w), lambda b: (b, 0, 0)),
                  pl.BlockSpec((None, nblk, w), lambda b: (b, 0, 0))],
        out_specs=pl.BlockSpec((None, MOBA_HEADS, LANES), lambda b: (b, 0, 0)),
        out_shape=jax.ShapeDtypeStruct((batch, MOBA_HEADS, LANES), jnp.int32),
        compiler_params=_cparams(("arbitrary",)),
        name="moba_select",
    )(q3, kmean)


def _moba_step_body(sel_ref, pt_ref, q_ref, kn_ref, vn_ref, *refs, nsl):
    del sel_ref, pt_ref
    k_refs, v_refs, o_ref = refs[:2 * nsl], refs[2 * nsl:4 * nsl], refs[4 * nsl]
    q = q_ref[...] * SCALE
    lid = lax.broadcasted_iota(jnp.int32, q.shape, 1)
    out = jnp.zeros(q.shape, F32)
    for hh in range(LANES // HEAD_DIM):
        mine = lid // HEAD_DIM == hh
        qh = jnp.where(mine, q, 0.0)
        q8 = jnp.broadcast_to(qh, (8, LANES)).astype(BF16)
        ks = jnp.concatenate([r[...] for r in k_refs[hh * nsl:(hh + 1) * nsl]], axis=0).astype(BF16)
        vs = jnp.concatenate([r[...] for r in v_refs[hh * nsl:(hh + 1) * nsl]], axis=0).astype(BF16)
        s = lax.dot_general(q8, ks, NT_DIMS, preferred_element_type=F32)[0:1, :]
        s_self = jnp.sum(qh * kn_ref[...], axis=-1, keepdims=True)
        m = jnp.maximum(jnp.max(s, axis=-1, keepdims=True), s_self)
        p = jnp.exp(s - m)
        p_self = jnp.exp(s_self - m)
        l = jnp.sum(p, axis=-1, keepdims=True) + p_self
        pv = jnp.dot(jnp.broadcast_to(p, (8, p.shape[1])).astype(BF16), vs,
                     preferred_element_type=F32)[0:1, :]
        out = jnp.where(mine, (pv + p_self * vn_ref[...]) / l, out)
    o_ref[...] = out


def _moba_step(q3, kn3, vn3, cache_k, cache_v, sel_flat, pt_flat, li, batch, n_pages):
    _, _, page, w = cache_k.shape
    ppb = MOBA_BLOCK // page
    nsl = MOBA_TOPK * ppb
    hpl = LANES // HEAD_DIM

    def slab_spec(hh, t, j):
        def imap(b, hp, sel, pt):
            blk = sel[(b * MOBA_HEADS + hp * hpl + hh) * MOBA_TOPK + t]
            return (li, pt[b * n_pages + blk * ppb + j], 0, hp)
        return pl.BlockSpec((None, None, page, LANES), imap)

    slabs = [slab_spec(hh, t, j) for hh in range(hpl) for t in range(MOBA_TOPK) for j in range(ppb)]
    row = pl.BlockSpec((None, 1, LANES), lambda b, hp, sel, pt: (b, 0, hp))
    return pl.pallas_call(
        functools.partial(_moba_step_body, nsl=nsl),
        grid_spec=pltpu.PrefetchScalarGridSpec(
            num_scalar_prefetch=2, grid=(batch, w // LANES),
            in_specs=[row, row, row] + slabs + slabs, out_specs=row),
        out_shape=jax.ShapeDtypeStruct((batch, 1, w), F32),
        compiler_params=_cparams(("arbitrary", "arbitrary")),
        name="moba_step",
    )(sel_flat, pt_flat, q3, kn3, vn3, *([cache_k] * len(slabs)), *([cache_v] * len(slabs)))


def _expand_mat(c, dtype):
    n, w = 2 * DIFF_HEADS, DIFF_HEADS * DIFF_DIM
    rid = lax.broadcasted_iota(jnp.int32, (n, w), 0)
    lid = lax.broadcasted_iota(jnp.int32, (n, w), 1)
    return (rid == 2 * (lid // DIFF_DIM) + c).astype(dtype)


def _diff_step_body(pt_ref, q_ref, kn_ref, vn_ref, lam_ref, gsub_ref, *refs, pp, lam_init):
    del pt_ref
    k_refs, v_refs = refs[:pp], refs[pp:2 * pp]
    o_ref, m_sc, l_sc, acc_sc = refs[2 * pp:2 * pp + 4]
    s_idx = pl.program_id(1)
    nrow = 2 * DIFF_HEADS
    page = k_refs[0].shape[0]
    w = q_ref.shape[1]

    @pl.when(s_idx == 0)
    def _():
        m_sc[...] = jnp.full(m_sc.shape, NEG, F32)
        l_sc[...] = jnp.zeros_like(l_sc)
        acc_sc[...] = jnp.zeros_like(acc_sc)

    qm = _head_rows(q_ref[...] * SCALE, nrow, HEAD_DIM)
    qmb = qm.astype(BF16)
    kb = jnp.concatenate([r[...].astype(BF16) for r in k_refs], axis=0)
    st = lax.dot_general(kb, qmb, NT_DIMS, preferred_element_type=F32)
    m_old = m_sc[...]
    m_new = jnp.maximum(m_old, jnp.max(st, axis=0, keepdims=True))
    alpha = jnp.exp(m_old - m_new)
    p = jnp.exp(st - m_new)
    l_sc[...] = alpha * l_sc[...] + jnp.sum(p, axis=0, keepdims=True)
    m_sc[...] = m_new
    pb = p.astype(BF16)
    alpha8 = jnp.broadcast_to(alpha, (8, nrow))
    for c in range(2):
        ax = jnp.dot(alpha8, _expand_mat(c, F32), precision=HIGHEST, preferred_element_type=F32)
        pe = jnp.dot(pb, _expand_mat(c, BF16), preferred_element_type=F32)
        part = jnp.zeros((8, w), F32)
        for j in range(pp):
            pv = pe[j * page:(j + 1) * page, :] * v_refs[j][...]
            part = part + jnp.sum(pv.reshape(page // 8, 8, w), axis=0)
        acc_sc[c] = ax * acc_sc[c] + part

    @pl.when(s_idx == pl.num_programs(1) - 1)
    def _():
        kn8 = jnp.broadcast_to(kn_ref[...], (8, w))
        s_self = lax.dot_general(kn8, qm, NT_DIMS, precision=HIGHEST,
                                 preferred_element_type=F32)[0:1, :]
        m_old = m_sc[...]
        m_new = jnp.maximum(m_old, s_self)
        alpha = jnp.exp(m_old - m_new)
        p_self = jnp.exp(s_self - m_new)
        l = alpha * l_sc[...] + p_self
        outs = []
        for c in range(2):
            e = _expand_mat(c, F32)
            expand = lambda r: jnp.dot(jnp.broadcast_to(r, (8, nrow)), e, precision=HIGHEST,
                                       preferred_element_type=F32)[0:1, :]
            acc = jnp.sum(acc_sc[c], axis=0, keepdims=True)
            outs.append((expand(alpha) * acc + expand(p_self) * vn_ref[...]) / expand(l))
        od = outs[0] - _diff_lambda(lam_ref, lam_init) * outs[1]
        for h in range(DIFF_HEADS):
            oh = od[:, h * DIFF_DIM:(h + 1) * DIFF_DIM]
            o_ref[:, h * DIFF_DIM:(h + 1) * DIFF_DIM] = _rms(oh, gsub_ref[...]) * (1.0 - lam_init)


def _diff_step(q3, kn3, vn3, lam4, gsub, cache_k, cache_v, pt_flat, li, batch, n_pages, lam_init, pp=8):
    _, _, page, w = cache_k.shape
    nrow = 2 * DIFF_HEADS

    def page_spec(j):
        return pl.BlockSpec((None, None, page, w),
                            lambda b, s, pt: (li, pt[b * n_pages + s * pp + j], 0, 0))

    pages = [page_spec(j) for j in range(pp)]
    row = pl.BlockSpec((None, 1, w), lambda b, s, pt: (b, 0, 0))
    return pl.pallas_call(
        functools.partial(_diff_step_body, pp=pp, lam_init=lam_init),
        grid_spec=pltpu.PrefetchScalarGridSpec(
            num_scalar_prefetch=1, grid=(batch, n_pages // pp),
            in_specs=[row, row, row,
                      pl.BlockSpec((4, HEAD_DIM), lambda b, s, pt: (0, 0)),
                      pl.BlockSpec((1, DIFF_DIM), lambda b, s, pt: (0, 0))] + pages + pages,
            out_specs=row,
            scratch_shapes=[pltpu.VMEM((1, nrow), F32), pltpu.VMEM((1, nrow), F32),
                            pltpu.VMEM((2, 8, w), F32)]),
        out_shape=jax.ShapeDtypeStruct((batch, 1, w), F32),
        compiler_params=_cparams(("arbitrary", "arbitrary")),
        name="diff_step",
    )(pt_flat, q3, kn3, vn3, lam4, gsub.reshape(1, DIFF_DIM), *([cache_k] * pp), *([cache_v] * pp))


def kernel(x_prompt, x_sample, cache_moba_k, cache_moba_v, cache_diff_k, cache_diff_v, state_conv, state_rnn, page_table, attn_g_pre, attn_w_in, diff_lambda_q1, diff_lambda_k1, diff_lambda_q2, diff_lambda_k2, diff_g_sub, attn_w_out, attn_g_post, rnn_g_pre, rnn_w_in, rnn_conv_w, rnn_conv_b, rnn_w_a, rnn_b_a, rnn_w_x, rnn_b_x, rnn_lambda, rnn_w_out, rnn_g_post, mlp_g_pre, mlp_w_up, mlp_w_down, mlp_g_post):
    batch, seq, d = x_prompt.shape
    dec_batch, dec_seq, _ = x_sample.shape
    assert dec_seq == 1
    depth = mlp_w_up.shape[0]
    na, n_pool, page = cache_moba_k.shape[:3]
    n_pages = page_table.shape[1]
    past_len = n_pages * page
    mw = MOBA_HEADS * HEAD_DIM
    dw = DIFF_HEADS * DIFF_DIM

    xp = x_prompt.reshape(batch * seq, d)
    xs = x_sample.reshape(dec_batch, d)
    tab_p = _rope_tables(jnp.arange(seq, dtype=jnp.int32))
    tab_s = _rope_tables(jnp.full((dec_batch,), past_len, jnp.int32))
    pt_flat = page_table.reshape(-1)
    cmk = cache_moba_k.reshape(na, n_pool, page, mw)
    cmv = cache_moba_v.reshape(na, n_pool, page, mw)
    cdk = cache_diff_k.reshape(na, n_pool, page, dw)
    cdv = cache_diff_v.reshape(na, n_pool, page, dw)

    tm_p = 512
    attn_rope = (True, True, False, True, True, False)
    outs = {k: [] for k in ("pmk", "pmv", "pdk", "pdv", "pconv", "prnn",
                            "smk", "smv", "sdk", "sdv", "sconv", "srnn")}
    for layer in range(depth):
        li = layer // 2
        if layer % 2 == 0:
            lam_init = 0.8 - 0.6 * math.exp(-0.3 * layer)
            w_in = attn_w_in[li].astype(BF16)
            w_out = attn_w_out[li].astype(BF16)
            lam4 = jnp.stack([diff_lambda_q1[li], diff_lambda_k1[li], diff_lambda_q2[li], diff_lambda_k2[li]])
            mq, mk, mv, dq, dk, dv = _in_proj(xp, attn_g_pre[li], w_in, tab_p, attn_rope, tm_p)
            o_m = _moba_prompt(mq, mk, mv, batch, seq)
            o_d = _diff_prompt(dq, dk, dv, lam4, diff_g_sub[li], batch, seq, lam_init)
            xp = _attn_out(o_m, o_d, w_out, attn_g_post[li], xp, tm_p)
            outs["pmk"].append(mk.reshape(batch, seq, MOBA_HEADS, HEAD_DIM))
            outs["pmv"].append(mv.reshape(batch, seq, MOBA_HEADS, HEAD_DIM))
            outs["pdk"].append(dk.reshape(batch, seq, DIFF_HEADS, 2, HEAD_DIM))
            outs["pdv"].append(dv.reshape(batch, seq, DIFF_HEADS, DIFF_DIM))
            mq, mk, mv, dq, dk, dv = _in_proj(xs, attn_g_pre[li], w_in, tab_s, attn_rope, dec_batch)
            r3 = lambda t: t.reshape(dec_batch, 1, t.shape[-1])
            kmean = _moba_kmean(cmk, pt_flat, li, dec_batch, n_pages)
            sel = _moba_select(r3(mq), kmean)[:, :, :MOBA_TOPK].reshape(-1)
            o_m = _moba_step(r3(mq), r3(mk), r3(mv), cmk, cmv, sel, pt_flat, li, dec_batch, n_pages)
            o_d = _diff_step(r3(dq), r3(dk), r3(dv), lam4, diff_g_sub[li], cdk, cdv, pt_flat, li,
                             dec_batch, n_pages, lam_init)
            xs = _attn_out(o_m.reshape(dec_batch, mw), o_d.reshape(dec_batch, dw), w_out,
                           attn_g_post[li], xs, dec_batch)
            outs["smk"].append(mk.reshape(dec_batch, 1, MOBA_HEADS, HEAD_DIM))
            outs["smv"].append(mv.reshape(dec_batch, 1, MOBA_HEADS, HEAD_DIM))
            outs["sdk"].append(dk.reshape(dec_batch, 1, DIFF_HEADS, 2, HEAD_DIM))
            outs["sdv"].append(dv.reshape(dec_batch, 1, DIFF_HEADS, DIFF_DIM))
        else:
            w_in = rnn_w_in[li].astype(BF16)
            wts = (rnn_conv_w[li], rnn_conv_b[li], rnn_w_a[li].astype(BF16), rnn_b_a[li].reshape(-1),
                   rnn_w_x[li].astype(BF16), rnn_b_x[li].reshape(-1), rnn_lambda[li],
                   rnn_w_out[li].astype(BF16), rnn_g_post[li])
            gate, xb = _in_proj(xp, rnn_g_pre[li], w_in, tab_p, (False, False), tm_p)
            xp, cbuf, hlast = _rglru_prompt(gate, xb, xp, *wts, batch, seq)
            outs["pconv"].append(cbuf)
            outs["prnn"].append(hlast.reshape(batch, d))
            gate, xb = _in_proj(xs, rnn_g_pre[li], w_in, tab_s, (False, False), dec_batch)
            conv_t = jnp.swapaxes(state_conv[li], 0, 1)
            xs, hnew = _rglru_step(gate, xb, xs, conv_t, state_rnn[li], *wts)
            outs["sconv"].append(jnp.concatenate([state_conv[li][:, 1:], xb[:, None, :]], axis=1))
            outs["srnn"].append(hnew)
        wu = mlp_w_up[layer].astype(BF16)
        wd = mlp_w_down[layer].astype(BF16)
        xp = _mlp(xp, mlp_g_pre[layer], wu, wd, mlp_g_post[layer], 1024, 512)
        xs = _mlp(xs, mlp_g_pre[layer], wu, wd, mlp_g_post[layer], dec_batch, 512)

    st = lambda k: jnp.stack(outs[k])
    return (xp.reshape(batch, seq, d), xs.reshape(dec_batch, 1, d),
            st("pmk"), st("pmv"), st("pdk"), st("pdv"), st("pconv"), st("prnn"),
            st("smk"), st("smv"), st("sdk"), st("sdv"), st("sconv"), st("srnn"))
```

```python
import functools
import math

import jax
import jax.numpy as jnp
from jax import lax
from jax.experimental import pallas as pl
from jax.experimental.pallas import tpu as pltpu

F32 = jnp.float32
BF16 = jnp.bfloat16
HIGHEST = lax.Precision.HIGHEST

HEAD_DIM = 64
ROT_DIM = HEAD_DIM // 4
ROPE_THETA = 500000.0
MOBA_HEADS = 8
MOBA_BLOCK = 256
MOBA_TOPK = 3
DIFF_HEADS = 4
DIFF_DIM = 2 * HEAD_DIM
RNN_BLOCKS = 4
CONV_W = 4
LRU_C = 8.0
EPS = 1e-6
LANES = 128
NEG = -1e30
SCALE = HEAD_DIM ** -0.5
NT_DIMS = (((1,), (1,)), ((), ()))
VMEM_LIMIT = 52 * 1024 * 1024


def _cparams(sem):
    return pltpu.CompilerParams(dimension_semantics=sem, vmem_limit_bytes=VMEM_LIMIT)


def _rms(x, g):
    ms = jnp.mean(x * x, axis=-1, keepdims=True)
    return x * lax.rsqrt(ms + EPS) * g


def _rope_tables(pos):
    half = ROT_DIM // 2
    inv = jnp.exp(-math.log(ROPE_THETA) * jnp.arange(half, dtype=F32) * (2.0 / ROT_DIM))
    ang = pos.astype(F32)[:, None] * inv[None, :]
    cos, sin = jnp.cos(ang), jnp.sin(ang)
    n = pos.shape[0]
    pad = jnp.zeros((n, HEAD_DIM - ROT_DIM), F32)
    c64 = jnp.concatenate([cos, cos, pad + 1.0], axis=1)
    sa64 = jnp.concatenate([-sin, jnp.zeros((n, half), F32), pad], axis=1)
    sb64 = jnp.concatenate([jnp.zeros((n, half), F32), sin, pad], axis=1)
    rep = LANES // HEAD_DIM
    return tuple(jnp.tile(t, (1, rep)) for t in (c64, sa64, sb64))


def _in_proj_body(x_ref, g_ref, w_ref, c_ref, sa_ref, sb_ref, *rest, n_out, rope_flags, tn):
    outs = rest[:n_out]
    xn_sc = rest[n_out]
    j = pl.program_id(1)

    @pl.when(j == 0)
    def _():
        xn_sc[...] = _rms(x_ref[...], g_ref[...]).astype(BF16)

    u = jnp.dot(xn_sc[...], w_ref[...], preferred_element_type=F32)
    for jj in range(n_out):
        @pl.when(j == jj)
        def _(jj=jj):
            if rope_flags[jj]:
                c, sa, sb = c_ref[...], sa_ref[...], sb_ref[...]
                for k in range(tn // LANES):
                    uk = u[:, k * LANES:(k + 1) * LANES]
                    outs[jj][:, k * LANES:(k + 1) * LANES] = (
                        uk * c + pltpu.roll(uk, LANES - ROT_DIM // 2, 1) * sa
                        + pltpu.roll(uk, ROT_DIM // 2, 1) * sb)
            else:
                outs[jj][...] = u


def _in_proj(x, g, w_bf, tables, rope_flags, tm):
    m, d = x.shape
    n_out = len(rope_flags)
    tn = w_bf.shape[1] // n_out
    npos = tables[0].shape[0] // tm
    tab_spec = pl.BlockSpec((tm, LANES), lambda i, j: (i % npos, 0))
    return pl.pallas_call(
        functools.partial(_in_proj_body, n_out=n_out, rope_flags=rope_flags, tn=tn),
        grid=(m // tm, n_out),
        in_specs=[pl.BlockSpec((tm, d), lambda i, j: (i, 0)),
                  pl.BlockSpec((1, d), lambda i, j: (0, 0)),
                  pl.BlockSpec((d, tn), lambda i, j: (0, j)),
                  tab_spec, tab_spec, tab_spec],
        out_specs=[pl.BlockSpec((tm, tn), lambda i, j: (i, 0))] * n_out,
        out_shape=[jax.ShapeDtypeStruct((m, tn), F32)] * n_out,
        scratch_shapes=[pltpu.VMEM((tm, d), BF16)],
        compiler_params=_cparams(("arbitrary", "arbitrary")),
        name="in_proj",
    )(x, g.reshape(1, d), w_bf, *tables)


def _attn_out_body(om_ref, od_ref, w_ref, g_ref, x_ref, y_ref):
    km = om_ref.shape[1]
    o = jnp.dot(om_ref[...].astype(BF16), w_ref[:km, :], preferred_element_type=F32)
    o = o + jnp.dot(od_ref[...].astype(BF16), w_ref[km:, :], preferred_element_type=F32)
    y_ref[...] = x_ref[...] + _rms(o, g_ref[...])


def _attn_out(om, od, w_bf, g, x, tm):
    m, d = x.shape
    km, kd = om.shape[1], od.shape[1]
    return pl.pallas_call(
        _attn_out_body,
        grid=(m // tm,),
        in_specs=[pl.BlockSpec((tm, km), lambda i: (i, 0)),
                  pl.BlockSpec((tm, kd), lambda i: (i, 0)),
                  pl.BlockSpec((km + kd, d), lambda i: (0, 0)),
                  pl.BlockSpec((1, d), lambda i: (0, 0)),
                  pl.BlockSpec((tm, d), lambda i: (i, 0))],
        out_specs=pl.BlockSpec((tm, d), lambda i: (i, 0)),
        out_shape=jax.ShapeDtypeStruct((m, d), F32),
        compiler_params=_cparams(("arbitrary",)),
        name="attn_out",
    )(om, od, w_bf, g.reshape(1, d), x)


def _mlp_body(x_ref, g1_ref, wu_ref, wd_ref, g2_ref, y_ref, xn_sc, acc_sc):
    f = pl.program_id(1)

    @pl.when(f == 0)
    def _():
        xn_sc[...] = _rms(x_ref[...], g1_ref[...]).astype(BF16)
        acc_sc[...] = jnp.zeros_like(acc_sc)

    h = jnp.dot(xn_sc[...], wu_ref[...], preferred_element_type=F32)
    h = jnp.square(jnp.maximum(h, 0.0))
    acc_sc[...] += jnp.dot(h.astype(BF16), wd_ref[...], preferred_element_type=F32)

    @pl.when(f == pl.num_programs(1) - 1)
    def _():
        y_ref[...] = x_ref[...] + _rms(acc_sc[...], g2_ref[...])


def _mlp(x, g1, wu_bf, wd_bf, g2, tm, tf):
    m, d = x.shape
    ff = wu_bf.shape[1]
    return pl.pallas_call(
        _mlp_body,
        grid=(m // tm, ff // tf),
        in_specs=[pl.BlockSpec((tm, d), lambda i, f: (i, 0)),
                  pl.BlockSpec((1, d), lambda i, f: (0, 0)),
                  pl.BlockSpec((d, tf), lambda i, f: (0, f)),
                  pl.BlockSpec((tf, d), lambda i, f: (f, 0)),
                  pl.BlockSpec((1, d), lambda i, f: (0, 0))],
        out_specs=pl.BlockSpec((tm, d), lambda i, f: (i, 0)),
        out_shape=jax.ShapeDtypeStruct((m, d), F32),
        scratch_shapes=[pltpu.VMEM((tm, d), BF16), pltpu.VMEM((tm, d), F32)],
        compiler_params=_cparams(("arbitrary", "arbitrary")),
        name="mlp",
    )(x, g1.reshape(1, d), wu_bf, wd_bf, g2.reshape(1, d))


def _softmax_block(st, m, l, acc, vt):
    m_new = jnp.maximum(m, jnp.max(st, axis=0, keepdims=True))
    alpha = jnp.exp(m - m_new)
    p = jnp.exp(st - m_new)
    l = alpha * l + jnp.sum(p, axis=0, keepdims=True)
    acc = alpha * acc + jnp.dot(vt, p.astype(BF16), preferred_element_type=F32)
    return m_new, l, acc


def _first_block(st, vt):
    m = jnp.max(st, axis=0, keepdims=True)
    p = jnp.exp(st - m)
    l = jnp.sum(p, axis=0, keepdims=True)
    acc = jnp.dot(vt, p.astype(BF16), preferred_element_type=F32)
    return m, l, acc


def _causal_mask(st):
    kidx = lax.broadcasted_iota(jnp.int32, st.shape, 0)
    qidx = lax.broadcasted_iota(jnp.int32, st.shape, 1)
    return jnp.where(kidx <= qidx, st, NEG)


def _moba_body(q_ref, k_ref, v_ref, o_ref, kmean_sc, kh_sc, vt_sc, sb_sc, qs_sc, m_sc, l_sc, acc_sc,
               ot_sc, *, nb):
    qi = pl.program_id(1)
    blk = MOBA_BLOCK
    hd = HEAD_DIM

    @pl.when(qi == 0)
    def _prep():
        for n in range(nb):
            kblk = k_ref[n * blk:(n + 1) * blk, :]
            kmean_sc[n:n + 1, :] = jnp.sum(kblk, axis=0, keepdims=True) * (1.0 / blk)
            vtb = v_ref[n * blk:(n + 1) * blk, :].T
            for h in range(MOBA_HEADS):
                kh_sc[h, n] = kblk[:, h * hd:(h + 1) * hd].astype(BF16)
                vt_sc[h, n] = vtb[h * hd:(h + 1) * hd, :].astype(BF16)

    k_eff = min(MOBA_TOPK, nb)
    for h in range(MOBA_HEADS):
        hs = slice(h * hd, (h + 1) * hd)
        qh = q_ref[:, hs]
        gt = lax.dot_general(kmean_sc[:, hs], qh, NT_DIMS, precision=HIGHEST,
                             preferred_element_type=F32)
        rid = lax.broadcasted_iota(jnp.int32, gt.shape, 0)
        valid = rid < qi
        g = jnp.where(valid, gt, -jnp.inf)
        rank = jnp.zeros(gt.shape, jnp.int32)
        for mm in range(nb):
            gm = g[mm:mm + 1, :]
            beats = (gm > g) | ((gm == g) & (rid > mm))
            rank = rank + beats.astype(jnp.int32)
        sel = valid & (rank < k_eff)
        sb_sc[h] = jnp.where(sel, 0.0, NEG)

        qs = (qh * SCALE).astype(BF16)
        qs_sc[h] = qs
        st = lax.dot_general(kh_sc[h, qi], qs, NT_DIMS, preferred_element_type=F32)
        m_sc[h], l_sc[h], acc_sc[h] = _first_block(_causal_mask(st), vt_sc[h, qi])

    def body(n, carry):
        scores = lambda h: lax.dot_general(kh_sc[h, n], qs_sc[h], NT_DIMS, preferred_element_type=F32)
        st_next = scores(0)
        for h in range(MOBA_HEADS):
            st = st_next + sb_sc[h, pl.ds(n, 1), :]
            if h + 1 < MOBA_HEADS:
                st_next = scores(h + 1)
            m_sc[h], l_sc[h], acc_sc[h] = _softmax_block(st, m_sc[h], l_sc[h], acc_sc[h], vt_sc[h, n])
        return carry

    lax.fori_loop(0, qi, body, 0)
    for h in range(MOBA_HEADS):
        ot_sc[h * hd:(h + 1) * hd, :] = acc_sc[h] / l_sc[h]
    o_ref[...] = ot_sc[...].T


def _moba_prompt(q, k, v, batch, seq):
    w = q.shape[1]
    blk = MOBA_BLOCK
    nb = seq // blk
    return pl.pallas_call(
        functools.partial(_moba_body, nb=nb),
        grid=(batch, nb),
        in_specs=[pl.BlockSpec((blk, w), lambda b, i: (b * nb + i, 0)),
                  pl.BlockSpec((seq, w), lambda b, i: (b, 0)),
                  pl.BlockSpec((seq, w), lambda b, i: (b, 0))],
        out_specs=pl.BlockSpec((blk, w), lambda b, i: (b * nb + i, 0)),
        out_shape=jax.ShapeDtypeStruct(q.shape, F32),
        scratch_shapes=[pltpu.VMEM((nb, w), F32),
                        pltpu.VMEM((MOBA_HEADS, nb, blk, HEAD_DIM), BF16),
                        pltpu.VMEM((MOBA_HEADS, nb, HEAD_DIM, blk), BF16),
                        pltpu.VMEM((MOBA_HEADS, nb, blk), F32),
                        pltpu.VMEM((MOBA_HEADS, blk, HEAD_DIM), BF16),
                        pltpu.VMEM((MOBA_HEADS, 1, blk), F32),
                        pltpu.VMEM((MOBA_HEADS, 1, blk), F32),
                        pltpu.VMEM((MOBA_HEADS, HEAD_DIM, blk), F32),
                        pltpu.VMEM((w, blk), F32)],
        compiler_params=_cparams(("arbitrary", "arbitrary")),
        name="moba_prompt",
    )(q, k, v)


def _diff_lambda(lam_ref, lam_init):
    lv = lam_ref[...]
    s1 = jnp.sum(lv[0:1, :] * lv[1:2, :], axis=-1, keepdims=True)
    s2 = jnp.sum(lv[2:3, :] * lv[3:4, :], axis=-1, keepdims=True)
    return jnp.exp(s1) - jnp.exp(s2) + lam_init


def _diff_body(q_ref, k_ref, v_ref, lam_ref, gsub_ref, o_ref, kh_sc, vt_sc, qs_sc, m_sc, l_sc, acc_sc,
               ot_sc, *, nb, tq, lam_init):
    qi = pl.program_id(1)
    hd = HEAD_DIM

    @pl.when(qi == 0)
    def _prep():
        for n in range(nb):
            kblk = k_ref[n * tq:(n + 1) * tq, :]
            vtb = v_ref[n * tq:(n + 1) * tq, :].T
            for r in range(2 * DIFF_HEADS):
                kh_sc[r, n] = kblk[:, r * hd:(r + 1) * hd].astype(BF16)
            for h in range(DIFF_HEADS):
                vt_sc[h, n] = vtb[h * DIFF_DIM:(h + 1) * DIFF_DIM, :].astype(BF16)

    nrow = 2 * DIFF_HEADS
    for r in range(nrow):
        qs = (q_ref[:, r * hd:(r + 1) * hd] * SCALE).astype(BF16)
        qs_sc[r] = qs
        st = lax.dot_general(kh_sc[r, qi], qs, NT_DIMS, preferred_element_type=F32)
        m_sc[r], l_sc[r], acc_sc[r] = _first_block(_causal_mask(st), vt_sc[r // 2, qi])

    def body(n, carry):
        for r in range(nrow):
            st = lax.dot_general(kh_sc[r, n], qs_sc[r], NT_DIMS, preferred_element_type=F32)
            m_sc[r], l_sc[r], acc_sc[r] = _softmax_block(st, m_sc[r], l_sc[r], acc_sc[r], vt_sc[r // 2, n])
        return carry

    lax.fori_loop(0, qi, body, 0)
    lam = _diff_lambda(lam_ref, lam_init)
    for h in range(DIFF_HEADS):
        od = acc_sc[2 * h] / l_sc[2 * h] - lam * (acc_sc[2 * h + 1] / l_sc[2 * h + 1])
        ms = jnp.mean(od * od, axis=0, keepdims=True)
        od = od * lax.rsqrt(ms + EPS) * gsub_ref[...]
        ot_sc[h * DIFF_DIM:(h + 1) * DIFF_DIM, :] = od * (1.0 - lam_init)
    o_ref[...] = ot_sc[...].T


def _diff_prompt(q, k, v, lam4, gsub, batch, seq, lam_init, tq=256):
    w = q.shape[1]
    nb = seq // tq
    return pl.pallas_call(
        functools.partial(_diff_body, nb=nb, tq=tq, lam_init=lam_init),
        grid=(batch, nb),
        in_specs=[pl.BlockSpec((tq, w), lambda b, i: (b * nb + i, 0)),
                  pl.BlockSpec((seq, w), lambda b, i: (b, 0)),
                  pl.BlockSpec((seq, w), lambda b, i: (b, 0)),
                  pl.BlockSpec((4, HEAD_DIM), lambda b, i: (0, 0)),
                  pl.BlockSpec((DIFF_DIM, 1), lambda b, i: (0, 0))],
        out_specs=pl.BlockSpec((tq, w), lambda b, i: (b * nb + i, 0)),
        out_shape=jax.ShapeDtypeStruct(q.shape, F32),
        scratch_shapes=[pltpu.VMEM((2 * DIFF_HEADS, nb, tq, HEAD_DIM), BF16),
                        pltpu.VMEM((DIFF_HEADS, nb, DIFF_DIM, tq), BF16),
                        pltpu.VMEM((2 * DIFF_HEADS, tq, HEAD_DIM), BF16),
                        pltpu.VMEM((2 * DIFF_HEADS, 1, tq), F32),
                        pltpu.VMEM((2 * DIFF_HEADS, 1, tq), F32),
                        pltpu.VMEM((2 * DIFF_HEADS, DIFF_DIM, tq), F32),
                        pltpu.VMEM((w, tq), F32)],
        compiler_params=_cparams(("arbitrary", "arbitrary")),
        name="diff_prompt",
    )(q, k, v, lam4, gsub.reshape(DIFF_DIM, 1))


def _gelu_tanh(x):
    return 0.5 * x * (1.0 + jnp.tanh(math.sqrt(2.0 / math.pi) * (x + 0.044715 * (x * x * x))))


def _log_sigmoid(x):
    return jnp.minimum(x, 0.0) - jnp.log1p(jnp.exp(-jnp.abs(x)))


def _lru_gates(xc, wa_ref, ba_ref, wx_ref, bx_ref, lam_ref):
    bw = wa_ref.shape[1]
    xcb = xc.astype(BF16)
    ra, rx = [], []
    for n in range(RNN_BLOCKS):
        xn = xcb[:, n * bw:(n + 1) * bw]
        ra.append(jnp.dot(xn, wa_ref[n], preferred_element_type=F32))
        rx.append(jnp.dot(xn, wx_ref[n], preferred_element_type=F32))
    r = jax.nn.sigmoid(jnp.concatenate(ra, axis=1) + ba_ref[...])
    ig = jax.nn.sigmoid(jnp.concatenate(rx, axis=1) + bx_ref[...])
    log_a = LRU_C * r * _log_sigmoid(lam_ref[...])
    a = jnp.exp(log_a)
    b = jnp.sqrt(-jnp.tanh(log_a) * (1.0 + a * a)) * (ig * xc)
    return a, b


def _rglru_body(gate_ref, xb_ref, x_ref, cw_ref, cb_ref, wa_ref, ba_ref, wx_ref, bx_ref, lam_ref,
                wo_ref, g_ref, y_ref, buf_ref, hl_ref, xpad_sc, a_sc, h_sc, hc_sc, *, tc):
    c = pl.program_id(1)
    hist = CONV_W - 1

    @pl.when(c == 0)
    def _():
        xpad_sc[0:8, :] = jnp.zeros((8, xpad_sc.shape[1]), F32)
        hc_sc[...] = jnp.zeros_like(hc_sc)

    @pl.when(c > 0)
    def _():
        xpad_sc[0:8, :] = xpad_sc[tc:tc + 8, :]

    xpad_sc[8:8 + tc, :] = xb_ref[...]
    xc = cb_ref[...] + xpad_sc[8 - hist:8 - hist + tc, :] * cw_ref[0:1, :]
    for i in range(1, CONV_W):
        xc = xc + xpad_sc[8 - hist + i:8 - hist + i + tc, :] * cw_ref[i:i + 1, :]
    a, b = _lru_gates(xc, wa_ref, ba_ref, wx_ref, bx_ref, lam_ref)
    a_sc[...] = a
    h_sc[...] = b

    def step(t, h):
        h = a_sc[pl.ds(t, 1), :] * h + h_sc[pl.ds(t, 1), :]
        h_sc[pl.ds(t, 1), :] = h
        return h

    h = lax.fori_loop(0, tc, step, hc_sc[...], unroll=8)
    hc_sc[...] = h
    y = (h_sc[...] * _gelu_tanh(gate_ref[...])).astype(BF16)
    y = jnp.dot(y, wo_ref[...], preferred_element_type=F32)
    y_ref[...] = x_ref[...] + _rms(y, g_ref[...])

    @pl.when(c == pl.num_programs(1) - 1)
    def _():
        buf_ref[...] = xpad_sc[8 + tc - hist:8 + tc, :]
        hl_ref[...] = h


def _rglru_prompt(gate, xb, x, cw, cb, wa_bf, ba, wx_bf, bx, lam, wo_bf, g, batch, seq, tc=256):
    m, d = x.shape
    nc = seq // tc
    bw = d // RNN_BLOCKS
    row = lambda b, c: (b * nc + c, 0)
    const2 = lambda b, c: (0, 0)
    const3 = lambda b, c: (0, 0, 0)
    vec = pl.BlockSpec((1, d), const2)
    return pl.pallas_call(
        functools.partial(_rglru_body, tc=tc),
        grid=(batch, nc),
        in_specs=[pl.BlockSpec((tc, d), row), pl.BlockSpec((tc, d), row), pl.BlockSpec((tc, d), row),
                  pl.BlockSpec((CONV_W, d), const2), vec,
                  pl.BlockSpec((RNN_BLOCKS, bw, bw), const3), vec,
                  pl.BlockSpec((RNN_BLOCKS, bw, bw), const3), vec, vec,
                  pl.BlockSpec((d, d), const2), vec],
        out_specs=[pl.BlockSpec((tc, d), row),
                   pl.BlockSpec((None, CONV_W - 1, d), lambda b, c: (b, 0, 0)),
                   pl.BlockSpec((None, 1, d), lambda b, c: (b, 0, 0))],
        out_shape=[jax.ShapeDtypeStruct((m, d), F32),
                   jax.ShapeDtypeStruct((batch, CONV_W - 1, d), F32),
                   jax.ShapeDtypeStruct((batch, 1, d), F32)],
        scratch_shapes=[pltpu.VMEM((tc + 8, d), F32), pltpu.VMEM((tc, d), F32),
                        pltpu.VMEM((tc, d), F32), pltpu.VMEM((1, d), F32)],
        compiler_params=_cparams(("arbitrary", "arbitrary")),
        name="rglru_prompt",
    )(gate, xb, x, cw, cb.reshape(1, d), wa_bf, ba.reshape(1, d), wx_bf, bx.reshape(1, d),
      lam.reshape(1, d), wo_bf, g.reshape(1, d))


def _rglru_step_body(gate_ref, xb_ref, x_ref, conv_ref, h0_ref, cw_ref, cb_ref, wa_ref, ba_ref,
                     wx_ref, bx_ref, lam_ref, wo_ref, g_ref, y_ref, h_ref):
    xc = cb_ref[...] + xb_ref[...] * cw_ref[CONV_W - 1:CONV_W, :]
    for i in range(CONV_W - 1):
        xc = xc + conv_ref[i] * cw_ref[i:i + 1, :]
    a, b = _lru_gates(xc, wa_ref, ba_ref, wx_ref, bx_ref, lam_ref)
    h = a * h0_ref[...] + b
    h_ref[...] = h
    y = (h * _gelu_tanh(gate_ref[...])).astype(BF16)
    y = jnp.dot(y, wo_ref[...], preferred_element_type=F32)
    y_ref[...] = x_ref[...] + _rms(y, g_ref[...])


def _rglru_step(gate, xb, x, conv_t, h0, cw, cb, wa_bf, ba, wx_bf, bx, lam, wo_bf, g):
    m, d = x.shape
    return pl.pallas_call(
        _rglru_step_body,
        out_shape=[jax.ShapeDtypeStruct((m, d), F32), jax.ShapeDtypeStruct((m, d), F32)],
        compiler_params=pltpu.CompilerParams(vmem_limit_bytes=VMEM_LIMIT),
        name="rglru_step",
    )(gate, xb, x, conv_t, h0, cw, cb.reshape(1, d), wa_bf, ba.reshape(1, d), wx_bf, bx.reshape(1, d),
      lam.reshape(1, d), wo_bf, g.reshape(1, d))


def _key_minor(cache):
    nd = cache.ndim
    t = jnp.transpose(cache, (0, 1) + tuple(range(3, nd)) + (2,))
    return t.reshape(t.shape[:2] + (-1,) + t.shape[-2:])


def _page_scores(kt_ref, qb):
    return jnp.sum(kt_ref[...] * qb, axis=1)


def _moba_gate_body(pt_ref, q_ref, *refs, pp, ppb):
    del pt_ref
    k_refs = refs[:pp]
    sel_ref, qb_sc, g_sc = refs[pp:pp + 3]
    s_idx = pl.program_id(1)
    page = k_refs[0].shape[-1]
    nblk = g_sc.shape[0]

    @pl.when(s_idx == 0)
    def _():
        qb_sc[...] = jnp.broadcast_to(q_ref[...], qb_sc.shape)

    qb = qb_sc[...]
    for j in range(pp // ppb):
        tot = _page_scores(k_refs[ppb * j], qb)
        for t in range(1, ppb):
            tot = tot + _page_scores(k_refs[ppb * j + t], qb)
        g = jnp.sum(tot, axis=-1, keepdims=True) * (1.0 / (ppb * page))
        g_sc[s_idx * (pp // ppb) + j] = jnp.broadcast_to(g, g_sc.shape[1:])

    @pl.when(s_idx == pl.num_programs(1) - 1)
    def _():
        g = g_sc[...]
        bid = lax.broadcasted_iota(jnp.int32, g.shape, 0)
        lid = lax.broadcasted_iota(jnp.int32, sel_ref.shape, 1)
        out = jnp.zeros(sel_ref.shape, jnp.int32)
        for t in range(MOBA_TOPK):
            mx = jnp.max(g, axis=0, keepdims=True)
            idx = jnp.min(jnp.where(g == mx, bid, nblk), axis=0, keepdims=True)
            out = jnp.where(lid == t, idx[0], out)
            g = jnp.where(bid == idx, -jnp.inf, g)
        sel_ref[...] = out


def _moba_gate(q_col, kt, pt_flat, li, batch, n_pages, pp=16):
    _, _, nh, hd, page = kt.shape
    ppb = MOBA_BLOCK // page
    nblk = n_pages // ppb

    def page_spec(j):
        return pl.BlockSpec((None, None, nh, hd, page),
                            lambda b, s, pt: (li, pt[b * n_pages + s * pp + j], 0, 0, 0))

    return pl.pallas_call(
        functools.partial(_moba_gate_body, pp=pp, ppb=ppb),
        grid_spec=pltpu.PrefetchScalarGridSpec(
            num_scalar_prefetch=1, grid=(batch, n_pages // pp),
            in_specs=[pl.BlockSpec((None, nh, hd, 1), lambda b, s, pt: (b, 0, 0, 0))]
            + [page_spec(j) for j in range(pp)],
            out_specs=pl.BlockSpec((None, nh, LANES), lambda b, s, pt: (b, 0, 0)),
            scratch_shapes=[pltpu.VMEM((nh, hd, page), F32), pltpu.VMEM((nblk, nh, LANES), F32)]),
        out_shape=jax.ShapeDtypeStruct((batch, nh, LANES), jnp.int32),
        compiler_params=_cparams(("arbitrary", "arbitrary")),
        name="moba_gate",
    )(pt_flat, q_col, *([kt] * pp))


def _moba_step_body(sel_ref, pt_ref, q_ref, kn_ref, vn_ref, *refs, nsl):
    del sel_ref, pt_ref
    k_refs, v_refs, o_ref = refs[:nsl], refs[nsl:2 * nsl], refs[2 * nsl]
    q = q_ref[...] * SCALE
    q8 = jnp.broadcast_to(q, (8, q.shape[1]))
    kt = jnp.concatenate([r[...] for r in k_refs], axis=1).astype(BF16)
    vt = jnp.concatenate([r[...] for r in v_refs], axis=1).astype(BF16)
    s = jnp.dot(q8.astype(BF16), kt, preferred_element_type=F32)
    s_self = jnp.sum(q8 * kn_ref[...], axis=-1, keepdims=True)
    m = jnp.maximum(jnp.max(s, axis=-1, keepdims=True), s_self)
    p = jnp.exp(s - m)
    p_self = jnp.exp(s_self - m)
    l = jnp.sum(p, axis=-1, keepdims=True) + p_self
    pv = lax.dot_general(p.astype(BF16), vt, NT_DIMS, preferred_element_type=F32)
    o_ref[...] = (pv + p_self * vn_ref[...]) / l


def _moba_step(q4, kn4, vn4, kt, vt, sel_flat, pt_flat, li, batch, n_pages):
    _, _, nh, hd, page = kt.shape
    ppb = MOBA_BLOCK // page
    nsl = MOBA_TOPK * ppb

    def slab_spec(t, j):
        def imap(b, h, sel, pt):
            blk = sel[(b * nh + h) * MOBA_TOPK + t]
            return (li, pt[b * n_pages + blk * ppb + j], h, 0, 0)
        return pl.BlockSpec((None, None, None, hd, page), imap)

    slabs = [slab_spec(t, j) for t in range(MOBA_TOPK) for j in range(ppb)]
    row = pl.BlockSpec((None, None, 1, hd), lambda b, h, sel, pt: (b, h, 0, 0))
    return pl.pallas_call(
        functools.partial(_moba_step_body, nsl=nsl),
        grid_spec=pltpu.PrefetchScalarGridSpec(
            num_scalar_prefetch=2, grid=(batch, nh),
            in_specs=[row, row, row] + slabs + slabs,
            out_specs=pl.BlockSpec((None, None, 8, hd), lambda b, h, sel, pt: (b, h, 0, 0))),
        out_shape=jax.ShapeDtypeStruct((batch, nh, 8, hd), F32),
        compiler_params=_cparams(("arbitrary", "arbitrary")),
        name="moba_step",
    )(sel_flat, pt_flat, q4, kn4, vn4, *([kt] * nsl), *([vt] * nsl))


def _diff_step_body(pt_ref, q_ref, kn_ref, vn_ref, lam_ref, gsub_ref, *refs, pp, lam_init):
    del pt_ref
    k_refs, v_refs = refs[:pp], refs[pp:2 * pp]
    o_ref, qb_sc, m_sc, l_sc, acc_sc = refs[2 * pp:2 * pp + 5]
    s_idx = pl.program_id(1)
    page = k_refs[0].shape[-1]

    @pl.when(s_idx == 0)
    def _():
        qb_sc[...] = jnp.broadcast_to(q_ref[...] * SCALE, qb_sc.shape)
        m_sc[...] = jnp.full(m_sc.shape, NEG, F32)
        l_sc[...] = jnp.zeros_like(l_sc)
        acc_sc[...] = jnp.zeros_like(acc_sc)

    qb = qb_sc[...]
    s = jnp.concatenate([_page_scores(r, qb) for r in k_refs], axis=1)
    m_old = m_sc[...]
    m_new = jnp.maximum(m_old, jnp.max(s, axis=-1, keepdims=True))
    alpha = jnp.exp(m_old - m_new)
    p = jnp.exp(s - m_new)
    l_sc[...] = alpha * l_sc[...] + jnp.sum(p, axis=-1, keepdims=True)
    m_sc[...] = m_new
    pb = p.astype(BF16)
    for h in range(DIFF_HEADS):
        vh = jnp.concatenate([r[pl.ds(h, page, stride=DIFF_HEADS), :] for r in v_refs], axis=0)
        acc_sc[h] = alpha * acc_sc[h] + jnp.dot(pb, vh.astype(BF16), preferred_element_type=F32)

    @pl.when(s_idx == pl.num_programs(1) - 1)
    def _():
        s_self = jnp.sum(q_ref[...] * SCALE * kn_ref[...], axis=1)
        m_old = m_sc[...]
        m_new = jnp.maximum(m_old, s_self)
        alpha = jnp.exp(m_old - m_new)
        p_self = jnp.exp(s_self - m_new)
        l = alpha * l_sc[...] + p_self
        lam = _diff_lambda(lam_ref, lam_init)
        for h in range(DIFF_HEADS):
            vn = vn_ref[:, h * DIFF_DIM:(h + 1) * DIFF_DIM]
            o = (alpha * acc_sc[h] + p_self * vn) / l
            od = o[2 * h:2 * h + 1, :] - lam * o[2 * h + 1:2 * h + 2, :]
            o_ref[:, h * DIFF_DIM:(h + 1) * DIFF_DIM] = _rms(od, gsub_ref[...]) * (1.0 - lam_init)


def _diff_step(q_col, kn_col, vn3, lam4, gsub, kt, v2, pt_flat, li, batch, n_pages, lam_init, pp=8):
    _, _, nrow, hd, page = kt.shape
    vrows, dv = v2.shape[2:]
    w = vn3.shape[-1]

    def k_spec(j):
        return pl.BlockSpec((None, None, nrow, hd, page),
                            lambda b, s, pt: (li, pt[b * n_pages + s * pp + j], 0, 0, 0))

    def v_spec(j):
        return pl.BlockSpec((None, None, vrows, dv),
                            lambda b, s, pt: (li, pt[b * n_pages + s * pp + j], 0, 0))

    col = pl.BlockSpec((None, nrow, hd, 1), lambda b, s, pt: (b, 0, 0, 0))
    row = pl.BlockSpec((None, 1, w), lambda b, s, pt: (b, 0, 0))
    return pl.pallas_call(
        functools.partial(_diff_step_body, pp=pp, lam_init=lam_init),
        grid_spec=pltpu.PrefetchScalarGridSpec(
            num_scalar_prefetch=1, grid=(batch, n_pages // pp),
            in_specs=[col, col, row,
                      pl.BlockSpec((4, HEAD_DIM), lambda b, s, pt: (0, 0)),
                      pl.BlockSpec((1, DIFF_DIM), lambda b, s, pt: (0, 0))]
            + [k_spec(j) for j in range(pp)] + [v_spec(j) for j in range(pp)],
            out_specs=row,
            scratch_shapes=[pltpu.VMEM((nrow, hd, page), F32), pltpu.VMEM((nrow, 1), F32),
                            pltpu.VMEM((nrow, 1), F32), pltpu.VMEM((DIFF_HEADS, nrow, dv), F32)]),
        out_shape=jax.ShapeDtypeStruct((batch, 1, w), F32),
        compiler_params=_cparams(("arbitrary", "arbitrary")),
        name="diff_step",
    )(pt_flat, q_col, kn_col, vn3, lam4, gsub.reshape(1, DIFF_DIM), *([kt] * pp), *([v2] * pp))


def kernel(x_prompt, x_sample, cache_moba_k, cache_moba_v, cache_diff_k, cache_diff_v, state_conv, state_rnn, page_table, attn_g_pre, attn_w_in, diff_lambda_q1, diff_lambda_k1, diff_lambda_q2, diff_lambda_k2, diff_g_sub, attn_w_out, attn_g_post, rnn_g_pre, rnn_w_in, rnn_conv_w, rnn_conv_b, rnn_w_a, rnn_b_a, rnn_w_x, rnn_b_x, rnn_lambda, rnn_w_out, rnn_g_post, mlp_g_pre, mlp_w_up, mlp_w_down, mlp_g_post):
    batch, seq, d = x_prompt.shape
    dec_batch, dec_seq, _ = x_sample.shape
    assert dec_seq == 1
    depth = mlp_w_up.shape[0]
    na, n_pool, page = cache_moba_k.shape[:3]
    n_pages = page_table.shape[1]
    past_len = n_pages * page
    mw = MOBA_HEADS * HEAD_DIM
    dw = DIFF_HEADS * DIFF_DIM

    xp = x_prompt.reshape(batch * seq, d)
    xs = x_sample.reshape(dec_batch, d)
    tab_p = _rope_tables(jnp.arange(seq, dtype=jnp.int32))
    tab_s = _rope_tables(jnp.full((dec_batch,), past_len, jnp.int32))
    pt_flat = page_table.reshape(-1)
    cmk_t = _key_minor(cache_moba_k)
    cmv_t = _key_minor(cache_moba_v)
    cdk_t = _key_minor(cache_diff_k)
    cdv2 = cache_diff_v.reshape(na, n_pool, page * DIFF_HEADS, DIFF_DIM)

    tm_p = 512
    attn_rope = (True, True, False, True, True, False)
    outs = {k: [] for k in ("pmk", "pmv", "pdk", "pdv", "pconv", "prnn",
                            "smk", "smv", "sdk", "sdv", "sconv", "srnn")}
    for layer in range(depth):
        li = layer // 2
        if layer % 2 == 0:
            lam_init = 0.8 - 0.6 * math.exp(-0.3 * layer)
            w_in = attn_w_in[li].astype(BF16)
            w_out = attn_w_out[li].astype(BF16)
            lam4 = jnp.stack([diff_lambda_q1[li], diff_lambda_k1[li], diff_lambda_q2[li], diff_lambda_k2[li]])
            mq, mk, mv, dq, dk, dv = _in_proj(xp, attn_g_pre[li], w_in, tab_p, attn_rope, tm_p)
            o_m = _moba_prompt(mq, mk, mv, batch, seq)
            o_d = _diff_prompt(dq, dk, dv, lam4, diff_g_sub[li], batch, seq, lam_init)
            xp = _attn_out(o_m, o_d, w_out, attn_g_post[li], xp, tm_p)
            outs["pmk"].append(mk.reshape(batch, seq, MOBA_HEADS, HEAD_DIM))
            outs["pmv"].append(mv.reshape(batch, seq, MOBA_HEADS, HEAD_DIM))
            outs["pdk"].append(dk.reshape(batch, seq, DIFF_HEADS, 2, HEAD_DIM))
            outs["pdv"].append(dv.reshape(batch, seq, DIFF_HEADS, DIFF_DIM))
            mq, mk, mv, dq, dk, dv = _in_proj(xs, attn_g_pre[li], w_in, tab_s, attn_rope, dec_batch)
            col = lambda t: t.reshape(dec_batch, -1, HEAD_DIM, 1)
            hrow = lambda t: t.reshape(dec_batch, MOBA_HEADS, 1, HEAD_DIM)
            sel = _moba_gate(col(mq), cmk_t, pt_flat, li, dec_batch, n_pages)
            sel = sel[:, :, :MOBA_TOPK].reshape(-1)
            o_m = _moba_step(hrow(mq), hrow(mk), hrow(mv), cmk_t, cmv_t, sel, pt_flat, li,
                             dec_batch, n_pages)[:, :, 0, :]
            o_d = _diff_step(col(dq), col(dk), dv.reshape(dec_batch, 1, dw), lam4, diff_g_sub[li],
                             cdk_t, cdv2, pt_flat, li, dec_batch, n_pages, lam_init)
            xs = _attn_out(o_m.reshape(dec_batch, mw), o_d.reshape(dec_batch, dw), w_out,
                           attn_g_post[li], xs, dec_batch)
            outs["smk"].append(mk.reshape(dec_batch, 1, MOBA_HEADS, HEAD_DIM))
            outs["smv"].append(mv.reshape(dec_batch, 1, MOBA_HEADS, HEAD_DIM))
            outs["sdk"].append(dk.reshape(dec_batch, 1, DIFF_HEADS, 2, HEAD_DIM))
            outs["sdv"].append(dv.reshape(dec_batch, 1, DIFF_HEADS, DIFF_DIM))
        else:
            w_in = rnn_w_in[li].astype(BF16)
            wts = (rnn_conv_w[li], rnn_conv_b[li], rnn_w_a[li].astype(BF16), rnn_b_a[li].reshape(-1),
                   rnn_w_x[li].astype(BF16), rnn_b_x[li].reshape(-1), rnn_lambda[li],
                   rnn_w_out[li].astype(BF16), rnn_g_post[li])
            gate, xb = _in_proj(xp, rnn_g_pre[li], w_in, tab_p, (False, False), tm_p)
            xp, cbuf, hlast = _rglru_prompt(gate, xb, xp, *wts, batch, seq)
            outs["pconv"].append(cbuf)
            outs["prnn"].append(hlast.reshape(batch, d))
            gate, xb = _in_proj(xs, rnn_g_pre[li], w_in, tab_s, (False, False), dec_batch)
            conv_t = jnp.swapaxes(state_conv[li], 0, 1)
            xs, hnew = _rglru_step(gate, xb, xs, conv_t, state_rnn[li], *wts)
            outs["sconv"].append(jnp.concatenate([state_conv[li][:, 1:], xb[:, None, :]], axis=1))
            outs["srnn"].append(hnew)
        wu = mlp_w_up[layer].astype(BF16)
        wd = mlp_w_down[layer].astype(BF16)
        xp = _mlp(xp, mlp_g_pre[layer], wu, wd, mlp_g_post[layer], 1024, 512)
        xs = _mlp(xs, mlp_g_pre[layer], wu, wd, mlp_g_post[layer], dec_batch, 512)

    st = lambda k: jnp.stack(outs[k])
    return (xp.reshape(batch, seq, d), xs.reshape(dec_batch, 1, d),
            st("pmk"), st("pmv"), st("pdk"), st("pdv"), st("pconv"), st("prnn"),
            st("smk"), st("smv"), st("sdk"), st("sdv"), st("sconv"), st("srnn"))
```

```python
import functools
import math

import jax
import jax.numpy as jnp
from jax import lax
from jax.experimental import pallas as pl
from jax.experimental.pallas import tpu as pltpu

F32 = jnp.float32
BF16 = jnp.bfloat16
HIGHEST = lax.Precision.HIGHEST

HEAD_DIM = 64
ROT_DIM = HEAD_DIM // 4
ROPE_THETA = 500000.0
MOBA_HEADS = 8
MOBA_BLOCK = 256
MOBA_TOPK = 3
DIFF_HEADS = 4
DIFF_DIM = 2 * HEAD_DIM
RNN_BLOCKS = 4
CONV_W = 4
LRU_C = 8.0
EPS = 1e-6
LANES = 128
NEG = -1e30
SCALE = HEAD_DIM ** -0.5
SCALE_LOG2E = SCALE * math.log2(math.e)
NT_DIMS = (((1,), (1,)), ((), ()))
ATTN_CHAINS = 4
VMEM_LIMIT = 52 * 1024 * 1024


def _cparams(sem):
    return pltpu.CompilerParams(dimension_semantics=sem, vmem_limit_bytes=VMEM_LIMIT)


def _rms(x, g):
    ms = jnp.mean(x * x, axis=-1, keepdims=True)
    return x * lax.rsqrt(ms + EPS) * g


def _rope_tables(pos):
    half = ROT_DIM // 2
    inv = jnp.exp(-math.log(ROPE_THETA) * jnp.arange(half, dtype=F32) * (2.0 / ROT_DIM))
    ang = pos.astype(F32)[:, None] * inv[None, :]
    cos, sin = jnp.cos(ang), jnp.sin(ang)
    n = pos.shape[0]
    pad = jnp.zeros((n, HEAD_DIM - ROT_DIM), F32)
    c64 = jnp.concatenate([cos, cos, pad + 1.0], axis=1)
    sa64 = jnp.concatenate([-sin, jnp.zeros((n, half), F32), pad], axis=1)
    sb64 = jnp.concatenate([jnp.zeros((n, half), F32), sin, pad], axis=1)
    rep = LANES // HEAD_DIM
    return tuple(jnp.tile(t, (1, rep)) for t in (c64, sa64, sb64))


def _in_proj_body(x_ref, g_ref, w_ref, c_ref, sa_ref, sb_ref, *outs, rope_flags, tn):
    xn = _rms(x_ref[...], g_ref[...]).astype(BF16)
    for j, out_ref in enumerate(outs):
        u = jnp.dot(xn, w_ref[:, j * tn:(j + 1) * tn], preferred_element_type=F32)
        if rope_flags[j]:
            c, sa, sb = c_ref[...], sa_ref[...], sb_ref[...]
            for k in range(tn // LANES):
                uk = u[:, k * LANES:(k + 1) * LANES]
                out_ref[:, k * LANES:(k + 1) * LANES] = (
                    uk * c + pltpu.roll(uk, LANES - ROT_DIM // 2, 1) * sa
                    + pltpu.roll(uk, ROT_DIM // 2, 1) * sb)
        else:
            out_ref[...] = u


def _in_proj(x, g, w_bf, tables, rope_flags, tm):
    m, d = x.shape
    n_out = len(rope_flags)
    n = w_bf.shape[1]
    tn = n // n_out
    npos = tables[0].shape[0] // tm
    tab_spec = pl.BlockSpec((tm, LANES), lambda i: (i % npos, 0))
    return pl.pallas_call(
        functools.partial(_in_proj_body, rope_flags=rope_flags, tn=tn),
        grid=(m // tm,),
        in_specs=[pl.BlockSpec((tm, d), lambda i: (i, 0)),
                  pl.BlockSpec((1, d), lambda i: (0, 0)),
                  pl.BlockSpec((d, n), lambda i: (0, 0)),
                  tab_spec, tab_spec, tab_spec],
        out_specs=[pl.BlockSpec((tm, tn), lambda i: (i, 0))] * n_out,
        out_shape=[jax.ShapeDtypeStruct((m, tn), F32)] * n_out,
        compiler_params=_cparams(("arbitrary",)),
        name="in_proj",
    )(x, g.reshape(1, d), w_bf, *tables)


def _attn_out_body(om_ref, od_ref, w_ref, g_ref, x_ref, y_ref):
    km = om_ref.shape[1]
    o = jnp.dot(om_ref[...].astype(BF16), w_ref[:km, :], preferred_element_type=F32)
    o = o + jnp.dot(od_ref[...].astype(BF16), w_ref[km:, :], preferred_element_type=F32)
    y_ref[...] = x_ref[...] + _rms(o, g_ref[...])


def _attn_out(om, od, w_bf, g, x, tm):
    m, d = x.shape
    km, kd = om.shape[1], od.shape[1]
    return pl.pallas_call(
        _attn_out_body,
        grid=(m // tm,),
        in_specs=[pl.BlockSpec((tm, km), lambda i: (i, 0)),
                  pl.BlockSpec((tm, kd), lambda i: (i, 0)),
                  pl.BlockSpec((km + kd, d), lambda i: (0, 0)),
                  pl.BlockSpec((1, d), lambda i: (0, 0)),
                  pl.BlockSpec((tm, d), lambda i: (i, 0))],
        out_specs=pl.BlockSpec((tm, d), lambda i: (i, 0)),
        out_shape=jax.ShapeDtypeStruct((m, d), F32),
        compiler_params=_cparams(("arbitrary",)),
        name="attn_out",
    )(om, od, w_bf, g.reshape(1, d), x)


def _mlp_body(x_ref, g1_ref, wu_ref, wd_ref, g2_ref, y_ref, xn_sc, acc_sc):
    f = pl.program_id(1)

    @pl.when(f == 0)
    def _():
        xn_sc[...] = _rms(x_ref[...], g1_ref[...]).astype(BF16)
        acc_sc[...] = jnp.zeros_like(acc_sc)

    h = jnp.dot(xn_sc[...], wu_ref[...], preferred_element_type=F32)
    h = jnp.square(jnp.maximum(h, 0.0))
    acc_sc[...] += jnp.dot(h.astype(BF16), wd_ref[...], preferred_element_type=F32)

    @pl.when(f == pl.num_programs(1) - 1)
    def _():
        y_ref[...] = x_ref[...] + _rms(acc_sc[...], g2_ref[...])


def _mlp(x, g1, wu_bf, wd_bf, g2, tm, tf):
    m, d = x.shape
    ff = wu_bf.shape[1]
    return pl.pallas_call(
        _mlp_body,
        grid=(m // tm, ff // tf),
        in_specs=[pl.BlockSpec((tm, d), lambda i, f: (i, 0)),
                  pl.BlockSpec((1, d), lambda i, f: (0, 0)),
                  pl.BlockSpec((d, tf), lambda i, f: (0, f)),
                  pl.BlockSpec((tf, d), lambda i, f: (f, 0)),
                  pl.BlockSpec((1, d), lambda i, f: (0, 0))],
        out_specs=pl.BlockSpec((tm, d), lambda i, f: (i, 0)),
        out_shape=jax.ShapeDtypeStruct((m, d), F32),
        scratch_shapes=[pltpu.VMEM((tm, d), BF16), pltpu.VMEM((tm, d), F32)],
        compiler_params=_cparams(("arbitrary", "arbitrary")),
        name="mlp",
    )(x, g1.reshape(1, d), wu_bf, wd_bf, g2.reshape(1, d))


def _causal_mask(st):
    kidx = lax.broadcasted_iota(jnp.int32, st.shape, 0)
    qidx = lax.broadcasted_iota(jnp.int32, st.shape, 1)
    return jnp.where(kidx <= qidx, st, NEG)


def _tile_attention(chains, nblk, blk):
    nk = nblk * blk
    for kh, qs, _, s_ref, _, _ in chains:
        s_ref[0:nk, :] = lax.dot_general(kh, qs, NT_DIMS, preferred_element_type=F32)
    ms = []
    for _, _, _, s_ref, _, bias in chains:
        m = None
        for n in range(nblk):
            rows = slice(n * blk, (n + 1) * blk)
            piece = s_ref[rows, :]
            if n == nblk - 1:
                piece = _causal_mask(piece)
                s_ref[rows, :] = piece
            elif bias is not None:
                piece = piece + bias[n:n + 1, :]
                s_ref[rows, :] = piece
            pm = jnp.max(piece, axis=0, keepdims=True)
            m = pm if m is None else jnp.maximum(m, pm)
        ms.append(m)
    ls = []
    for (_, _, _, s_ref, p_ref, _), m in zip(chains, ms):
        l = None
        for n in range(nblk):
            rows = slice(n * blk, (n + 1) * blk)
            p = jnp.exp2(s_ref[rows, :] - m)
            ps = jnp.sum(p, axis=0, keepdims=True)
            l = ps if l is None else l + ps
            p_ref[rows, :] = p.astype(BF16)
        ls.append(l)
    accs = [jnp.dot(vt, p_ref[0:nk, :], preferred_element_type=F32) for _, _, vt, _, p_ref, _ in chains]
    return [acc / l for acc, l in zip(accs, ls)]


def _moba_select_bias(gt, k_eff):
    npast = gt.shape[0]
    rid = lax.broadcasted_iota(jnp.int32, gt.shape, 0)
    rank = jnp.zeros(gt.shape, jnp.int32)
    for mm in range(npast):
        gm = gt[mm:mm + 1, :]
        beats = (gm > gt) | ((gm == gt) & (rid > mm))
        rank = rank + beats.astype(jnp.int32)
    return jnp.where(rank < k_eff, 0.0, NEG)


def _moba_body(q_ref, k_ref, v_ref, o_ref, kmean_sc, kh_sc, vt_sc, qf_sc, qs_sc, s_sc, p_sc, ot_sc, *, nb):
    qi = pl.program_id(1)
    blk = MOBA_BLOCK
    hd = HEAD_DIM
    k_eff = min(MOBA_TOPK, nb)

    @pl.when(qi == 0)
    def _prep():
        for n in range(nb):
            rows = slice(n * blk, (n + 1) * blk)
            kblk = k_ref[rows, :]
            kmean = jnp.sum(kblk, axis=0, keepdims=True) * (1.0 / blk)
            vtb = v_ref[rows, :].T
            for h in range(MOBA_HEADS):
                hs = slice(h * hd, (h + 1) * hd)
                kmean_sc[h, n:n + 1, :] = kmean[:, hs]
                kh_sc[h, rows, :] = kblk[:, hs].astype(BF16)
                vt_sc[h, :, rows] = vtb[hs, :].astype(BF16)

    for h in range(MOBA_HEADS):
        qh = q_ref[:, h * hd:(h + 1) * hd]
        qf_sc[h] = qh
        qs_sc[h] = (qh * SCALE_LOG2E).astype(BF16)

    for qq in range(nb):
        @pl.when(qi == qq)
        def _tile(qq=qq):
            nk = (qq + 1) * blk

            def head_group(hg, carry):
                chains = []
                for i in range(ATTN_CHAINS):
                    h = ATTN_CHAINS * hg + i
                    bias = None
                    if qq > k_eff:
                        gt = lax.dot_general(kmean_sc[h, 0:qq, :], qf_sc[h], NT_DIMS, precision=HIGHEST,
                                             preferred_element_type=F32)
                        bias = _moba_select_bias(gt, k_eff)
                    chains.append((kh_sc[h, 0:nk, :], qs_sc[h], vt_sc[h, :, 0:nk], s_sc.at[i],
                                   p_sc.at[i], bias))
                o = _tile_attention(chains, qq + 1, blk)
                rows = ATTN_CHAINS * hd
                ot_sc[pl.ds(pl.multiple_of(hg * rows, rows), rows), :] = jnp.concatenate(o, axis=0)
                return carry

            lax.fori_loop(0, MOBA_HEADS // ATTN_CHAINS, head_group, 0)

    o_ref[...] = ot_sc[...].T


def _moba_prompt(q, k, v, batch, seq):
    w = q.shape[1]
    blk = MOBA_BLOCK
    nb = seq // blk
    return pl.pallas_call(
        functools.partial(_moba_body, nb=nb),
        grid=(batch, nb),
        in_specs=[pl.BlockSpec((blk, w), lambda b, i: (b * nb + i, 0)),
                  pl.BlockSpec((seq, w), lambda b, i: (b, 0)),
                  pl.BlockSpec((seq, w), lambda b, i: (b, 0))],
        out_specs=pl.BlockSpec((blk, w), lambda b, i: (b * nb + i, 0)),
        out_shape=jax.ShapeDtypeStruct(q.shape, F32),
        scratch_shapes=[pltpu.VMEM((MOBA_HEADS, nb, HEAD_DIM), F32),
                        pltpu.VMEM((MOBA_HEADS, seq, HEAD_DIM), BF16),
                        pltpu.VMEM((MOBA_HEADS, HEAD_DIM, seq), BF16),
                        pltpu.VMEM((MOBA_HEADS, blk, HEAD_DIM), F32),
                        pltpu.VMEM((MOBA_HEADS, blk, HEAD_DIM), BF16),
                        pltpu.VMEM((ATTN_CHAINS, seq, blk), F32),
                        pltpu.VMEM((ATTN_CHAINS, seq, blk), BF16),
                        pltpu.VMEM((w, blk), F32)],
        compiler_params=_cparams(("arbitrary", "arbitrary")),
        name="moba_prompt",
    )(q, k, v)


def _diff_lambda(lam_ref, lam_init):
    lv = lam_ref[...]
    s1 = jnp.sum(lv[0:1, :] * lv[1:2, :], axis=-1, keepdims=True)
    s2 = jnp.sum(lv[2:3, :] * lv[3:4, :], axis=-1, keepdims=True)
    return jnp.exp(s1) - jnp.exp(s2) + lam_init


def _diff_body(q_ref, k_ref, v_ref, lam_ref, gsub_ref, o_ref, kh_sc, vt_sc, qs_sc, s_sc, p_sc, ot_sc,
               *, nb, tq, lam_init):
    qi = pl.program_id(1)
    hd = HEAD_DIM
    nrow = 2 * DIFF_HEADS

    @pl.when(qi == 0)
    def _prep():
        for n in range(nb):
            rows = slice(n * tq, (n + 1) * tq)
            kblk = k_ref[rows, :]
            vtb = v_ref[rows, :].T
            for r in range(nrow):
                kh_sc[r, rows, :] = kblk[:, r * hd:(r + 1) * hd].astype(BF16)
            for h in range(DIFF_HEADS):
                vt_sc[h, :, rows] = vtb[h * DIFF_DIM:(h + 1) * DIFF_DIM, :].astype(BF16)

    for r in range(nrow):
        qs_sc[r] = (q_ref[:, r * hd:(r + 1) * hd] * SCALE_LOG2E).astype(BF16)
    lam = _diff_lambda(lam_ref, lam_init)

    for qq in range(nb):
        @pl.when(qi == qq)
        def _tile(qq=qq):
            nk = (qq + 1) * tq

            hpi = ATTN_CHAINS // 2

            def head_group(hg, carry):
                chains = []
                for i in range(ATTN_CHAINS):
                    r = ATTN_CHAINS * hg + i
                    chains.append((kh_sc[r, 0:nk, :], qs_sc[r], vt_sc[hpi * hg + i // 2, :, 0:nk],
                                   s_sc.at[i], p_sc.at[i], None))
                o = _tile_attention(chains, qq + 1, tq)
                for j in range(hpi):
                    od = o[2 * j] - lam * o[2 * j + 1]
                    ms = jnp.mean(od * od, axis=0, keepdims=True)
                    od = od * lax.rsqrt(ms + EPS) * gsub_ref[...]
                    row0 = pl.multiple_of((hpi * hg + j) * DIFF_DIM, DIFF_DIM)
                    ot_sc[pl.ds(row0, DIFF_DIM), :] = od * (1.0 - lam_init)
                return carry

            lax.fori_loop(0, 2 * DIFF_HEADS // ATTN_CHAINS, head_group, 0)

    o_ref[...] = ot_sc[...].T


def _diff_prompt(q, k, v, lam4, gsub, batch, seq, lam_init, tq=256):
    w = q.shape[1]
    nb = seq // tq
    return pl.pallas_call(
        functools.partial(_diff_body, nb=nb, tq=tq, lam_init=lam_init),
        grid=(batch, nb),
        in_specs=[pl.BlockSpec((tq, w), lambda b, i: (b * nb + i, 0)),
                  pl.BlockSpec((seq, w), lambda b, i: (b, 0)),
                  pl.BlockSpec((seq, w), lambda b, i: (b, 0)),
                  pl.BlockSpec((4, HEAD_DIM), lambda b, i: (0, 0)),
                  pl.BlockSpec((DIFF_DIM, 1), lambda b, i: (0, 0))],
        out_specs=pl.BlockSpec((tq, w), lambda b, i: (b * nb + i, 0)),
        out_shape=jax.ShapeDtypeStruct(q.shape, F32),
        scratch_shapes=[pltpu.VMEM((2 * DIFF_HEADS, seq, HEAD_DIM), BF16),
                        pltpu.VMEM((DIFF_HEADS, DIFF_DIM, seq), BF16),
                        pltpu.VMEM((2 * DIFF_HEADS, tq, HEAD_DIM), BF16),
                        pltpu.VMEM((ATTN_CHAINS, seq, tq), F32),
                        pltpu.VMEM((ATTN_CHAINS, seq, tq), BF16),
                        pltpu.VMEM((w, tq), F32)],
        compiler_params=_cparams(("arbitrary", "arbitrary")),
        name="diff_prompt",
    )(q, k, v, lam4, gsub.reshape(DIFF_DIM, 1))


def _gelu_tanh(x):
    return 0.5 * x * (1.0 + jnp.tanh(math.sqrt(2.0 / math.pi) * (x + 0.044715 * (x * x * x))))


def _log_sigmoid(x):
    return jnp.minimum(x, 0.0) - jnp.log1p(jnp.exp(-jnp.abs(x)))


def _lru_gates(xc, wa_ref, ba_ref, wx_ref, bx_ref, lam_ref):
    bw = wa_ref.shape[1]
    xcb = xc.astype(BF16)
    ra, rx = [], []
    for n in range(RNN_BLOCKS):
        xn = xcb[:, n * bw:(n + 1) * bw]
        ra.append(jnp.dot(xn, wa_ref[n], preferred_element_type=F32))
        rx.append(jnp.dot(xn, wx_ref[n], preferred_element_type=F32))
    r = jax.nn.sigmoid(jnp.concatenate(ra, axis=1) + ba_ref[...])
    ig = jax.nn.sigmoid(jnp.concatenate(rx, axis=1) + bx_ref[...])
    log_a = LRU_C * r * _log_sigmoid(lam_ref[...])
    a = jnp.exp(log_a)
    b = jnp.sqrt(-jnp.tanh(log_a) * (1.0 + a * a)) * (ig * xc)
    return a, b


def _rglru_body(gate_ref, xb_ref, x_ref, cw_ref, cb_ref, wa_ref, ba_ref, wx_ref, bx_ref, lam_ref,
                wo_ref, g_ref, y_ref, buf_ref, hl_ref, xpad_sc, a_sc, h_sc, hc_sc, *, tc):
    c = pl.program_id(1)
    hist = CONV_W - 1

    @pl.when(c == 0)
    def _():
        xpad_sc[0:8, :] = jnp.zeros((8, xpad_sc.shape[1]), F32)
        hc_sc[...] = jnp.zeros_like(hc_sc)

    @pl.when(c > 0)
    def _():
        xpad_sc[0:8, :] = xpad_sc[tc:tc + 8, :]

    xpad_sc[8:8 + tc, :] = xb_ref[...]
    xc = cb_ref[...] + xpad_sc[8 - hist:8 - hist + tc, :] * cw_ref[0:1, :]
    for i in range(1, CONV_W):
        xc = xc + xpad_sc[8 - hist + i:8 - hist + i + tc, :] * cw_ref[i:i + 1, :]
    a, b = _lru_gates(xc, wa_ref, ba_ref, wx_ref, bx_ref, lam_ref)
    a_sc[...] = a
    h_sc[...] = b

    def step(t, h):
        h = a_sc[pl.ds(t, 1), :] * h + h_sc[pl.ds(t, 1), :]
        h_sc[pl.ds(t, 1), :] = h
        return h

    h = lax.fori_loop(0, tc, step, hc_sc[...], unroll=8)
    hc_sc[...] = h
    y = (h_sc[...] * _gelu_tanh(gate_ref[...])).astype(BF16)
    y = jnp.dot(y, wo_ref[...], preferred_element_type=F32)
    y_ref[...] = x_ref[...] + _rms(y, g_ref[...])

    @pl.when(c == pl.num_programs(1) - 1)
    def _():
        buf_ref[...] = xpad_sc[8 + tc - hist:8 + tc, :]
        hl_ref[...] = h


def _rglru_prompt(gate, xb, x, cw, cb, wa_bf, ba, wx_bf, bx, lam, wo_bf, g, batch, seq, tc=256):
    m, d = x.shape
    nc = seq // tc
    bw = d // RNN_BLOCKS
    row = lambda b, c: (b * nc + c, 0)
    const2 = lambda b, c: (0, 0)
    const3 = lambda b, c: (0, 0, 0)
    vec = pl.BlockSpec((1, d), const2)
    return pl.pallas_call(
        functools.partial(_rglru_body, tc=tc),
        grid=(batch, nc),
        in_specs=[pl.BlockSpec((tc, d), row), pl.BlockSpec((tc, d), row), pl.BlockSpec((tc, d), row),
                  pl.BlockSpec((CONV_W, d), const2), vec,
                  pl.BlockSpec((RNN_BLOCKS, bw, bw), const3), vec,
                  pl.BlockSpec((RNN_BLOCKS, bw, bw), const3), vec, vec,
                  pl.BlockSpec((d, d), const2), vec],
        out_specs=[pl.BlockSpec((tc, d), row),
                   pl.BlockSpec((None, CONV_W - 1, d), lambda b, c: (b, 0, 0)),
                   pl.BlockSpec((None, 1, d), lambda b, c: (b, 0, 0))],
        out_shape=[jax.ShapeDtypeStruct((m, d), F32),
                   jax.ShapeDtypeStruct((batch, CONV_W - 1, d), F32),
                   jax.ShapeDtypeStruct((batch, 1, d), F32)],
        scratch_shapes=[pltpu.VMEM((tc + 8, d), F32), pltpu.VMEM((tc, d), F32),
                        pltpu.VMEM((tc, d), F32), pltpu.VMEM((1, d), F32)],
        compiler_params=_cparams(("arbitrary", "arbitrary")),
        name="rglru_prompt",
    )(gate, xb, x, cw, cb.reshape(1, d), wa_bf, ba.reshape(1, d), wx_bf, bx.reshape(1, d),
      lam.reshape(1, d), wo_bf, g.reshape(1, d))


def _rglru_step_body(gate_ref, xb_ref, x_ref, conv_ref, h0_ref, cw_ref, cb_ref, wa_ref, ba_ref,
                     wx_ref, bx_ref, lam_ref, wo_ref, g_ref, y_ref, h_ref):
    xc = cb_ref[...] + xb_ref[...] * cw_ref[CONV_W - 1:CONV_W, :]
    for i in range(CONV_W - 1):
        xc = xc + conv_ref[i] * cw_ref[i:i + 1, :]
    a, b = _lru_gates(xc, wa_ref, ba_ref, wx_ref, bx_ref, lam_ref)
    h = a * h0_ref[...] + b
    h_ref[...] = h
    y = (h * _gelu_tanh(gate_ref[...])).astype(BF16)
    y = jnp.dot(y, wo_ref[...], preferred_element_type=F32)
    y_ref[...] = x_ref[...] + _rms(y, g_ref[...])


def _rglru_step(gate, xb, x, conv_t, h0, cw, cb, wa_bf, ba, wx_bf, bx, lam, wo_bf, g):
    m, d = x.shape
    return pl.pallas_call(
        _rglru_step_body,
        out_shape=[jax.ShapeDtypeStruct((m, d), F32), jax.ShapeDtypeStruct((m, d), F32)],
        compiler_params=pltpu.CompilerParams(vmem_limit_bytes=VMEM_LIMIT),
        name="rglru_step",
    )(gate, xb, x, conv_t, h0, cw, cb.reshape(1, d), wa_bf, ba.reshape(1, d), wx_bf, bx.reshape(1, d),
      lam.reshape(1, d), wo_bf, g.reshape(1, d))


def _key_minor(cache):
    nd = cache.ndim
    t = jnp.transpose(cache, (0, 1) + tuple(range(3, nd)) + (2,))
    return t.reshape(t.shape[:2] + (-1,) + t.shape[-2:])


def _page_scores(kt_ref, qb):
    return jnp.sum(kt_ref[...] * qb, axis=1)


def _moba_gate_body(pt_ref, q_ref, *refs, pp, ppb):
    del pt_ref
    k_refs = refs[:pp]
    sel_ref, qb_sc, g_sc = refs[pp:pp + 3]
    s_idx = pl.program_id(1)
    page = k_refs[0].shape[-1]
    nblk = g_sc.shape[0]

    @pl.when(s_idx == 0)
    def _():
        qb_sc[...] = jnp.broadcast_to(q_ref[...], qb_sc.shape)

    qb = qb_sc[...]
    for j in range(pp // ppb):
        tot = _page_scores(k_refs[ppb * j], qb)
        for t in range(1, ppb):
            tot = tot + _page_scores(k_refs[ppb * j + t], qb)
        g = jnp.sum(tot, axis=-1, keepdims=True) * (1.0 / (ppb * page))
        g_sc[s_idx * (pp // ppb) + j] = jnp.broadcast_to(g, g_sc.shape[1:])

    @pl.when(s_idx == pl.num_programs(1) - 1)
    def _():
        g = g_sc[...]
        bid = lax.broadcasted_iota(jnp.int32, g.shape, 0)
        lid = lax.broadcasted_iota(jnp.int32, sel_ref.shape, 1)
        out = jnp.zeros(sel_ref.shape, jnp.int32)
        for t in range(MOBA_TOPK):
            mx = jnp.max(g, axis=0, keepdims=True)
            idx = jnp.min(jnp.where(g == mx, bid, nblk), axis=0, keepdims=True)
            out = jnp.where(lid == t, idx[0], out)
            g = jnp.where(bid == idx, -jnp.inf, g)
        sel_ref[...] = out


def _moba_gate(q_col, kt, pt_flat, li, batch, n_pages, pp=16):
    _, _, nh, hd, page = kt.shape
    ppb = MOBA_BLOCK // page
    nblk = n_pages // ppb

    def page_spec(j):
        return pl.BlockSpec((None, None, nh, hd, page),
                            lambda b, s, pt: (li, pt[b * n_pages + s * pp + j], 0, 0, 0))

    return pl.pallas_call(
        functools.partial(_moba_gate_body, pp=pp, ppb=ppb),
        grid_spec=pltpu.PrefetchScalarGridSpec(
            num_scalar_prefetch=1, grid=(batch, n_pages // pp),
            in_specs=[pl.BlockSpec((None, nh, hd, 1), lambda b, s, pt: (b, 0, 0, 0))]
            + [page_spec(j) for j in range(pp)],
            out_specs=pl.BlockSpec((None, nh, LANES), lambda b, s, pt: (b, 0, 0)),
            scratch_shapes=[pltpu.VMEM((nh, hd, page), F32), pltpu.VMEM((nblk, nh, LANES), F32)]),
        out_shape=jax.ShapeDtypeStruct((batch, nh, LANES), jnp.int32),
        compiler_params=_cparams(("arbitrary", "arbitrary")),
        name="moba_gate",
    )(pt_flat, q_col, *([kt] * pp))


def _moba_step_body(sel_ref, pt_ref, q_ref, kn_ref, vn_ref, *refs, nsl):
    del sel_ref, pt_ref
    k_refs, v_refs, o_ref = refs[:nsl], refs[nsl:2 * nsl], refs[2 * nsl]
    q = q_ref[...] * SCALE
    q8 = jnp.broadcast_to(q, (8, q.shape[1]))
    kt = jnp.concatenate([r[...] for r in k_refs], axis=1).astype(BF16)
    vt = jnp.concatenate([r[...] for r in v_refs], axis=1).astype(BF16)
    s = jnp.dot(q8.astype(BF16), kt, preferred_element_type=F32)
    s_self = jnp.sum(q8 * kn_ref[...], axis=-1, keepdims=True)
    m = jnp.maximum(jnp.max(s, axis=-1, keepdims=True), s_self)
    p = jnp.exp(s - m)
    p_self = jnp.exp(s_self - m)
    l = jnp.sum(p, axis=-1, keepdims=True) + p_self
    pv = lax.dot_general(p.astype(BF16), vt, NT_DIMS, preferred_element_type=F32)
    o_ref[...] = (pv + p_self * vn_ref[...]) / l


def _moba_step(q4, kn4, vn4, kt, vt, sel_flat, pt_flat, li, batch, n_pages):
    _, _, nh, hd, page = kt.shape
    ppb = MOBA_BLOCK // page
    nsl = MOBA_TOPK * ppb

    def slab_spec(t, j):
        def imap(b, h, sel, pt):
            blk = sel[(b * nh + h) * MOBA_TOPK + t]
            return (li, pt[b * n_pages + blk * ppb + j], h, 0, 0)
        return pl.BlockSpec((None, None, None, hd, page), imap)

    slabs = [slab_spec(t, j) for t in range(MOBA_TOPK) for j in range(ppb)]
    row = pl.BlockSpec((None, None, 1, hd), lambda b, h, sel, pt: (b, h, 0, 0))
    return pl.pallas_call(
        functools.partial(_moba_step_body, nsl=nsl),
        grid_spec=pltpu.PrefetchScalarGridSpec(
            num_scalar_prefetch=2, grid=(batch, nh),
            in_specs=[row, row, row] + slabs + slabs,
            out_specs=pl.BlockSpec((None, None, 8, hd), lambda b, h, sel, pt: (b, h, 0, 0))),
        out_shape=jax.ShapeDtypeStruct((batch, nh, 8, hd), F32),
        compiler_params=_cparams(("arbitrary", "arbitrary")),
        name="moba_step",
    )(sel_flat, pt_flat, q4, kn4, vn4, *([kt] * nsl), *([vt] * nsl))


def _diff_step_body(pt_ref, q_ref, kn_ref, vn_ref, lam_ref, gsub_ref, *refs, pp, lam_init):
    del pt_ref
    k_refs, v_refs = refs[:pp], refs[pp:2 * pp]
    o_ref, qb_sc, m_sc, l_sc, acc_sc = refs[2 * pp:2 * pp + 5]
    s_idx = pl.program_id(1)
    page = k_refs[0].shape[-1]

    @pl.when(s_idx == 0)
    def _():
        qb_sc[...] = jnp.broadcast_to(q_ref[...] * SCALE, qb_sc.shape)
        m_sc[...] = jnp.full(m_sc.shape, NEG, F32)
        l_sc[...] = jnp.zeros_like(l_sc)
        acc_sc[...] = jnp.zeros_like(acc_sc)

    qb = qb_sc[...]
    s = jnp.concatenate([_page_scores(r, qb) for r in k_refs], axis=1)
    m_old = m_sc[...]
    m_new = jnp.maximum(m_old, jnp.max(s, axis=-1, keepdims=True))
    alpha = jnp.exp(m_old - m_new)
    p = jnp.exp(s - m_new)
    l_sc[...] = alpha * l_sc[...] + jnp.sum(p, axis=-1, keepdims=True)
    m_sc[...] = m_new
    pb = p.astype(BF16)
    for h in range(DIFF_HEADS):
        vh = jnp.concatenate([r[pl.ds(h, page, stride=DIFF_HEADS), :] for r in v_refs], axis=0)
        acc_sc[h] = alpha * acc_sc[h] + jnp.dot(pb, vh.astype(BF16), preferred_element_type=F32)

    @pl.when(s_idx == pl.num_programs(1) - 1)
    def _():
        s_self = jnp.sum(q_ref[...] * SCALE * kn_ref[...], axis=1)
        m_old = m_sc[...]
        m_new = jnp.maximum(m_old, s_self)
        alpha = jnp.exp(m_old - m_new)
        p_self = jnp.exp(s_self - m_new)
        l = alpha * l_sc[...] + p_self
        lam = _diff_lambda(lam_ref, lam_init)
        for h in range(DIFF_HEADS):
            vn = vn_ref[:, h * DIFF_DIM:(h + 1) * DIFF_DIM]
            o = (alpha * acc_sc[h] + p_self * vn) / l
            od = o[2 * h:2 * h + 1, :] - lam * o[2 * h + 1:2 * h + 2, :]
            o_ref[:, h * DIFF_DIM:(h + 1) * DIFF_DIM] = _rms(od, gsub_ref[...]) * (1.0 - lam_init)


def _diff_step(q_col, kn_col, vn3, lam4, gsub, kt, v2, pt_flat, li, batch, n_pages, lam_init, pp=8):
    _, _, nrow, hd, page = kt.shape
    vrows, dv = v2.shape[2:]
    w = vn3.shape[-1]

    def k_spec(j):
        return pl.BlockSpec((None, None, nrow, hd, page),
                            lambda b, s, pt: (li, pt[b * n_pages + s * pp + j], 0, 0, 0))

    def v_spec(j):
        return pl.BlockSpec((None, None, vrows, dv),
                            lambda b, s, pt: (li, pt[b * n_pages + s * pp + j], 0, 0))

    col = pl.BlockSpec((None, nrow, hd, 1), lambda b, s, pt: (b, 0, 0, 0))
    row = pl.BlockSpec((None, 1, w), lambda b, s, pt: (b, 0, 0))
    return pl.pallas_call(
        functools.partial(_diff_step_body, pp=pp, lam_init=lam_init),
        grid_spec=pltpu.PrefetchScalarGridSpec(
            num_scalar_prefetch=1, grid=(batch, n_pages // pp),
            in_specs=[col, col, row,
                      pl.BlockSpec((4, HEAD_DIM), lambda b, s, pt: (0, 0)),
                      pl.BlockSpec((1, DIFF_DIM), lambda b, s, pt: (0, 0))]
            + [k_spec(j) for j in range(pp)] + [v_spec(j) for j in range(pp)],
            out_specs=row,
            scratch_shapes=[pltpu.VMEM((nrow, hd, page), F32), pltpu.VMEM((nrow, 1), F32),
                            pltpu.VMEM((nrow, 1), F32), pltpu.VMEM((DIFF_HEADS, nrow, dv), F32)]),
        out_shape=jax.ShapeDtypeStruct((batch, 1, w), F32),
        compiler_params=_cparams(("arbitrary", "arbitrary")),
        name="diff_step",
    )(pt_flat, q_col, kn_col, vn3, lam4, gsub.reshape(1, DIFF_DIM), *([kt] * pp), *([v2] * pp))


def kernel(x_prompt, x_sample, cache_moba_k, cache_moba_v, cache_diff_k, cache_diff_v, state_conv, state_rnn, page_table, attn_g_pre, attn_w_in, diff_lambda_q1, diff_lambda_k1, diff_lambda_q2, diff_lambda_k2, diff_g_sub, attn_w_out, attn_g_post, rnn_g_pre, rnn_w_in, rnn_conv_w, rnn_conv_b, rnn_w_a, rnn_b_a, rnn_w_x, rnn_b_x, rnn_lambda, rnn_w_out, rnn_g_post, mlp_g_pre, mlp_w_up, mlp_w_down, mlp_g_post):
    batch, seq, d = x_prompt.shape
    dec_batch, dec_seq, _ = x_sample.shape
    assert dec_seq == 1
    depth = mlp_w_up.shape[0]
    na, n_pool, page = cache_moba_k.shape[:3]
    n_pages = page_table.shape[1]
    past_len = n_pages * page
    mw = MOBA_HEADS * HEAD_DIM
    dw = DIFF_HEADS * DIFF_DIM

    xp = x_prompt.reshape(batch * seq, d)
    xs = x_sample.reshape(dec_batch, d)
    tab_p = _rope_tables(jnp.arange(seq, dtype=jnp.int32))
    tab_s = _rope_tables(jnp.full((dec_batch,), past_len, jnp.int32))
    pt_flat = page_table.reshape(-1)
    cmk_t = _key_minor(cache_moba_k)
    cmv_t = _key_minor(cache_moba_v)
    cdk_t = _key_minor(cache_diff_k)
    cdv2 = cache_diff_v.reshape(na, n_pool, page * DIFF_HEADS, DIFF_DIM)

    tm_p = 512
    attn_rope = (True, True, False, True, True, False)
    outs = {k: [] for k in ("pmk", "pmv", "pdk", "pdv", "pconv", "prnn",
                            "smk", "smv", "sdk", "sdv", "sconv", "srnn")}
    for layer in range(depth):
        li = layer // 2
        if layer % 2 == 0:
            lam_init = 0.8 - 0.6 * math.exp(-0.3 * layer)
            w_in = attn_w_in[li].astype(BF16)
            w_out = attn_w_out[li].astype(BF16)
            lam4 = jnp.stack([diff_lambda_q1[li], diff_lambda_k1[li], diff_lambda_q2[li], diff_lambda_k2[li]])
            mq, mk, mv, dq, dk, dv = _in_proj(xp, attn_g_pre[li], w_in, tab_p, attn_rope, tm_p)
            o_m = _moba_prompt(mq, mk, mv, batch, seq)
            o_d = _diff_prompt(dq, dk, dv, lam4, diff_g_sub[li], batch, seq, lam_init)
            xp = _attn_out(o_m, o_d, w_out, attn_g_post[li], xp, tm_p)
            outs["pmk"].append(mk.reshape(batch, seq, MOBA_HEADS, HEAD_DIM))
            outs["pmv"].append(mv.reshape(batch, seq, MOBA_HEADS, HEAD_DIM))
            outs["pdk"].append(dk.reshape(batch, seq, DIFF_HEADS, 2, HEAD_DIM))
            outs["pdv"].append(dv.reshape(batch, seq, DIFF_HEADS, DIFF_DIM))
            mq, mk, mv, dq, dk, dv = _in_proj(xs, attn_g_pre[li], w_in, tab_s, attn_rope, dec_batch)
            col = lambda t: t.reshape(dec_batch, -1, HEAD_DIM, 1)
            hrow = lambda t: t.reshape(dec_batch, MOBA_HEADS, 1, HEAD_DIM)
            sel = _moba_gate(col(mq), cmk_t, pt_flat, li, dec_batch, n_pages)
            sel = sel[:, :, :MOBA_TOPK].reshape(-1)
            o_m = _moba_step(hrow(mq), hrow(mk), hrow(mv), cmk_t, cmv_t, sel, pt_flat, li,
                             dec_batch, n_pages)[:, :, 0, :]
            o_d = _diff_step(col(dq), col(dk), dv.reshape(dec_batch, 1, dw), lam4, diff_g_sub[li],
                             cdk_t, cdv2, pt_flat, li, dec_batch, n_pages, lam_init)
            xs = _attn_out(o_m.reshape(dec_batch, mw), o_d.reshape(dec_batch, dw), w_out,
                           attn_g_post[li], xs, dec_batch)
            outs["smk"].append(mk.reshape(dec_batch, 1, MOBA_HEADS, HEAD_DIM))
            outs["smv"].append(mv.reshape(dec_batch, 1, MOBA_HEADS, HEAD_DIM))
            outs["sdk"].append(dk.reshape(dec_batch, 1, DIFF_HEADS, 2, HEAD_DIM))
            outs["sdv"].append(dv.reshape(dec_batch, 1, DIFF_HEADS, DIFF_DIM))
        else:
            w_in = rnn_w_in[li].astype(BF16)
            wts = (rnn_conv_w[li], rnn_conv_b[li], rnn_w_a[li].astype(BF16), rnn_b_a[li].reshape(-1),
                   rnn_w_x[li].astype(BF16), rnn_b_x[li].reshape(-1), rnn_lambda[li],
                   rnn_w_out[li].astype(BF16), rnn_g_post[li])
            gate, xb = _in_proj(xp, rnn_g_pre[li], w_in, tab_p, (False, False), tm_p)
            xp, cbuf, hlast = _rglru_prompt(gate, xb, xp, *wts, batch, seq)
            outs["pconv"].append(cbuf)
            outs["prnn"].append(hlast.reshape(batch, d))
            gate, xb = _in_proj(xs, rnn_g_pre[li], w_in, tab_s, (False, False), dec_batch)
            conv_t = jnp.swapaxes(state_conv[li], 0, 1)
            xs, hnew = _rglru_step(gate, xb, xs, conv_t, state_rnn[li], *wts)
            outs["sconv"].append(jnp.concatenate([state_conv[li][:, 1:], xb[:, None, :]], axis=1))
            outs["srnn"].append(hnew)
        wu = mlp_w_up[layer].astype(BF16)
        wd = mlp_w_down[layer].astype(BF16)
        xp = _mlp(xp, mlp_g_pre[layer], wu, wd, mlp_g_post[layer], 1024, 512)
        xs = _mlp(xs, mlp_g_pre[layer], wu, wd, mlp_g_post[layer], dec_batch, 512)

    st = lambda k: jnp.stack(outs[k])
    return (xp.reshape(batch, seq, d), xs.reshape(dec_batch, 1, d),
            st("pmk"), st("pmv"), st("pdk"), st("pdv"), st("pconv"), st("prnn"),
            st("smk"), st("smv"), st("sdk"), st("sdv"), st("sconv"), st("srnn"))
```

```python
import functools
import math

import jax
import jax.numpy as jnp
from jax import lax
from jax.experimental import pallas as pl
from jax.experimental.pallas import tpu as pltpu

F32 = jnp.float32
BF16 = jnp.bfloat16
HIGHEST = lax.Precision.HIGHEST

HEAD_DIM = 64
ROT_DIM = HEAD_DIM // 4
ROPE_THETA = 500000.0
MOBA_HEADS = 8
MOBA_BLOCK = 256
MOBA_TOPK = 3
DIFF_HEADS = 4
DIFF_DIM = 2 * HEAD_DIM
RNN_BLOCKS = 4
CONV_W = 4
LRU_C = 8.0
EPS = 1e-6
LANES = 128
NEG = -1e30
SCALE = HEAD_DIM ** -0.5
SCALE_LOG2E = SCALE * math.log2(math.e)
NT_DIMS = (((1,), (1,)), ((), ()))
ATTN_CHAINS = 4
VMEM_LIMIT = 52 * 1024 * 1024


def _cparams(sem):
    return pltpu.CompilerParams(dimension_semantics=sem, vmem_limit_bytes=VMEM_LIMIT)


def _rms(x, g):
    ms = jnp.mean(x * x, axis=-1, keepdims=True)
    return x * lax.rsqrt(ms + EPS) * g


def _rope_tables(pos):
    half = ROT_DIM // 2
    inv = jnp.exp(-math.log(ROPE_THETA) * jnp.arange(half, dtype=F32) * (2.0 / ROT_DIM))
    ang = pos.astype(F32)[:, None] * inv[None, :]
    cos, sin = jnp.cos(ang), jnp.sin(ang)
    n = pos.shape[0]
    pad = jnp.zeros((n, HEAD_DIM - ROT_DIM), F32)
    c64 = jnp.concatenate([cos, cos, pad + 1.0], axis=1)
    sa64 = jnp.concatenate([-sin, jnp.zeros((n, half), F32), pad], axis=1)
    sb64 = jnp.concatenate([jnp.zeros((n, half), F32), sin, pad], axis=1)
    rep = LANES // HEAD_DIM
    return tuple(jnp.tile(t, (1, rep)) for t in (c64, sa64, sb64))


def _in_proj_body(x_ref, g_ref, w_ref, c_ref, sa_ref, sb_ref, *outs, rope_flags, tn):
    xn = _rms(x_ref[...], g_ref[...]).astype(BF16)
    for j, out_ref in enumerate(outs):
        u = jnp.dot(xn, w_ref[:, j * tn:(j + 1) * tn], preferred_element_type=F32)
        if rope_flags[j]:
            c, sa, sb = c_ref[...], sa_ref[...], sb_ref[...]
            for k in range(tn // LANES):
                uk = u[:, k * LANES:(k + 1) * LANES]
                out_ref[:, k * LANES:(k + 1) * LANES] = (
                    uk * c + pltpu.roll(uk, LANES - ROT_DIM // 2, 1) * sa
                    + pltpu.roll(uk, ROT_DIM // 2, 1) * sb)
        else:
            out_ref[...] = u


def _in_proj(x, g, w_bf, tables, rope_flags, tm):
    m, d = x.shape
    n_out = len(rope_flags)
    n = w_bf.shape[1]
    tn = n // n_out
    npos = tables[0].shape[0] // tm
    tab_spec = pl.BlockSpec((tm, LANES), lambda i: (i % npos, 0))
    return pl.pallas_call(
        functools.partial(_in_proj_body, rope_flags=rope_flags, tn=tn),
        grid=(m // tm,),
        in_specs=[pl.BlockSpec((tm, d), lambda i: (i, 0)),
                  pl.BlockSpec((1, d), lambda i: (0, 0)),
                  pl.BlockSpec((d, n), lambda i: (0, 0)),
                  tab_spec, tab_spec, tab_spec],
        out_specs=[pl.BlockSpec((tm, tn), lambda i: (i, 0))] * n_out,
        out_shape=[jax.ShapeDtypeStruct((m, tn), F32)] * n_out,
        compiler_params=_cparams(("arbitrary",)),
        name="in_proj",
    )(x, g.reshape(1, d), w_bf, *tables)


def _attn_out_body(om_ref, od_ref, w_ref, g_ref, x_ref, y_ref):
    km = om_ref.shape[1]
    o = jnp.dot(om_ref[...].astype(BF16), w_ref[:km, :], preferred_element_type=F32)
    o = o + jnp.dot(od_ref[...].astype(BF16), w_ref[km:, :], preferred_element_type=F32)
    y_ref[...] = x_ref[...] + _rms(o, g_ref[...])


def _attn_out(om, od, w_bf, g, x, tm):
    m, d = x.shape
    km, kd = om.shape[1], od.shape[1]
    return pl.pallas_call(
        _attn_out_body,
        grid=(m // tm,),
        in_specs=[pl.BlockSpec((tm, km), lambda i: (i, 0)),
                  pl.BlockSpec((tm, kd), lambda i: (i, 0)),
                  pl.BlockSpec((km + kd, d), lambda i: (0, 0)),
                  pl.BlockSpec((1, d), lambda i: (0, 0)),
                  pl.BlockSpec((tm, d), lambda i: (i, 0))],
        out_specs=pl.BlockSpec((tm, d), lambda i: (i, 0)),
        out_shape=jax.ShapeDtypeStruct((m, d), F32),
        compiler_params=_cparams(("arbitrary",)),
        name="attn_out",
    )(om, od, w_bf, g.reshape(1, d), x)


def _mlp_body(x_ref, g1_ref, wu_ref, wd_ref, g2_ref, y_ref, xn_sc, acc_sc):
    f = pl.program_id(1)

    @pl.when(f == 0)
    def _():
        xn_sc[...] = _rms(x_ref[...], g1_ref[...]).astype(BF16)
        acc_sc[...] = jnp.zeros_like(acc_sc)

    h = jnp.dot(xn_sc[...], wu_ref[...], preferred_element_type=F32)
    h = jnp.square(jnp.maximum(h, 0.0))
    acc_sc[...] += jnp.dot(h.astype(BF16), wd_ref[...], preferred_element_type=F32)

    @pl.when(f == pl.num_programs(1) - 1)
    def _():
        y_ref[...] = x_ref[...] + _rms(acc_sc[...], g2_ref[...])


def _mlp(x, g1, wu_bf, wd_bf, g2, tm, tf):
    m, d = x.shape
    ff = wu_bf.shape[1]
    return pl.pallas_call(
        _mlp_body,
        grid=(m // tm, ff // tf),
        in_specs=[pl.BlockSpec((tm, d), lambda i, f: (i, 0)),
                  pl.BlockSpec((1, d), lambda i, f: (0, 0)),
                  pl.BlockSpec((d, tf), lambda i, f: (0, f)),
                  pl.BlockSpec((tf, d), lambda i, f: (f, 0)),
                  pl.BlockSpec((1, d), lambda i, f: (0, 0))],
        out_specs=pl.BlockSpec((tm, d), lambda i, f: (i, 0)),
        out_shape=jax.ShapeDtypeStruct((m, d), F32),
        scratch_shapes=[pltpu.VMEM((tm, d), BF16), pltpu.VMEM((tm, d), F32)],
        compiler_params=_cparams(("arbitrary", "arbitrary")),
        name="mlp",
    )(x, g1.reshape(1, d), wu_bf, wd_bf, g2.reshape(1, d))


def _causal_mask(st):
    kidx = lax.broadcasted_iota(jnp.int32, st.shape, 0)
    qidx = lax.broadcasted_iota(jnp.int32, st.shape, 1)
    return jnp.where(kidx <= qidx, st, NEG)


def _tile_attention(chains, nblk, blk):
    nk = nblk * blk
    for kh, qs, _, s_ref, _, _ in chains:
        s_ref[0:nk, :] = lax.dot_general(kh, qs, NT_DIMS, preferred_element_type=F32)
    ms = []
    for _, _, _, s_ref, _, bias in chains:
        m = None
        for n in range(nblk):
            rows = slice(n * blk, (n + 1) * blk)
            piece = s_ref[rows, :]
            if n == nblk - 1:
                piece = _causal_mask(piece)
                s_ref[rows, :] = piece
            elif bias is not None:
                piece = piece + bias[n:n + 1, :]
                s_ref[rows, :] = piece
            pm = jnp.max(piece, axis=0, keepdims=True)
            m = pm if m is None else jnp.maximum(m, pm)
        ms.append(m)
    ls = []
    for (_, _, _, s_ref, p_ref, _), m in zip(chains, ms):
        l = None
        for n in range(nblk):
            rows = slice(n * blk, (n + 1) * blk)
            p = jnp.exp2(s_ref[rows, :] - m)
            ps = jnp.sum(p, axis=0, keepdims=True)
            l = ps if l is None else l + ps
            p_ref[rows, :] = p.astype(BF16)
        ls.append(l)
    accs = [jnp.dot(vt, p_ref[0:nk, :], preferred_element_type=F32) for _, _, vt, _, p_ref, _ in chains]
    return [acc / l for acc, l in zip(accs, ls)]


def _moba_select_bias(gt, k_eff):
    npast = gt.shape[0]
    rid = lax.broadcasted_iota(jnp.int32, gt.shape, 0)
    rank = jnp.zeros(gt.shape, jnp.int32)
    for mm in range(npast):
        gm = gt[mm:mm + 1, :]
        beats = (gm > gt) | ((gm == gt) & (rid > mm))
        rank = rank + beats.astype(jnp.int32)
    return jnp.where(rank < k_eff, 0.0, NEG)


def _moba_body(q_ref, k_ref, v_ref, o_ref, kmean_sc, kh_sc, vt_sc, qf_sc, qs_sc, s_sc, p_sc, ot_sc, *, nb):
    qi = pl.program_id(1)
    blk = MOBA_BLOCK
    hd = HEAD_DIM
    k_eff = min(MOBA_TOPK, nb)

    @pl.when(qi == 0)
    def _prep():
        for n in range(nb):
            rows = slice(n * blk, (n + 1) * blk)
            kblk = k_ref[rows, :]
            kmean = jnp.sum(kblk, axis=0, keepdims=True) * (1.0 / blk)
            vtb = v_ref[rows, :].T
            for h in range(MOBA_HEADS):
                hs = slice(h * hd, (h + 1) * hd)
                kmean_sc[h, n:n + 1, :] = kmean[:, hs]
                kh_sc[h, rows, :] = kblk[:, hs].astype(BF16)
                vt_sc[h, :, rows] = vtb[hs, :].astype(BF16)

    for h in range(MOBA_HEADS):
        qh = q_ref[:, h * hd:(h + 1) * hd]
        qf_sc[h] = qh
        qs_sc[h] = (qh * SCALE_LOG2E).astype(BF16)

    for qq in range(nb):
        @pl.when(qi == qq)
        def _tile(qq=qq):
            nk = (qq + 1) * blk

            def head_group(hg, carry):
                chains = []
                for i in range(ATTN_CHAINS):
                    h = ATTN_CHAINS * hg + i
                    bias = None
                    if qq > k_eff:
                        gt = lax.dot_general(kmean_sc[h, 0:qq, :], qf_sc[h], NT_DIMS, precision=HIGHEST,
                                             preferred_element_type=F32)
                        bias = _moba_select_bias(gt, k_eff)
                    chains.append((kh_sc[h, 0:nk, :], qs_sc[h], vt_sc[h, :, 0:nk], s_sc.at[i],
                                   p_sc.at[i], bias))
                o = _tile_attention(chains, qq + 1, blk)
                rows = ATTN_CHAINS * hd
                ot_sc[pl.ds(pl.multiple_of(hg * rows, rows), rows), :] = jnp.concatenate(o, axis=0)
                return carry

            lax.fori_loop(0, MOBA_HEADS // ATTN_CHAINS, head_group, 0)

    o_ref[...] = ot_sc[...].T


def _moba_prompt(q, k, v, batch, seq):
    w = q.shape[1]
    blk = MOBA_BLOCK
    nb = seq // blk
    return pl.pallas_call(
        functools.partial(_moba_body, nb=nb),
        grid=(batch, nb),
        in_specs=[pl.BlockSpec((blk, w), lambda b, i: (b * nb + i, 0)),
                  pl.BlockSpec((seq, w), lambda b, i: (b, 0)),
                  pl.BlockSpec((seq, w), lambda b, i: (b, 0))],
        out_specs=pl.BlockSpec((blk, w), lambda b, i: (b * nb + i, 0)),
        out_shape=jax.ShapeDtypeStruct(q.shape, F32),
        scratch_shapes=[pltpu.VMEM((MOBA_HEADS, nb, HEAD_DIM), F32),
                        pltpu.VMEM((MOBA_HEADS, seq, HEAD_DIM), BF16),
                        pltpu.VMEM((MOBA_HEADS, HEAD_DIM, seq), BF16),
                        pltpu.VMEM((MOBA_HEADS, blk, HEAD_DIM), F32),
                        pltpu.VMEM((MOBA_HEADS, blk, HEAD_DIM), BF16),
                        pltpu.VMEM((ATTN_CHAINS, seq, blk), F32),
                        pltpu.VMEM((ATTN_CHAINS, seq, blk), BF16),
                        pltpu.VMEM((w, blk), F32)],
        compiler_params=_cparams(("arbitrary", "arbitrary")),
        name="moba_prompt",
    )(q, k, v)


def _diff_lambda(lam_ref, lam_init):
    lv = lam_ref[...]
    s1 = jnp.sum(lv[0:1, :] * lv[1:2, :], axis=-1, keepdims=True)
    s2 = jnp.sum(lv[2:3, :] * lv[3:4, :], axis=-1, keepdims=True)
    return jnp.exp(s1) - jnp.exp(s2) + lam_init


def _diff_body(q_ref, k_ref, v_ref, lam_ref, gsub_ref, o_ref, kh_sc, vt_sc, qs_sc, s_sc, p_sc, ot_sc,
               *, nb, tq, lam_init):
    qi = pl.program_id(1)
    hd = HEAD_DIM
    nrow = 2 * DIFF_HEADS

    @pl.when(qi == 0)
    def _prep():
        for n in range(nb):
            rows = slice(n * tq, (n + 1) * tq)
            kblk = k_ref[rows, :]
            vtb = v_ref[rows, :].T
            for r in range(nrow):
                kh_sc[r, rows, :] = kblk[:, r * hd:(r + 1) * hd].astype(BF16)
            for h in range(DIFF_HEADS):
                vt_sc[h, :, rows] = vtb[h * DIFF_DIM:(h + 1) * DIFF_DIM, :].astype(BF16)

    for r in range(nrow):
        qs_sc[r] = (q_ref[:, r * hd:(r + 1) * hd] * SCALE_LOG2E).astype(BF16)
    lam = _diff_lambda(lam_ref, lam_init)

    for qq in range(nb):
        @pl.when(qi == qq)
        def _tile(qq=qq):
            nk = (qq + 1) * tq

            hpi = ATTN_CHAINS // 2

            def head_group(hg, carry):
                chains = []
                for i in range(ATTN_CHAINS):
                    r = ATTN_CHAINS * hg + i
                    chains.append((kh_sc[r, 0:nk, :], qs_sc[r], vt_sc[hpi * hg + i // 2, :, 0:nk],
                                   s_sc.at[i], p_sc.at[i], None))
                o = _tile_attention(chains, qq + 1, tq)
                for j in range(hpi):
                    od = o[2 * j] - lam * o[2 * j + 1]
                    ms = jnp.mean(od * od, axis=0, keepdims=True)
                    od = od * lax.rsqrt(ms + EPS) * gsub_ref[...]
                    row0 = pl.multiple_of((hpi * hg + j) * DIFF_DIM, DIFF_DIM)
                    ot_sc[pl.ds(row0, DIFF_DIM), :] = od * (1.0 - lam_init)
                return carry

            lax.fori_loop(0, 2 * DIFF_HEADS // ATTN_CHAINS, head_group, 0)

    o_ref[...] = ot_sc[...].T


def _diff_prompt(q, k, v, lam4, gsub, batch, seq, lam_init, tq=256):
    w = q.shape[1]
    nb = seq // tq
    return pl.pallas_call(
        functools.partial(_diff_body, nb=nb, tq=tq, lam_init=lam_init),
        grid=(batch, nb),
        in_specs=[pl.BlockSpec((tq, w), lambda b, i: (b * nb + i, 0)),
                  pl.BlockSpec((seq, w), lambda b, i: (b, 0)),
                  pl.BlockSpec((seq, w), lambda b, i: (b, 0)),
                  pl.BlockSpec((4, HEAD_DIM), lambda b, i: (0, 0)),
                  pl.BlockSpec((DIFF_DIM, 1), lambda b, i: (0, 0))],
        out_specs=pl.BlockSpec((tq, w), lambda b, i: (b * nb + i, 0)),
        out_shape=jax.ShapeDtypeStruct(q.shape, F32),
        scratch_shapes=[pltpu.VMEM((2 * DIFF_HEADS, seq, HEAD_DIM), BF16),
                        pltpu.VMEM((DIFF_HEADS, DIFF_DIM, seq), BF16),
                        pltpu.VMEM((2 * DIFF_HEADS, tq, HEAD_DIM), BF16),
                        pltpu.VMEM((ATTN_CHAINS, seq, tq), F32),
                        pltpu.VMEM((ATTN_CHAINS, seq, tq), BF16),
                        pltpu.VMEM((w, tq), F32)],
        compiler_params=_cparams(("arbitrary", "arbitrary")),
        name="diff_prompt",
    )(q, k, v, lam4, gsub.reshape(DIFF_DIM, 1))


def _gelu_tanh(x):
    return 0.5 * x * (1.0 + jnp.tanh(math.sqrt(2.0 / math.pi) * (x + 0.044715 * (x * x * x))))


def _log_sigmoid(x):
    return jnp.minimum(x, 0.0) - jnp.log1p(jnp.exp(-jnp.abs(x)))


def _lru_gates(xc, wa_ref, ba_ref, wx_ref, bx_ref, lam_ref):
    bw = wa_ref.shape[1]
    xcb = xc.astype(BF16)
    ra, rx = [], []
    for n in range(RNN_BLOCKS):
        xn = xcb[:, n * bw:(n + 1) * bw]
        ra.append(jnp.dot(xn, wa_ref[n], preferred_element_type=F32))
        rx.append(jnp.dot(xn, wx_ref[n], preferred_element_type=F32))
    r = jax.nn.sigmoid(jnp.concatenate(ra, axis=1) + ba_ref[...])
    ig = jax.nn.sigmoid(jnp.concatenate(rx, axis=1) + bx_ref[...])
    log_a = LRU_C * r * _log_sigmoid(lam_ref[...])
    a = jnp.exp(log_a)
    b = jnp.sqrt(-jnp.tanh(log_a) * (1.0 + a * a)) * (ig * xc)
    return a, b


def _rglru_body(gate_ref, xb_ref, x_ref, cw_ref, cb_ref, wa_ref, ba_ref, wx_ref, bx_ref, lam_ref,
                wo_ref, g_ref, y_ref, buf_ref, hl_ref, xpad_sc, a_sc, h_sc, hc_sc, *, tc):
    c = pl.program_id(1)
    hist = CONV_W - 1

    @pl.when(c == 0)
    def _():
        xpad_sc[0:8, :] = jnp.zeros((8, xpad_sc.shape[1]), F32)
        hc_sc[...] = jnp.zeros_like(hc_sc)

    @pl.when(c > 0)
    def _():
        xpad_sc[0:8, :] = xpad_sc[tc:tc + 8, :]

    xpad_sc[8:8 + tc, :] = xb_ref[...]
    xc = cb_ref[...] + xpad_sc[8 - hist:8 - hist + tc, :] * cw_ref[0:1, :]
    for i in range(1, CONV_W):
        xc = xc + xpad_sc[8 - hist + i:8 - hist + i + tc, :] * cw_ref[i:i + 1, :]
    a, b = _lru_gates(xc, wa_ref, ba_ref, wx_ref, bx_ref, lam_ref)
    a_sc[...] = a
    h_sc[...] = b

    def step(t, h):
        h = a_sc[pl.ds(t, 1), :] * h + h_sc[pl.ds(t, 1), :]
        h_sc[pl.ds(t, 1), :] = h
        return h

    h = lax.fori_loop(0, tc, step, hc_sc[...], unroll=8)
    hc_sc[...] = h
    y = (h_sc[...] * _gelu_tanh(gate_ref[...])).astype(BF16)
    y = jnp.dot(y, wo_ref[...], preferred_element_type=F32)
    y_ref[...] = x_ref[...] + _rms(y, g_ref[...])

    @pl.when(c == pl.num_programs(1) - 1)
    def _():
        buf_ref[...] = xpad_sc[8 + tc - hist:8 + tc, :]
        hl_ref[...] = h


def _rglru_prompt(gate, xb, x, cw, cb, wa_bf, ba, wx_bf, bx, lam, wo_bf, g, batch, seq, tc=256):
    m, d = x.shape
    nc = seq // tc
    bw = d // RNN_BLOCKS
    row = lambda b, c: (b * nc + c, 0)
    const2 = lambda b, c: (0, 0)
    const3 = lambda b, c: (0, 0, 0)
    vec = pl.BlockSpec((1, d), const2)
    return pl.pallas_call(
        functools.partial(_rglru_body, tc=tc),
        grid=(batch, nc),
        in_specs=[pl.BlockSpec((tc, d), row), pl.BlockSpec((tc, d), row), pl.BlockSpec((tc, d), row),
                  pl.BlockSpec((CONV_W, d), const2), vec,
                  pl.BlockSpec((RNN_BLOCKS, bw, bw), const3), vec,
                  pl.BlockSpec((RNN_BLOCKS, bw, bw), const3), vec, vec,
                  pl.BlockSpec((d, d), const2), vec],
        out_specs=[pl.BlockSpec((tc, d), row),
                   pl.BlockSpec((None, CONV_W - 1, d), lambda b, c: (b, 0, 0)),
                   pl.BlockSpec((None, 1, d), lambda b, c: (b, 0, 0))],
        out_shape=[jax.ShapeDtypeStruct((m, d), F32),
                   jax.ShapeDtypeStruct((batch, CONV_W - 1, d), F32),
                   jax.ShapeDtypeStruct((batch, 1, d), F32)],
        scratch_shapes=[pltpu.VMEM((tc + 8, d), F32), pltpu.VMEM((tc, d), F32),
                        pltpu.VMEM((tc, d), F32), pltpu.VMEM((1, d), F32)],
        compiler_params=_cparams(("arbitrary", "arbitrary")),
        name="rglru_prompt",
    )(gate, xb, x, cw, cb.reshape(1, d), wa_bf, ba.reshape(1, d), wx_bf, bx.reshape(1, d),
      lam.reshape(1, d), wo_bf, g.reshape(1, d))


def _rglru_step_body(gate_ref, xb_ref, x_ref, conv_ref, h0_ref, cw_ref, cb_ref, wa_ref, ba_ref,
                     wx_ref, bx_ref, lam_ref, wo_ref, g_ref, y_ref, h_ref):
    xc = cb_ref[...] + xb_ref[...] * cw_ref[CONV_W - 1:CONV_W, :]
    for i in range(CONV_W - 1):
        xc = xc + conv_ref[i] * cw_ref[i:i + 1, :]
    a, b = _lru_gates(xc, wa_ref, ba_ref, wx_ref, bx_ref, lam_ref)
    h = a * h0_ref[...] + b
    h_ref[...] = h
    y = (h * _gelu_tanh(gate_ref[...])).astype(BF16)
    y = jnp.dot(y, wo_ref[...], preferred_element_type=F32)
    y_ref[...] = x_ref[...] + _rms(y, g_ref[...])


def _rglru_step(gate, xb, x, conv_t, h0, cw, cb, wa_bf, ba, wx_bf, bx, lam, wo_bf, g):
    m, d = x.shape
    return pl.pallas_call(
        _rglru_step_body,
        out_shape=[jax.ShapeDtypeStruct((m, d), F32), jax.ShapeDtypeStruct((m, d), F32)],
        compiler_params=pltpu.CompilerParams(vmem_limit_bytes=VMEM_LIMIT),
        name="rglru_step",
    )(gate, xb, x, conv_t, h0, cw, cb.reshape(1, d), wa_bf, ba.reshape(1, d), wx_bf, bx.reshape(1, d),
      lam.reshape(1, d), wo_bf, g.reshape(1, d))


def _key_minor(cache):
    nd = cache.ndim
    t = jnp.transpose(cache, (0, 1) + tuple(range(3, nd)) + (2,))
    return t.reshape(t.shape[:2] + (-1,) + t.shape[-2:])


def _page_scores(kt_ref, qb):
    return jnp.sum(kt_ref[...] * qb, axis=1)


def _page_stream(pt_ref, streams, li, n_pages, pp, compute):
    b = pl.program_id(0)
    nsteps = n_pages // pp
    assert nsteps % 2 == 0

    def copies(row, s, slot):
        base = row * n_pages + s * pp
        return [pltpu.make_async_copy(hbm.at[li, pt_ref[base + j]], buf.at[slot, j], sem.at[slot, j])
                for hbm, buf, sem in streams for j in range(pp)]

    def start(row, s, slot):
        for c in copies(row, s, slot):
            c.start()

    @pl.when(b == 0)
    def _():
        start(0, 0, 0)

    def two_steps(i, carry):
        for slot in range(2):
            s = 2 * i + slot

            @pl.when(s + 1 < nsteps)
            def _():
                start(b, s + 1, 1 - slot)

            if slot == 1:
                @pl.when((s + 1 == nsteps) & (b + 1 < pl.num_programs(0)))
                def _():
                    start(b + 1, 0, 0)

            for c in copies(b, s, slot):
                c.wait()
            compute(s, slot)
        return carry

    lax.fori_loop(0, nsteps // 2, two_steps, 0)


def _moba_gate_body(pt_ref, q_ref, kt_hbm, sel_ref, kbuf, ksem, qb_sc, g_sc, *, li, n_pages, pp, ppb):
    page = kbuf.shape[-1]
    nblk = g_sc.shape[0]
    qb_sc[...] = jnp.broadcast_to(q_ref[...], qb_sc.shape)

    def compute(s, slot):
        qb = qb_sc[...]
        for j in range(pp // ppb):
            tot = _page_scores(kbuf.at[slot, ppb * j], qb)
            for t in range(1, ppb):
                tot = tot + _page_scores(kbuf.at[slot, ppb * j + t], qb)
            g = jnp.sum(tot, axis=-1, keepdims=True) * (1.0 / (ppb * page))
            g_sc[s * (pp // ppb) + j] = jnp.broadcast_to(g, g_sc.shape[1:])

    _page_stream(pt_ref, [(kt_hbm, kbuf, ksem)], li, n_pages, pp, compute)

    g = g_sc[...]
    bid = lax.broadcasted_iota(jnp.int32, g.shape, 0)
    lid = lax.broadcasted_iota(jnp.int32, sel_ref.shape, 1)
    out = jnp.zeros(sel_ref.shape, jnp.int32)
    for t in range(MOBA_TOPK):
        mx = jnp.max(g, axis=0, keepdims=True)
        idx = jnp.min(jnp.where(g == mx, bid, nblk), axis=0, keepdims=True)
        out = jnp.where(lid == t, idx[0], out)
        g = jnp.where(bid == idx, -jnp.inf, g)
    sel_ref[...] = out


def _moba_gate(q_col, kt, pt_flat, li, batch, n_pages, pp=16):
    _, _, nh, hd, page = kt.shape
    ppb = MOBA_BLOCK // page
    nblk = n_pages // ppb
    return pl.pallas_call(
        functools.partial(_moba_gate_body, li=li, n_pages=n_pages, pp=pp, ppb=ppb),
        grid_spec=pltpu.PrefetchScalarGridSpec(
            num_scalar_prefetch=1, grid=(batch,),
            in_specs=[pl.BlockSpec((None, nh, hd, 1), lambda b, pt: (b, 0, 0, 0)),
                      pl.BlockSpec(memory_space=pl.ANY)],
            out_specs=pl.BlockSpec((None, nh, LANES), lambda b, pt: (b, 0, 0)),
            scratch_shapes=[pltpu.VMEM((2, pp, nh, hd, page), F32),
                            pltpu.SemaphoreType.DMA((2, pp)),
                            pltpu.VMEM((nh, hd, page), F32),
                            pltpu.VMEM((nblk, nh, LANES), F32)]),
        out_shape=jax.ShapeDtypeStruct((batch, nh, LANES), jnp.int32),
        compiler_params=_cparams(("arbitrary",)),
        name="moba_gate",
    )(pt_flat, q_col, kt)


def _moba_step_body(sel_ref, pt_ref, q_ref, kn_ref, vn_ref, kt_hbm, vt_hbm, o_ref, kbuf, vbuf, ksem, vsem,
                    *, li, n_pages, ppb):
    b = pl.program_id(0)
    _, nh, nsl = kbuf.shape[:3]
    slot = b % 2

    def copies(row, slot):
        out = []
        for h in range(nh):
            for t in range(MOBA_TOPK):
                blk = sel_ref[(row * nh + h) * MOBA_TOPK + t]
                for j in range(ppb):
                    page = pt_ref[row * n_pages + blk * ppb + j]
                    i = t * ppb + j
                    out.append(pltpu.make_async_copy(kt_hbm.at[li, page, h], kbuf.at[slot, h, i],
                                                     ksem.at[slot, h, i]))
                    out.append(pltpu.make_async_copy(vt_hbm.at[li, page, h], vbuf.at[slot, h, i],
                                                     vsem.at[slot, h, i]))
        return out

    @pl.when(b == 0)
    def _():
        for c in copies(0, 0):
            c.start()

    @pl.when(b + 1 < pl.num_programs(0))
    def _():
        for c in copies(b + 1, 1 - slot):
            c.start()

    for c in copies(b, slot):
        c.wait()

    for h in range(nh):
        q8 = jnp.broadcast_to(q_ref[h] * SCALE, (8, q_ref.shape[-1]))
        kt = jnp.concatenate([kbuf[slot, h, i] for i in range(nsl)], axis=1).astype(BF16)
        vt = jnp.concatenate([vbuf[slot, h, i] for i in range(nsl)], axis=1).astype(BF16)
        s = jnp.dot(q8.astype(BF16), kt, preferred_element_type=F32)
        s_self = jnp.sum(q8 * kn_ref[h], axis=-1, keepdims=True)
        m = jnp.maximum(jnp.max(s, axis=-1, keepdims=True), s_self)
        p = jnp.exp(s - m)
        p_self = jnp.exp(s_self - m)
        l = jnp.sum(p, axis=-1, keepdims=True) + p_self
        pv = lax.dot_general(p.astype(BF16), vt, NT_DIMS, preferred_element_type=F32)
        o_ref[h] = (pv + p_self * vn_ref[h]) / l


def _moba_step(q4, kn4, vn4, kt, vt, sel_flat, pt_flat, li, batch, n_pages):
    _, _, nh, hd, page = kt.shape
    ppb = MOBA_BLOCK // page
    nsl = MOBA_TOPK * ppb
    row = pl.BlockSpec((None, nh, 1, hd), lambda b, sel, pt: (b, 0, 0, 0))
    hbm = pl.BlockSpec(memory_space=pl.ANY)
    return pl.pallas_call(
        functools.partial(_moba_step_body, li=li, n_pages=n_pages, ppb=ppb),
        grid_spec=pltpu.PrefetchScalarGridSpec(
            num_scalar_prefetch=2, grid=(batch,),
            in_specs=[row, row, row, hbm, hbm],
            out_specs=pl.BlockSpec((None, nh, 8, hd), lambda b, sel, pt: (b, 0, 0, 0)),
            scratch_shapes=[pltpu.VMEM((2, nh, nsl, hd, page), F32), pltpu.VMEM((2, nh, nsl, hd, page), F32),
                            pltpu.SemaphoreType.DMA((2, nh, nsl)), pltpu.SemaphoreType.DMA((2, nh, nsl))]),
        out_shape=jax.ShapeDtypeStruct((batch, nh, 8, hd), F32),
        compiler_params=_cparams(("arbitrary",)),
        name="moba_step",
    )(sel_flat, pt_flat, q4, kn4, vn4, kt, vt)


def _diff_step_body(pt_ref, q_ref, kn_ref, vn_ref, lam_ref, gsub_ref, kt_hbm, v_hbm, o_ref,
                    kbuf, vbuf, ksem, vsem, qb_sc, m_sc, l_sc, acc_sc, *, li, n_pages, pp, lam_init):
    page = kbuf.shape[-1]
    qb_sc[...] = jnp.broadcast_to(q_ref[...] * SCALE, qb_sc.shape)
    m_sc[...] = jnp.full(m_sc.shape, NEG, F32)
    l_sc[...] = jnp.zeros_like(l_sc)
    acc_sc[...] = jnp.zeros_like(acc_sc)

    def compute(s, slot):
        del s
        qb = qb_sc[...]
        sc = jnp.concatenate([_page_scores(kbuf.at[slot, j], qb) for j in range(pp)], axis=1)
        m_old = m_sc[...]
        m_new = jnp.maximum(m_old, jnp.max(sc, axis=-1, keepdims=True))
        alpha = jnp.exp(m_old - m_new)
        p = jnp.exp(sc - m_new)
        l_sc[...] = alpha * l_sc[...] + jnp.sum(p, axis=-1, keepdims=True)
        m_sc[...] = m_new
        pb = p.astype(BF16)
        for h in range(DIFF_HEADS):
            vh = jnp.concatenate([vbuf[slot, j, pl.ds(h, page, stride=DIFF_HEADS), :] for j in range(pp)],
                                 axis=0)
            acc_sc[h] = alpha * acc_sc[h] + jnp.dot(pb, vh.astype(BF16), preferred_element_type=F32)

    _page_stream(pt_ref, [(kt_hbm, kbuf, ksem), (v_hbm, vbuf, vsem)], li, n_pages, pp, compute)

    s_self = jnp.sum(q_ref[...] * SCALE * kn_ref[...], axis=1)
    m_old = m_sc[...]
    m_new = jnp.maximum(m_old, s_self)
    alpha = jnp.exp(m_old - m_new)
    p_self = jnp.exp(s_self - m_new)
    l = alpha * l_sc[...] + p_self
    lam = _diff_lambda(lam_ref, lam_init)
    for h in range(DIFF_HEADS):
        vn = vn_ref[:, h * DIFF_DIM:(h + 1) * DIFF_DIM]
        o = (alpha * acc_sc[h] + p_self * vn) / l
        od = o[2 * h:2 * h + 1, :] - lam * o[2 * h + 1:2 * h + 2, :]
        o_ref[:, h * DIFF_DIM:(h + 1) * DIFF_DIM] = _rms(od, gsub_ref[...]) * (1.0 - lam_init)


def _diff_step(q_col, kn_col, vn3, lam4, gsub, kt, v2, pt_flat, li, batch, n_pages, lam_init, pp=8):
    _, _, nrow, hd, page = kt.shape
    vrows, dv = v2.shape[2:]
    w = vn3.shape[-1]
    col = pl.BlockSpec((None, nrow, hd, 1), lambda b, pt: (b, 0, 0, 0))
    row = pl.BlockSpec((None, 1, w), lambda b, pt: (b, 0, 0))
    hbm = pl.BlockSpec(memory_space=pl.ANY)
    return pl.pallas_call(
        functools.partial(_diff_step_body, li=li, n_pages=n_pages, pp=pp, lam_init=lam_init),
        grid_spec=pltpu.PrefetchScalarGridSpec(
            num_scalar_prefetch=1, grid=(batch,),
            in_specs=[col, col, row,
                      pl.BlockSpec((4, HEAD_DIM), lambda b, pt: (0, 0)),
                      pl.BlockSpec((1, DIFF_DIM), lambda b, pt: (0, 0)), hbm, hbm],
            out_specs=row,
            scratch_shapes=[pltpu.VMEM((2, pp, nrow, hd, page), F32), pltpu.VMEM((2, pp, vrows, dv), F32),
                            pltpu.SemaphoreType.DMA((2, pp)), pltpu.SemaphoreType.DMA((2, pp)),
                            pltpu.VMEM((nrow, hd, page), F32), pltpu.VMEM((nrow, 1), F32),
                            pltpu.VMEM((nrow, 1), F32), pltpu.VMEM((DIFF_HEADS, nrow, dv), F32)]),
        out_shape=jax.ShapeDtypeStruct((batch, 1, w), F32),
        compiler_params=_cparams(("arbitrary",)),
        name="diff_step",
    )(pt_flat, q_col, kn_col, vn3, lam4, gsub.reshape(1, DIFF_DIM), kt, v2)


def kernel(x_prompt, x_sample, cache_moba_k, cache_moba_v, cache_diff_k, cache_diff_v, state_conv, state_rnn, page_table, attn_g_pre, attn_w_in, diff_lambda_q1, diff_lambda_k1, diff_lambda_q2, diff_lambda_k2, diff_g_sub, attn_w_out, attn_g_post, rnn_g_pre, rnn_w_in, rnn_conv_w, rnn_conv_b, rnn_w_a, rnn_b_a, rnn_w_x, rnn_b_x, rnn_lambda, rnn_w_out, rnn_g_post, mlp_g_pre, mlp_w_up, mlp_w_down, mlp_g_post):
    batch, seq, d = x_prompt.shape
    dec_batch, dec_seq, _ = x_sample.shape
    assert dec_seq == 1
    depth = mlp_w_up.shape[0]
    na, n_pool, page = cache_moba_k.shape[:3]
    n_pages = page_table.shape[1]
    past_len = n_pages * page
    mw = MOBA_HEADS * HEAD_DIM
    dw = DIFF_HEADS * DIFF_DIM

    xp = x_prompt.reshape(batch * seq, d)
    xs = x_sample.reshape(dec_batch, d)
    tab_p = _rope_tables(jnp.arange(seq, dtype=jnp.int32))
    tab_s = _rope_tables(jnp.full((dec_batch,), past_len, jnp.int32))
    pt_flat = page_table.reshape(-1)
    cmk_t = _key_minor(cache_moba_k)
    cmv_t = _key_minor(cache_moba_v)
    cdk_t = _key_minor(cache_diff_k)
    cdv2 = cache_diff_v.reshape(na, n_pool, page * DIFF_HEADS, DIFF_DIM)

    tm_p = 512
    attn_rope = (True, True, False, True, True, False)
    outs = {k: [] for k in ("pmk", "pmv", "pdk", "pdv", "pconv", "prnn",
                            "smk", "smv", "sdk", "sdv", "sconv", "srnn")}
    for layer in range(depth):
        li = layer // 2
        if layer % 2 == 0:
            lam_init = 0.8 - 0.6 * math.exp(-0.3 * layer)
            w_in = attn_w_in[li].astype(BF16)
            w_out = attn_w_out[li].astype(BF16)
            lam4 = jnp.stack([diff_lambda_q1[li], diff_lambda_k1[li], diff_lambda_q2[li], diff_lambda_k2[li]])
            mq, mk, mv, dq, dk, dv = _in_proj(xp, attn_g_pre[li], w_in, tab_p, attn_rope, tm_p)
            o_m = _moba_prompt(mq, mk, mv, batch, seq)
            o_d = _diff_prompt(dq, dk, dv, lam4, diff_g_sub[li], batch, seq, lam_init)
            xp = _attn_out(o_m, o_d, w_out, attn_g_post[li], xp, tm_p)
            outs["pmk"].append(mk.reshape(batch, seq, MOBA_HEADS, HEAD_DIM))
            outs["pmv"].append(mv.reshape(batch, seq, MOBA_HEADS, HEAD_DIM))
            outs["pdk"].append(dk.reshape(batch, seq, DIFF_HEADS, 2, HEAD_DIM))
            outs["pdv"].append(dv.reshape(batch, seq, DIFF_HEADS, DIFF_DIM))
            mq, mk, mv, dq, dk, dv = _in_proj(xs, attn_g_pre[li], w_in, tab_s, attn_rope, dec_batch)
            col = lambda t: t.reshape(dec_batch, -1, HEAD_DIM, 1)
            hrow = lambda t: t.reshape(dec_batch, MOBA_HEADS, 1, HEAD_DIM)
            sel = _moba_gate(col(mq), cmk_t, pt_flat, li, dec_batch, n_pages)
            sel = sel[:, :, :MOBA_TOPK].reshape(-1)
            o_m = _moba_step(hrow(mq), hrow(mk), hrow(mv), cmk_t, cmv_t, sel, pt_flat, li,
                             dec_batch, n_pages)[:, :, 0, :]
            o_d = _diff_step(col(dq), col(dk), dv.reshape(dec_batch, 1, dw), lam4, diff_g_sub[li],
                             cdk_t, cdv2, pt_flat, li, dec_batch, n_pages, lam_init)
            xs = _attn_out(o_m.reshape(dec_batch, mw), o_d.reshape(dec_batch, dw), w_out,
                           attn_g_post[li], xs, dec_batch)
            outs["smk"].append(mk.reshape(dec_batch, 1, MOBA_HEADS, HEAD_DIM))
            outs["smv"].append(mv.reshape(dec_batch, 1, MOBA_HEADS, HEAD_DIM))
            outs["sdk"].append(dk.reshape(dec_batch, 1, DIFF_HEADS, 2, HEAD_DIM))
            outs["sdv"].append(dv.reshape(dec_batch, 1, DIFF_HEADS, DIFF_DIM))
        else:
            w_in = rnn_w_in[li].astype(BF16)
            wts = (rnn_conv_w[li], rnn_conv_b[li], rnn_w_a[li].astype(BF16), rnn_b_a[li].reshape(-1),
                   rnn_w_x[li].astype(BF16), rnn_b_x[li].reshape(-1), rnn_lambda[li],
                   rnn_w_out[li].astype(BF16), rnn_g_post[li])
            gate, xb = _in_proj(xp, rnn_g_pre[li], w_in, tab_p, (False, False), tm_p)
            xp, cbuf, hlast = _rglru_prompt(gate, xb, xp, *wts, batch, seq)
            outs["pconv"].append(cbuf)
            outs["prnn"].append(hlast.reshape(batch, d))
            gate, xb = _in_proj(xs, rnn_g_pre[li], w_in, tab_s, (False, False), dec_batch)
            conv_t = jnp.swapaxes(state_conv[li], 0, 1)
            xs, hnew = _rglru_step(gate, xb, xs, conv_t, state_rnn[li], *wts)
            outs["sconv"].append(jnp.concatenate([state_conv[li][:, 1:], xb[:, None, :]], axis=1))
            outs["srnn"].append(hnew)
        wu = mlp_w_up[layer].astype(BF16)
        wd = mlp_w_down[layer].astype(BF16)
        xp = _mlp(xp, mlp_g_pre[layer], wu, wd, mlp_g_post[layer], 1024, 512)
        xs = _mlp(xs, mlp_g_pre[layer], wu, wd, mlp_g_post[layer], dec_batch, 512)

    st = lambda k: jnp.stack(outs[k])
    return (xp.reshape(batch, seq, d), xs.reshape(dec_batch, 1, d),
            st("pmk"), st("pmv"), st("pdk"), st("pdv"), st("pconv"), st("prnn"),
            st("smk"), st("smv"), st("sdk"), st("sdv"), st("sconv"), st("srnn"))
```

```python
import functools
import math

import jax
import jax.numpy as jnp
from jax import lax
from jax.experimental import pallas as pl
from jax.experimental.pallas import tpu as pltpu

F32 = jnp.float32
BF16 = jnp.bfloat16
HIGHEST = lax.Precision.HIGHEST

HEAD_DIM = 64
ROT_DIM = HEAD_DIM // 4
ROPE_THETA = 500000.0
MOBA_HEADS = 8
MOBA_BLOCK = 256
MOBA_TOPK = 3
DIFF_HEADS = 4
DIFF_DIM = 2 * HEAD_DIM
RNN_BLOCKS = 4
CONV_W = 4
LRU_C = 8.0
EPS = 1e-6
LANES = 128
NEG = -1e30
SCALE = HEAD_DIM ** -0.5
SCALE_LOG2E = SCALE * math.log2(math.e)
NT_DIMS = (((1,), (1,)), ((), ()))
ATTN_CHAINS = 4
VMEM_LIMIT = 52 * 1024 * 1024


def _cparams(sem):
    return pltpu.CompilerParams(dimension_semantics=sem, vmem_limit_bytes=VMEM_LIMIT)


def _rms(x, g):
    ms = jnp.mean(x * x, axis=-1, keepdims=True)
    return x * lax.rsqrt(ms + EPS) * g


def _rope_tables(pos):
    half = ROT_DIM // 2
    inv = jnp.exp(-math.log(ROPE_THETA) * jnp.arange(half, dtype=F32) * (2.0 / ROT_DIM))
    ang = pos.astype(F32)[:, None] * inv[None, :]
    cos, sin = jnp.cos(ang), jnp.sin(ang)
    n = pos.shape[0]
    pad = jnp.zeros((n, HEAD_DIM - ROT_DIM), F32)
    c64 = jnp.concatenate([cos, cos, pad + 1.0], axis=1)
    sa64 = jnp.concatenate([-sin, jnp.zeros((n, half), F32), pad], axis=1)
    sb64 = jnp.concatenate([jnp.zeros((n, half), F32), sin, pad], axis=1)
    rep = LANES // HEAD_DIM
    return tuple(jnp.tile(t, (1, rep)) for t in (c64, sa64, sb64))


def _in_proj_body(x_ref, g_ref, w_ref, c_ref, sa_ref, sb_ref, *outs, rope_flags, t_flags, tn):
    n_out = len(rope_flags)
    outs_t = iter(outs[n_out:])
    xn = _rms(x_ref[...], g_ref[...]).astype(BF16)
    for j, out_ref in enumerate(outs[:n_out]):
        u = jnp.dot(xn, w_ref[:, j * tn:(j + 1) * tn], preferred_element_type=F32)
        if rope_flags[j]:
            c, sa, sb = c_ref[...], sa_ref[...], sb_ref[...]
            u = jnp.concatenate(
                [uk * c + pltpu.roll(uk, LANES - ROT_DIM // 2, 1) * sa + pltpu.roll(uk, ROT_DIM // 2, 1) * sb
                 for uk in (u[:, k * LANES:(k + 1) * LANES] for k in range(tn // LANES))], axis=1)
        out_ref[...] = u
        if t_flags[j]:
            next(outs_t)[...] = u.T


def _in_proj(x, g, w_bf, tables, rope_flags, tm, t_flags=None):
    m, d = x.shape
    n_out = len(rope_flags)
    t_flags = t_flags or (False,) * n_out
    n = w_bf.shape[1]
    tn = n // n_out
    seq = tables[0].shape[0]
    npos = seq // tm
    tab_spec = pl.BlockSpec((tm, LANES), lambda i: (i % npos, 0))
    n_t = sum(t_flags)
    return pl.pallas_call(
        functools.partial(_in_proj_body, rope_flags=rope_flags, t_flags=t_flags, tn=tn),
        grid=(m // tm,),
        in_specs=[pl.BlockSpec((tm, d), lambda i: (i, 0)),
                  pl.BlockSpec((1, d), lambda i: (0, 0)),
                  pl.BlockSpec((d, n), lambda i: (0, 0)),
                  tab_spec, tab_spec, tab_spec],
        out_specs=[pl.BlockSpec((tm, tn), lambda i: (i, 0))] * n_out
        + [pl.BlockSpec((None, tn, tm), lambda i: (i // npos, 0, i % npos))] * n_t,
        out_shape=[jax.ShapeDtypeStruct((m, tn), F32)] * n_out
        + [jax.ShapeDtypeStruct((m // seq, tn, seq), F32)] * n_t,
        compiler_params=_cparams(("arbitrary",)),
        name="in_proj",
    )(x, g.reshape(1, d), w_bf, *tables)


def _attn_out_body(om_ref, od_ref, w_ref, g_ref, x_ref, y_ref):
    km = om_ref.shape[1]
    o = jnp.dot(om_ref[...].astype(BF16), w_ref[:km, :], preferred_element_type=F32)
    o = o + jnp.dot(od_ref[...].astype(BF16), w_ref[km:, :], preferred_element_type=F32)
    y_ref[...] = x_ref[...] + _rms(o, g_ref[...])


def _attn_out(om, od, w_bf, g, x, tm):
    m, d = x.shape
    km, kd = om.shape[1], od.shape[1]
    return pl.pallas_call(
        _attn_out_body,
        grid=(m // tm,),
        in_specs=[pl.BlockSpec((tm, km), lambda i: (i, 0)),
                  pl.BlockSpec((tm, kd), lambda i: (i, 0)),
                  pl.BlockSpec((km + kd, d), lambda i: (0, 0)),
                  pl.BlockSpec((1, d), lambda i: (0, 0)),
                  pl.BlockSpec((tm, d), lambda i: (i, 0))],
        out_specs=pl.BlockSpec((tm, d), lambda i: (i, 0)),
        out_shape=jax.ShapeDtypeStruct((m, d), F32),
        compiler_params=_cparams(("arbitrary",)),
        name="attn_out",
    )(om, od, w_bf, g.reshape(1, d), x)


def _mlp_body(x_ref, g1_ref, wu_ref, wd_ref, g2_ref, y_ref, xn_sc, acc_sc):
    f = pl.program_id(1)

    @pl.when(f == 0)
    def _():
        xn_sc[...] = _rms(x_ref[...], g1_ref[...]).astype(BF16)
        acc_sc[...] = jnp.zeros_like(acc_sc)

    h = jnp.dot(xn_sc[...], wu_ref[...], preferred_element_type=F32)
    h = jnp.square(jnp.maximum(h, 0.0))
    acc_sc[...] += jnp.dot(h.astype(BF16), wd_ref[...], preferred_element_type=F32)

    @pl.when(f == pl.num_programs(1) - 1)
    def _():
        y_ref[...] = x_ref[...] + _rms(acc_sc[...], g2_ref[...])


def _mlp(x, g1, wu_bf, wd_bf, g2, tm, tf):
    m, d = x.shape
    ff = wu_bf.shape[1]
    return pl.pallas_call(
        _mlp_body,
        grid=(m // tm, ff // tf),
        in_specs=[pl.BlockSpec((tm, d), lambda i, f: (i, 0)),
                  pl.BlockSpec((1, d), lambda i, f: (0, 0)),
                  pl.BlockSpec((d, tf), lambda i, f: (0, f)),
                  pl.BlockSpec((tf, d), lambda i, f: (f, 0)),
                  pl.BlockSpec((1, d), lambda i, f: (0, 0))],
        out_specs=pl.BlockSpec((tm, d), lambda i, f: (i, 0)),
        out_shape=jax.ShapeDtypeStruct((m, d), F32),
        scratch_shapes=[pltpu.VMEM((tm, d), BF16), pltpu.VMEM((tm, d), F32)],
        compiler_params=_cparams(("arbitrary", "arbitrary")),
        name="mlp",
    )(x, g1.reshape(1, d), wu_bf, wd_bf, g2.reshape(1, d))


def _causal_mask(st):
    kidx = lax.broadcasted_iota(jnp.int32, st.shape, 0)
    qidx = lax.broadcasted_iota(jnp.int32, st.shape, 1)
    return jnp.where(kidx <= qidx, st, NEG)


def _tile_attention(chains, nblk, blk):
    nk = nblk * blk
    for kh, qs, _, s_ref, _, _ in chains:
        s_ref[0:nk, :] = lax.dot_general(kh, qs, NT_DIMS, preferred_element_type=F32)
    ms = []
    for _, _, _, s_ref, _, bias in chains:
        m = None
        for n in range(nblk):
            rows = slice(n * blk, (n + 1) * blk)
            piece = s_ref[rows, :]
            if n == nblk - 1:
                piece = _causal_mask(piece)
                s_ref[rows, :] = piece
            elif bias is not None:
                piece = piece + bias[n:n + 1, :]
                s_ref[rows, :] = piece
            pm = jnp.max(piece, axis=0, keepdims=True)
            m = pm if m is None else jnp.maximum(m, pm)
        ms.append(m)
    ls = []
    for (_, _, _, s_ref, p_ref, _), m in zip(chains, ms):
        l = None
        for n in range(nblk):
            rows = slice(n * blk, (n + 1) * blk)
            p = jnp.exp2(s_ref[rows, :] - m)
            ps = jnp.sum(p, axis=0, keepdims=True)
            l = ps if l is None else l + ps
            p_ref[rows, :] = p.astype(BF16)
        ls.append(l)
    accs = [jnp.dot(vt, p_ref[0:nk, :], preferred_element_type=F32) for _, _, vt, _, p_ref, _ in chains]
    return [acc / l for acc, l in zip(accs, ls)]


def _moba_select_bias(gt, k_eff):
    npast = gt.shape[0]
    rid = lax.broadcasted_iota(jnp.int32, gt.shape, 0)
    rank = jnp.zeros(gt.shape, jnp.int32)
    for mm in range(npast):
        gm = gt[mm:mm + 1, :]
        beats = (gm > gt) | ((gm == gt) & (rid > mm))
        rank = rank + beats.astype(jnp.int32)
    return jnp.where(rank < k_eff, 0.0, NEG)


def _moba_body(q_ref, k_ref, v_ref, o_ref, kmean_sc, kh_sc, vt_sc, qf_sc, qs_sc, s_sc, p_sc, ot_sc, *, nb):
    qi = pl.program_id(1)
    blk = MOBA_BLOCK
    hd = HEAD_DIM
    k_eff = min(MOBA_TOPK, nb)

    @pl.when(qi == 0)
    def _prep():
        for n in range(nb):
            rows = slice(n * blk, (n + 1) * blk)
            kblk = k_ref[rows, :]
            kmean = jnp.sum(kblk, axis=0, keepdims=True) * (1.0 / blk)
            vtb = v_ref[rows, :].T
            for h in range(MOBA_HEADS):
                hs = slice(h * hd, (h + 1) * hd)
                kmean_sc[h, n:n + 1, :] = kmean[:, hs]
                kh_sc[h, rows, :] = kblk[:, hs].astype(BF16)
                vt_sc[h, :, rows] = vtb[hs, :].astype(BF16)

    for h in range(MOBA_HEADS):
        qh = q_ref[:, h * hd:(h + 1) * hd]
        qf_sc[h] = qh
        qs_sc[h] = (qh * SCALE_LOG2E).astype(BF16)

    for qq in range(nb):
        @pl.when(qi == qq)
        def _tile(qq=qq):
            nk = (qq + 1) * blk

            def head_group(hg, carry):
                chains = []
                for i in range(ATTN_CHAINS):
                    h = ATTN_CHAINS * hg + i
                    bias = None
                    if qq > k_eff:
                        gt = lax.dot_general(kmean_sc[h, 0:qq, :], qf_sc[h], NT_DIMS, precision=HIGHEST,
                                             preferred_element_type=F32)
                        bias = _moba_select_bias(gt, k_eff)
                    chains.append((kh_sc[h, 0:nk, :], qs_sc[h], vt_sc[h, :, 0:nk], s_sc.at[i],
                                   p_sc.at[i], bias))
                o = _tile_attention(chains, qq + 1, blk)
                rows = ATTN_CHAINS * hd
                ot_sc[pl.ds(pl.multiple_of(hg * rows, rows), rows), :] = jnp.concatenate(o, axis=0)
                return carry

            lax.fori_loop(0, MOBA_HEADS // ATTN_CHAINS, head_group, 0)

    o_ref[...] = ot_sc[...].T


def _moba_prompt(q, k, v, batch, seq):
    w = q.shape[1]
    blk = MOBA_BLOCK
    nb = seq // blk
    return pl.pallas_call(
        functools.partial(_moba_body, nb=nb),
        grid=(batch, nb),
        in_specs=[pl.BlockSpec((blk, w), lambda b, i: (b * nb + i, 0)),
                  pl.BlockSpec((seq, w), lambda b, i: (b, 0)),
                  pl.BlockSpec((seq, w), lambda b, i: (b, 0))],
        out_specs=pl.BlockSpec((blk, w), lambda b, i: (b * nb + i, 0)),
        out_shape=jax.ShapeDtypeStruct(q.shape, F32),
        scratch_shapes=[pltpu.VMEM((MOBA_HEADS, nb, HEAD_DIM), F32),
                        pltpu.VMEM((MOBA_HEADS, seq, HEAD_DIM), BF16),
                        pltpu.VMEM((MOBA_HEADS, HEAD_DIM, seq), BF16),
                        pltpu.VMEM((MOBA_HEADS, blk, HEAD_DIM), F32),
                        pltpu.VMEM((MOBA_HEADS, blk, HEAD_DIM), BF16),
                        pltpu.VMEM((ATTN_CHAINS, seq, blk), F32),
                        pltpu.VMEM((ATTN_CHAINS, seq, blk), BF16),
                        pltpu.VMEM((w, blk), F32)],
        compiler_params=_cparams(("arbitrary", "arbitrary")),
        name="moba_prompt",
    )(q, k, v)


def _diff_lambda(lam_ref, lam_init):
    lv = lam_ref[...]
    s1 = jnp.sum(lv[0:1, :] * lv[1:2, :], axis=-1, keepdims=True)
    s2 = jnp.sum(lv[2:3, :] * lv[3:4, :], axis=-1, keepdims=True)
    return jnp.exp(s1) - jnp.exp(s2) + lam_init


def _diff_body(q_ref, k_ref, v_ref, lam_ref, gsub_ref, o_ref, kh_sc, vt_sc, qs_sc, s_sc, p_sc, ot_sc,
               *, nb, tq, lam_init):
    qi = pl.program_id(1)
    hd = HEAD_DIM
    nrow = 2 * DIFF_HEADS

    @pl.when(qi == 0)
    def _prep():
        for n in range(nb):
            rows = slice(n * tq, (n + 1) * tq)
            kblk = k_ref[rows, :]
            vtb = v_ref[rows, :].T
            for r in range(nrow):
                kh_sc[r, rows, :] = kblk[:, r * hd:(r + 1) * hd].astype(BF16)
            for h in range(DIFF_HEADS):
                vt_sc[h, :, rows] = vtb[h * DIFF_DIM:(h + 1) * DIFF_DIM, :].astype(BF16)

    for r in range(nrow):
        qs_sc[r] = (q_ref[:, r * hd:(r + 1) * hd] * SCALE_LOG2E).astype(BF16)
    lam = _diff_lambda(lam_ref, lam_init)

    for qq in range(nb):
        @pl.when(qi == qq)
        def _tile(qq=qq):
            nk = (qq + 1) * tq

            hpi = ATTN_CHAINS // 2

            def head_group(hg, carry):
                chains = []
                for i in range(ATTN_CHAINS):
                    r = ATTN_CHAINS * hg + i
                    chains.append((kh_sc[r, 0:nk, :], qs_sc[r], vt_sc[hpi * hg + i // 2, :, 0:nk],
                                   s_sc.at[i], p_sc.at[i], None))
                o = _tile_attention(chains, qq + 1, tq)
                for j in range(hpi):
                    od = o[2 * j] - lam * o[2 * j + 1]
                    ms = jnp.mean(od * od, axis=0, keepdims=True)
                    od = od * lax.rsqrt(ms + EPS) * gsub_ref[...]
                    row0 = pl.multiple_of((hpi * hg + j) * DIFF_DIM, DIFF_DIM)
                    ot_sc[pl.ds(row0, DIFF_DIM), :] = od * (1.0 - lam_init)
                return carry

            lax.fori_loop(0, 2 * DIFF_HEADS // ATTN_CHAINS, head_group, 0)

    o_ref[...] = ot_sc[...].T


def _diff_prompt(q, k, v, lam4, gsub, batch, seq, lam_init, tq=256):
    w = q.shape[1]
    nb = seq // tq
    return pl.pallas_call(
        functools.partial(_diff_body, nb=nb, tq=tq, lam_init=lam_init),
        grid=(batch, nb),
        in_specs=[pl.BlockSpec((tq, w), lambda b, i: (b * nb + i, 0)),
                  pl.BlockSpec((seq, w), lambda b, i: (b, 0)),
                  pl.BlockSpec((seq, w), lambda b, i: (b, 0)),
                  pl.BlockSpec((4, HEAD_DIM), lambda b, i: (0, 0)),
                  pl.BlockSpec((DIFF_DIM, 1), lambda b, i: (0, 0))],
        out_specs=pl.BlockSpec((tq, w), lambda b, i: (b * nb + i, 0)),
        out_shape=jax.ShapeDtypeStruct(q.shape, F32),
        scratch_shapes=[pltpu.VMEM((2 * DIFF_HEADS, seq, HEAD_DIM), BF16),
                        pltpu.VMEM((DIFF_HEADS, DIFF_DIM, seq), BF16),
                        pltpu.VMEM((2 * DIFF_HEADS, tq, HEAD_DIM), BF16),
                        pltpu.VMEM((ATTN_CHAINS, seq, tq), F32),
                        pltpu.VMEM((ATTN_CHAINS, seq, tq), BF16),
                        pltpu.VMEM((w, tq), F32)],
        compiler_params=_cparams(("arbitrary", "arbitrary")),
        name="diff_prompt",
    )(q, k, v, lam4, gsub.reshape(DIFF_DIM, 1))


def _gelu_tanh(x):
    return 0.5 * x * (1.0 + jnp.tanh(math.sqrt(2.0 / math.pi) * (x + 0.044715 * (x * x * x))))


def _log_sigmoid(x):
    return jnp.minimum(x, 0.0) - jnp.log1p(jnp.exp(-jnp.abs(x)))


def _lru_gates(xc, wa_ref, ba_ref, wx_ref, bx_ref, lam_ref):
    bw = wa_ref.shape[1]
    xcb = xc.astype(BF16)
    ra, rx = [], []
    for n in range(RNN_BLOCKS):
        xn = xcb[:, n * bw:(n + 1) * bw]
        ra.append(jnp.dot(xn, wa_ref[n], preferred_element_type=F32))
        rx.append(jnp.dot(xn, wx_ref[n], preferred_element_type=F32))
    r = jax.nn.sigmoid(jnp.concatenate(ra, axis=1) + ba_ref[...])
    ig = jax.nn.sigmoid(jnp.concatenate(rx, axis=1) + bx_ref[...])
    log_a = LRU_C * r * _log_sigmoid(lam_ref[...])
    a = jnp.exp(log_a)
    b = jnp.sqrt(-jnp.tanh(log_a) * (1.0 + a * a)) * (ig * xc)
    return a, b


def _scan_rows(a, b_in, h0, nb, tc):
    sub = lax.broadcasted_iota(jnp.int32, (8, a.shape[1]), 0)
    rows, last = [], []
    for b in range(nb):
        h_in = h0[b:b + 1, :]
        for g in range(tc // 8):
            r0 = b * tc + 8 * g
            ga, gb = a[r0:r0 + 8, :], b_in[r0:r0 + 8, :]
            for sh in (1, 2, 4):
                keep = sub >= sh
                a_sh = jnp.where(keep, pltpu.roll(ga, sh, 0), 1.0)
                b_sh = jnp.where(keep, pltpu.roll(gb, sh, 0), 0.0)
                gb = ga * b_sh + gb
                ga = ga * a_sh
            h = ga * h_in + gb
            rows.append(h)
            h_in = h[7:8, :]
        last.append(h_in)
    return jnp.concatenate(rows, axis=0), jnp.concatenate(last, axis=0)


def _rglru_body(gate_ref, xb_ref, x_ref, cw_ref, cb_ref, wa_ref, ba_ref, wx_ref, bx_ref, lam_ref,
                wo_ref, g_ref, y_ref, buf_ref, hl_ref, xpad_sc, hc_sc):
    c = pl.program_id(0)
    nb, tc, d = xb_ref.shape
    hist = CONV_W - 1

    @pl.when(c == 0)
    def _():
        xpad_sc[:, 0:8, :] = jnp.zeros((nb, 8, d), F32)
        hc_sc[...] = jnp.zeros_like(hc_sc)

    @pl.when(c > 0)
    def _():
        xpad_sc[:, 0:8, :] = xpad_sc[:, tc:tc + 8, :]

    xpad_sc[:, 8:8 + tc, :] = xb_ref[...]
    xcs = []
    for b in range(nb):
        xc = cb_ref[...] + xpad_sc[b, 8 - hist:8 - hist + tc, :] * cw_ref[0:1, :]
        for i in range(1, CONV_W):
            xc = xc + xpad_sc[b, 8 - hist + i:8 - hist + i + tc, :] * cw_ref[i:i + 1, :]
        xcs.append(xc)
    a, b_in = _lru_gates(jnp.concatenate(xcs, axis=0), wa_ref, ba_ref, wx_ref, bx_ref, lam_ref)
    hall, h = _scan_rows(a, b_in, hc_sc[...], nb, tc)
    hc_sc[...] = h
    y = (hall * _gelu_tanh(gate_ref[...].reshape(nb * tc, d))).astype(BF16)
    y = jnp.dot(y, wo_ref[...], preferred_element_type=F32)
    y_ref[...] = (x_ref[...].reshape(nb * tc, d) + _rms(y, g_ref[...])).reshape(nb, tc, d)

    @pl.when(c == pl.num_programs(0) - 1)
    def _():
        buf_ref[...] = xpad_sc[:, 8 + tc - hist:8 + tc, :]
        hl_ref[...] = h


def _rglru_prompt(gate, xb, x, cw, cb, wa_bf, ba, wx_bf, bx, lam, wo_bf, g, batch, seq, tc=32):
    m, d = x.shape
    bw = d // RNN_BLOCKS
    chunk = pl.BlockSpec((batch, tc, d), lambda c: (0, c, 0))
    const2 = lambda c: (0, 0)
    const3 = lambda c: (0, 0, 0)
    vec = pl.BlockSpec((1, d), const2)
    r3 = lambda t: t.reshape(batch, seq, d)
    y, buf, hl = pl.pallas_call(
        _rglru_body,
        grid=(seq // tc,),
        in_specs=[chunk, chunk, chunk,
                  pl.BlockSpec((CONV_W, d), const2), vec,
                  pl.BlockSpec((RNN_BLOCKS, bw, bw), const3), vec,
                  pl.BlockSpec((RNN_BLOCKS, bw, bw), const3), vec, vec,
                  pl.BlockSpec((d, d), const2), vec],
        out_specs=[chunk,
                   pl.BlockSpec((batch, CONV_W - 1, d), const3),
                   pl.BlockSpec((batch, d), const2)],
        out_shape=[jax.ShapeDtypeStruct((batch, seq, d), F32),
                   jax.ShapeDtypeStruct((batch, CONV_W - 1, d), F32),
                   jax.ShapeDtypeStruct((batch, d), F32)],
        scratch_shapes=[pltpu.VMEM((batch, tc + 8, d), F32), pltpu.VMEM((batch, d), F32)],
        compiler_params=_cparams(("arbitrary",)),
        name="rglru_prompt",
    )(r3(gate), r3(xb), r3(x), cw, cb.reshape(1, d), wa_bf, ba.reshape(1, d), wx_bf, bx.reshape(1, d),
      lam.reshape(1, d), wo_bf, g.reshape(1, d))
    return y.reshape(m, d), buf, hl


def _rglru_step_body(gate_ref, xb_ref, x_ref, conv_ref, h0_ref, cw_ref, cb_ref, wa_ref, ba_ref,
                     wx_ref, bx_ref, lam_ref, wo_ref, g_ref, y_ref, h_ref):
    xc = cb_ref[...] + xb_ref[...] * cw_ref[CONV_W - 1:CONV_W, :]
    for i in range(CONV_W - 1):
        xc = xc + conv_ref[i] * cw_ref[i:i + 1, :]
    a, b = _lru_gates(xc, wa_ref, ba_ref, wx_ref, bx_ref, lam_ref)
    h = a * h0_ref[...] + b
    h_ref[...] = h
    y = (h * _gelu_tanh(gate_ref[...])).astype(BF16)
    y = jnp.dot(y, wo_ref[...], preferred_element_type=F32)
    y_ref[...] = x_ref[...] + _rms(y, g_ref[...])


def _rglru_step(gate, xb, x, conv_t, h0, cw, cb, wa_bf, ba, wx_bf, bx, lam, wo_bf, g):
    m, d = x.shape
    return pl.pallas_call(
        _rglru_step_body,
        out_shape=[jax.ShapeDtypeStruct((m, d), F32), jax.ShapeDtypeStruct((m, d), F32)],
        compiler_params=pltpu.CompilerParams(vmem_limit_bytes=VMEM_LIMIT),
        name="rglru_step",
    )(gate, xb, x, conv_t, h0, cw, cb.reshape(1, d), wa_bf, ba.reshape(1, d), wx_bf, bx.reshape(1, d),
      lam.reshape(1, d), wo_bf, g.reshape(1, d))


def _key_minor(cache):
    nd = cache.ndim
    t = jnp.transpose(cache, (0, 1) + tuple(range(3, nd)) + (2,))
    return t.reshape(t.shape[:2] + (-1,) + t.shape[-2:])


def _page_scores(kt_ref, qb):
    return jnp.sum(kt_ref[...] * qb, axis=1)


def _page_stream(pt_ref, streams, li, n_pages, pp, compute):
    b = pl.program_id(0)
    nsteps = n_pages // pp
    assert nsteps % 2 == 0

    def copies(row, s, slot):
        base = row * n_pages + s * pp
        return [pltpu.make_async_copy(hbm.at[li, pt_ref[base + j]], buf.at[slot, j], sem.at[slot, j])
                for hbm, buf, sem in streams for j in range(pp)]

    def start(row, s, slot):
        for c in copies(row, s, slot):
            c.start()

    @pl.when(b == 0)
    def _():
        start(0, 0, 0)

    def two_steps(i, carry):
        for slot in range(2):
            s = 2 * i + slot

            @pl.when(s + 1 < nsteps)
            def _():
                start(b, s + 1, 1 - slot)

            if slot == 1:
                @pl.when((s + 1 == nsteps) & (b + 1 < pl.num_programs(0)))
                def _():
                    start(b + 1, 0, 0)

            for c in copies(b, s, slot):
                c.wait()
            compute(s, slot)
        return carry

    lax.fori_loop(0, nsteps // 2, two_steps, 0)


def _moba_gate_body(pt_ref, q_ref, kt_hbm, sel_ref, kbuf, ksem, qb_sc, g_sc, *, li, n_pages, pp, ppb):
    page = kbuf.shape[-1]
    nblk = g_sc.shape[0]
    qb_sc[...] = jnp.broadcast_to(q_ref[...], qb_sc.shape)

    def compute(s, slot):
        qb = qb_sc[...]
        for j in range(pp // ppb):
            tot = _page_scores(kbuf.at[slot, ppb * j], qb)
            for t in range(1, ppb):
                tot = tot + _page_scores(kbuf.at[slot, ppb * j + t], qb)
            g = jnp.sum(tot, axis=-1, keepdims=True) * (1.0 / (ppb * page))
            g_sc[s * (pp // ppb) + j] = jnp.broadcast_to(g, g_sc.shape[1:])

    _page_stream(pt_ref, [(kt_hbm, kbuf, ksem)], li, n_pages, pp, compute)

    g = g_sc[...]
    bid = lax.broadcasted_iota(jnp.int32, g.shape, 0)
    lid = lax.broadcasted_iota(jnp.int32, sel_ref.shape, 1)
    out = jnp.zeros(sel_ref.shape, jnp.int32)
    for t in range(MOBA_TOPK):
        mx = jnp.max(g, axis=0, keepdims=True)
        idx = jnp.min(jnp.where(g == mx, bid, nblk), axis=0, keepdims=True)
        out = jnp.where(lid == t, idx[0], out)
        g = jnp.where(bid == idx, -jnp.inf, g)
    sel_ref[...] = out


def _moba_gate(q_col, kt, pt_flat, li, batch, n_pages, pp=16):
    _, _, nh, hd, page = kt.shape
    ppb = MOBA_BLOCK // page
    nblk = n_pages // ppb
    return pl.pallas_call(
        functools.partial(_moba_gate_body, li=li, n_pages=n_pages, pp=pp, ppb=ppb),
        grid_spec=pltpu.PrefetchScalarGridSpec(
            num_scalar_prefetch=1, grid=(batch,),
            in_specs=[pl.BlockSpec((None, nh, hd, 1), lambda b, pt: (b, 0, 0, 0)),
                      pl.BlockSpec(memory_space=pl.ANY)],
            out_specs=pl.BlockSpec((None, nh, LANES), lambda b, pt: (b, 0, 0)),
            scratch_shapes=[pltpu.VMEM((2, pp, nh, hd, page), F32),
                            pltpu.SemaphoreType.DMA((2, pp)),
                            pltpu.VMEM((nh, hd, page), F32),
                            pltpu.VMEM((nblk, nh, LANES), F32)]),
        out_shape=jax.ShapeDtypeStruct((batch, nh, LANES), jnp.int32),
        compiler_params=_cparams(("arbitrary",)),
        name="moba_gate",
    )(pt_flat, q_col, kt)


def _moba_step_body(sel_ref, pt_ref, q_ref, kn_ref, vn_ref, kt_hbm, vt_hbm, o_ref, kbuf, vbuf, ksem, vsem,
                    *, li, n_pages, ppb):
    b = pl.program_id(0)
    _, nh, nsl = kbuf.shape[:3]
    slot = b % 2

    def copies(row, slot):
        out = []
        for h in range(nh):
            for t in range(MOBA_TOPK):
                blk = sel_ref[(row * nh + h) * MOBA_TOPK + t]
                for j in range(ppb):
                    page = pt_ref[row * n_pages + blk * ppb + j]
                    i = t * ppb + j
                    out.append(pltpu.make_async_copy(kt_hbm.at[li, page, h], kbuf.at[slot, h, i],
                                                     ksem.at[slot, h, i]))
                    out.append(pltpu.make_async_copy(vt_hbm.at[li, page, h], vbuf.at[slot, h, i],
                                                     vsem.at[slot, h, i]))
        return out

    @pl.when(b == 0)
    def _():
        for c in copies(0, 0):
            c.start()

    @pl.when(b + 1 < pl.num_programs(0))
    def _():
        for c in copies(b + 1, 1 - slot):
            c.start()

    for c in copies(b, slot):
        c.wait()

    for h in range(nh):
        q8 = jnp.broadcast_to(q_ref[h] * SCALE, (8, q_ref.shape[-1]))
        kt = jnp.concatenate([kbuf[slot, h, i] for i in range(nsl)], axis=1).astype(BF16)
        vt = jnp.concatenate([vbuf[slot, h, i] for i in range(nsl)], axis=1).astype(BF16)
        s = jnp.dot(q8.astype(BF16), kt, preferred_element_type=F32)
        s_self = jnp.sum(q8 * kn_ref[h], axis=-1, keepdims=True)
        m = jnp.maximum(jnp.max(s, axis=-1, keepdims=True), s_self)
        p = jnp.exp(s - m)
        p_self = jnp.exp(s_self - m)
        l = jnp.sum(p, axis=-1, keepdims=True) + p_self
        pv = lax.dot_general(p.astype(BF16), vt, NT_DIMS, preferred_element_type=F32)
        o_ref[h] = (pv + p_self * vn_ref[h]) / l


def _moba_step(q4, kn4, vn4, kt, vt, sel_flat, pt_flat, li, batch, n_pages):
    _, _, nh, hd, page = kt.shape
    ppb = MOBA_BLOCK // page
    nsl = MOBA_TOPK * ppb
    row = pl.BlockSpec((None, nh, 1, hd), lambda b, sel, pt: (b, 0, 0, 0))
    hbm = pl.BlockSpec(memory_space=pl.ANY)
    return pl.pallas_call(
        functools.partial(_moba_step_body, li=li, n_pages=n_pages, ppb=ppb),
        grid_spec=pltpu.PrefetchScalarGridSpec(
            num_scalar_prefetch=2, grid=(batch,),
            in_specs=[row, row, row, hbm, hbm],
            out_specs=pl.BlockSpec((None, nh, 8, hd), lambda b, sel, pt: (b, 0, 0, 0)),
            scratch_shapes=[pltpu.VMEM((2, nh, nsl, hd, page), F32), pltpu.VMEM((2, nh, nsl, hd, page), F32),
                            pltpu.SemaphoreType.DMA((2, nh, nsl)), pltpu.SemaphoreType.DMA((2, nh, nsl))]),
        out_shape=jax.ShapeDtypeStruct((batch, nh, 8, hd), F32),
        compiler_params=_cparams(("arbitrary",)),
        name="moba_step",
    )(sel_flat, pt_flat, q4, kn4, vn4, kt, vt)


def _diff_step_body(pt_ref, q_ref, kn_ref, vn_ref, lam_ref, gsub_ref, kt_hbm, v_hbm, o_ref,
                    kbuf, vbuf, ksem, vsem, qb_sc, m_sc, l_sc, acc_sc, *, li, n_pages, pp, lam_init):
    page = kbuf.shape[-1]
    qb_sc[...] = jnp.broadcast_to(q_ref[...] * SCALE, qb_sc.shape)
    m_sc[...] = jnp.full(m_sc.shape, NEG, F32)
    l_sc[...] = jnp.zeros_like(l_sc)
    acc_sc[...] = jnp.zeros_like(acc_sc)

    def compute(s, slot):
        del s
        qb = qb_sc[...]
        sc = jnp.concatenate([_page_scores(kbuf.at[slot, j], qb) for j in range(pp)], axis=1)
        m_old = m_sc[...]
        m_new = jnp.maximum(m_old, jnp.max(sc, axis=-1, keepdims=True))
        alpha = jnp.exp(m_old - m_new)
        p = jnp.exp(sc - m_new)
        l_sc[...] = alpha * l_sc[...] + jnp.sum(p, axis=-1, keepdims=True)
        m_sc[...] = m_new
        pb = p.astype(BF16)
        for h in range(DIFF_HEADS):
            vh = jnp.concatenate([vbuf[slot, j, pl.ds(h, page, stride=DIFF_HEADS), :] for j in range(pp)],
                                 axis=0)
            acc_sc[h] = alpha * acc_sc[h] + jnp.dot(pb, vh.astype(BF16), preferred_element_type=F32)

    _page_stream(pt_ref, [(kt_hbm, kbuf, ksem), (v_hbm, vbuf, vsem)], li, n_pages, pp, compute)

    s_self = jnp.sum(q_ref[...] * SCALE * kn_ref[...], axis=1)
    m_old = m_sc[...]
    m_new = jnp.maximum(m_old, s_self)
    alpha = jnp.exp(m_old - m_new)
    p_self = jnp.exp(s_self - m_new)
    l = alpha * l_sc[...] + p_self
    lam = _diff_lambda(lam_ref, lam_init)
    for h in range(DIFF_HEADS):
        vn = vn_ref[:, h * DIFF_DIM:(h + 1) * DIFF_DIM]
        o = (alpha * acc_sc[h] + p_self * vn) / l
        od = o[2 * h:2 * h + 1, :] - lam * o[2 * h + 1:2 * h + 2, :]
        o_ref[:, h * DIFF_DIM:(h + 1) * DIFF_DIM] = _rms(od, gsub_ref[...]) * (1.0 - lam_init)


def _diff_step(q_col, kn_col, vn3, lam4, gsub, kt, v2, pt_flat, li, batch, n_pages, lam_init, pp=8):
    _, _, nrow, hd, page = kt.shape
    vrows, dv = v2.shape[2:]
    w = vn3.shape[-1]
    col = pl.BlockSpec((None, nrow, hd, 1), lambda b, pt: (b, 0, 0, 0))
    row = pl.BlockSpec((None, 1, w), lambda b, pt: (b, 0, 0))
    hbm = pl.BlockSpec(memory_space=pl.ANY)
    return pl.pallas_call(
        functools.partial(_diff_step_body, li=li, n_pages=n_pages, pp=pp, lam_init=lam_init),
        grid_spec=pltpu.PrefetchScalarGridSpec(
            num_scalar_prefetch=1, grid=(batch,),
            in_specs=[col, col, row,
                      pl.BlockSpec((4, HEAD_DIM), lambda b, pt: (0, 0)),
                      pl.BlockSpec((1, DIFF_DIM), lambda b, pt: (0, 0)), hbm, hbm],
            out_specs=row,
            scratch_shapes=[pltpu.VMEM((2, pp, nrow, hd, page), F32), pltpu.VMEM((2, pp, vrows, dv), F32),
                            pltpu.SemaphoreType.DMA((2, pp)), pltpu.SemaphoreType.DMA((2, pp)),
                            pltpu.VMEM((nrow, hd, page), F32), pltpu.VMEM((nrow, 1), F32),
                            pltpu.VMEM((nrow, 1), F32), pltpu.VMEM((DIFF_HEADS, nrow, dv), F32)]),
        out_shape=jax.ShapeDtypeStruct((batch, 1, w), F32),
        compiler_params=_cparams(("arbitrary",)),
        name="diff_step",
    )(pt_flat, q_col, kn_col, vn3, lam4, gsub.reshape(1, DIFF_DIM), kt, v2)


def kernel(x_prompt, x_sample, cache_moba_k, cache_moba_v, cache_diff_k, cache_diff_v, state_conv, state_rnn, page_table, attn_g_pre, attn_w_in, diff_lambda_q1, diff_lambda_k1, diff_lambda_q2, diff_lambda_k2, diff_g_sub, attn_w_out, attn_g_post, rnn_g_pre, rnn_w_in, rnn_conv_w, rnn_conv_b, rnn_w_a, rnn_b_a, rnn_w_x, rnn_b_x, rnn_lambda, rnn_w_out, rnn_g_post, mlp_g_pre, mlp_w_up, mlp_w_down, mlp_g_post):
    batch, seq, d = x_prompt.shape
    dec_batch, dec_seq, _ = x_sample.shape
    assert dec_seq == 1
    depth = mlp_w_up.shape[0]
    na, n_pool, page = cache_moba_k.shape[:3]
    n_pages = page_table.shape[1]
    past_len = n_pages * page
    mw = MOBA_HEADS * HEAD_DIM
    dw = DIFF_HEADS * DIFF_DIM

    xp = x_prompt.reshape(batch * seq, d)
    xs = x_sample.reshape(dec_batch, d)
    tab_p = _rope_tables(jnp.arange(seq, dtype=jnp.int32))
    tab_s = _rope_tables(jnp.full((dec_batch,), past_len, jnp.int32))
    pt_flat = page_table.reshape(-1)
    cmk_t = _key_minor(cache_moba_k)
    cmv_t = _key_minor(cache_moba_v)
    cdk_t = _key_minor(cache_diff_k)
    cdv2 = cache_diff_v.reshape(na, n_pool, page * DIFF_HEADS, DIFF_DIM)

    tm_p = 512
    attn_rope = (True, True, False, True, True, False)
    outs = {k: [] for k in ("pmk", "pmv", "pdk", "pdv", "pconv", "prnn",
                            "smk", "smv", "sdk", "sdv", "sconv", "srnn")}
    for layer in range(depth):
        li = layer // 2
        if layer % 2 == 0:
            lam_init = 0.8 - 0.6 * math.exp(-0.3 * layer)
            w_in = attn_w_in[li].astype(BF16)
            w_out = attn_w_out[li].astype(BF16)
            lam4 = jnp.stack([diff_lambda_q1[li], diff_lambda_k1[li], diff_lambda_q2[li], diff_lambda_k2[li]])
            mq, mk, mv, dq, dk, dv, mk_t, mv_t, dk_t = _in_proj(
                xp, attn_g_pre[li], w_in, tab_p, attn_rope, tm_p,
                t_flags=(False, True, True, False, True, False))
            o_m = _moba_prompt(mq, mk, mv, batch, seq)
            o_d = _diff_prompt(dq, dk, dv, lam4, diff_g_sub[li], batch, seq, lam_init)
            xp = _attn_out(o_m, o_d, w_out, attn_g_post[li], xp, tm_p)
            outs["pmk"].append(jnp.moveaxis(mk_t.reshape(batch, MOBA_HEADS, HEAD_DIM, seq), -1, 1))
            outs["pmv"].append(jnp.moveaxis(mv_t.reshape(batch, MOBA_HEADS, HEAD_DIM, seq), -1, 1))
            outs["pdk"].append(jnp.moveaxis(dk_t.reshape(batch, DIFF_HEADS, 2, HEAD_DIM, seq), -1, 1))
            outs["pdv"].append(dv.reshape(batch, seq, DIFF_HEADS, DIFF_DIM))
            mq, mk, mv, dq, dk, dv = _in_proj(xs, attn_g_pre[li], w_in, tab_s, attn_rope, dec_batch)
            col = lambda t: t.reshape(dec_batch, -1, HEAD_DIM, 1)
            hrow = lambda t: t.reshape(dec_batch, MOBA_HEADS, 1, HEAD_DIM)
            sel = _moba_gate(col(mq), cmk_t, pt_flat, li, dec_batch, n_pages)
            sel = sel[:, :, :MOBA_TOPK].reshape(-1)
            o_m = _moba_step(hrow(mq), hrow(mk), hrow(mv), cmk_t, cmv_t, sel, pt_flat, li,
                             dec_batch, n_pages)[:, :, 0, :]
            o_d = _diff_step(col(dq), col(dk), dv.reshape(dec_batch, 1, dw), lam4, diff_g_sub[li],
                             cdk_t, cdv2, pt_flat, li, dec_batch, n_pages, lam_init)
            xs = _attn_out(o_m.reshape(dec_batch, mw), o_d.reshape(dec_batch, dw), w_out,
                           attn_g_post[li], xs, dec_batch)
            outs["smk"].append(mk.reshape(dec_batch, 1, MOBA_HEADS, HEAD_DIM))
            outs["smv"].append(mv.reshape(dec_batch, 1, MOBA_HEADS, HEAD_DIM))
            outs["sdk"].append(dk.reshape(dec_batch, 1, DIFF_HEADS, 2, HEAD_DIM))
            outs["sdv"].append(dv.reshape(dec_batch, 1, DIFF_HEADS, DIFF_DIM))
        else:
            w_in = rnn_w_in[li].astype(BF16)
            wts = (rnn_conv_w[li], rnn_conv_b[li], rnn_w_a[li].astype(BF16), rnn_b_a[li].reshape(-1),
                   rnn_w_x[li].astype(BF16), rnn_b_x[li].reshape(-1), rnn_lambda[li],
                   rnn_w_out[li].astype(BF16), rnn_g_post[li])
            gate, xb = _in_proj(xp, rnn_g_pre[li], w_in, tab_p, (False, False), tm_p)
            xp, cbuf, hlast = _rglru_prompt(gate, xb, xp, *wts, batch, seq)
            outs["pconv"].append(cbuf)
            outs["prnn"].append(hlast.reshape(batch, d))
            gate, xb = _in_proj(xs, rnn_g_pre[li], w_in, tab_s, (False, False), dec_batch)
            conv_t = jnp.swapaxes(state_conv[li], 0, 1)
            xs, hnew = _rglru_step(gate, xb, xs, conv_t, state_rnn[li], *wts)
            outs["sconv"].append(jnp.concatenate([state_conv[li][:, 1:], xb[:, None, :]], axis=1))
            outs["srnn"].append(hnew)
        wu = mlp_w_up[layer].astype(BF16)
        wd = mlp_w_down[layer].astype(BF16)
        xp = _mlp(xp, mlp_g_pre[layer], wu, wd, mlp_g_post[layer], 1024, 512)
        xs = _mlp(xs, mlp_g_pre[layer], wu, wd, mlp_g_post[layer], dec_batch, 512)

    st = lambda k: jnp.stack(outs[k])
    return (xp.reshape(batch, seq, d), xs.reshape(dec_batch, 1, d),
            st("pmk"), st("pmv"), st("pdk"), st("pdv"), st("pconv"), st("prnn"),
            st("smk"), st("smv"), st("sdk"), st("sdv"), st("sconv"), st("srnn"))
```

```python
import functools
import math

import jax
import jax.numpy as jnp
from jax import lax
from jax.experimental import pallas as pl
from jax.experimental.pallas import tpu as pltpu

F32 = jnp.float32
BF16 = jnp.bfloat16
HIGHEST = lax.Precision.HIGHEST

HEAD_DIM = 64
ROT_DIM = HEAD_DIM // 4
ROPE_THETA = 500000.0
MOBA_HEADS = 8
MOBA_BLOCK = 256
MOBA_TOPK = 3
DIFF_HEADS = 4
DIFF_DIM = 2 * HEAD_DIM
RNN_BLOCKS = 4
CONV_W = 4
LRU_C = 8.0
EPS = 1e-6
LANES = 128
NEG = -1e30
SCALE = HEAD_DIM ** -0.5
SCALE_LOG2E = SCALE * math.log2(math.e)
NT_DIMS = (((1,), (1,)), ((), ()))
ATTN_CHAINS = 4
PAGE_SLOTS = 4
VMEM_LIMIT = 52 * 1024 * 1024


def _cparams(sem):
    return pltpu.CompilerParams(dimension_semantics=sem, vmem_limit_bytes=VMEM_LIMIT)


def _rms(x, g):
    ms = jnp.mean(x * x, axis=-1, keepdims=True)
    return x * lax.rsqrt(ms + EPS) * g


def _rope_tables(pos):
    half = ROT_DIM // 2
    inv = jnp.exp(-math.log(ROPE_THETA) * jnp.arange(half, dtype=F32) * (2.0 / ROT_DIM))
    ang = pos.astype(F32)[:, None] * inv[None, :]
    cos, sin = jnp.cos(ang), jnp.sin(ang)
    n = pos.shape[0]
    pad = jnp.zeros((n, HEAD_DIM - ROT_DIM), F32)
    c64 = jnp.concatenate([cos, cos, pad + 1.0], axis=1)
    sa64 = jnp.concatenate([-sin, jnp.zeros((n, half), F32), pad], axis=1)
    sb64 = jnp.concatenate([jnp.zeros((n, half), F32), sin, pad], axis=1)
    rep = LANES // HEAD_DIM
    return tuple(jnp.tile(t, (1, rep)) for t in (c64, sa64, sb64))


def _in_proj_body(x_ref, g_ref, w_ref, c_ref, sa_ref, sb_ref, *outs, rope_flags, t_flags, tn):
    n_out = len(rope_flags)
    outs_t = iter(outs[n_out:])
    xn = _rms(x_ref[...], g_ref[...]).astype(BF16)
    for j, out_ref in enumerate(outs[:n_out]):
        u = jnp.dot(xn, w_ref[:, j * tn:(j + 1) * tn], preferred_element_type=F32)
        if rope_flags[j]:
            c, sa, sb = c_ref[...], sa_ref[...], sb_ref[...]
            u = jnp.concatenate(
                [uk * c + pltpu.roll(uk, LANES - ROT_DIM // 2, 1) * sa + pltpu.roll(uk, ROT_DIM // 2, 1) * sb
                 for uk in (u[:, k * LANES:(k + 1) * LANES] for k in range(tn // LANES))], axis=1)
        out_ref[...] = u
        if t_flags[j]:
            next(outs_t)[...] = u.T


def _in_proj(x, g, w_bf, tables, rope_flags, tm, t_flags=None):
    m, d = x.shape
    n_out = len(rope_flags)
    t_flags = t_flags or (False,) * n_out
    n = w_bf.shape[1]
    tn = n // n_out
    seq = tables[0].shape[0]
    npos = seq // tm
    tab_spec = pl.BlockSpec((tm, LANES), lambda i: (i % npos, 0))
    n_t = sum(t_flags)
    return pl.pallas_call(
        functools.partial(_in_proj_body, rope_flags=rope_flags, t_flags=t_flags, tn=tn),
        grid=(m // tm,),
        in_specs=[pl.BlockSpec((tm, d), lambda i: (i, 0)),
                  pl.BlockSpec((1, d), lambda i: (0, 0)),
                  pl.BlockSpec((d, n), lambda i: (0, 0)),
                  tab_spec, tab_spec, tab_spec],
        out_specs=[pl.BlockSpec((tm, tn), lambda i: (i, 0))] * n_out
        + [pl.BlockSpec((None, tn, tm), lambda i: (i // npos, 0, i % npos))] * n_t,
        out_shape=[jax.ShapeDtypeStruct((m, tn), F32)] * n_out
        + [jax.ShapeDtypeStruct((m // seq, tn, seq), F32)] * n_t,
        compiler_params=_cparams(("arbitrary",)),
        name="in_proj",
    )(x, g.reshape(1, d), w_bf, *tables)


def _attn_out_body(om_ref, od_ref, w_ref, g_ref, x_ref, y_ref):
    km = om_ref.shape[1]
    o = jnp.dot(om_ref[...].astype(BF16), w_ref[:km, :], preferred_element_type=F32)
    o = o + jnp.dot(od_ref[...].astype(BF16), w_ref[km:, :], preferred_element_type=F32)
    y_ref[...] = x_ref[...] + _rms(o, g_ref[...])


def _attn_out(om, od, w_bf, g, x, tm):
    m, d = x.shape
    km, kd = om.shape[1], od.shape[1]
    return pl.pallas_call(
        _attn_out_body,
        grid=(m // tm,),
        in_specs=[pl.BlockSpec((tm, km), lambda i: (i, 0)),
                  pl.BlockSpec((tm, kd), lambda i: (i, 0)),
                  pl.BlockSpec((km + kd, d), lambda i: (0, 0)),
                  pl.BlockSpec((1, d), lambda i: (0, 0)),
                  pl.BlockSpec((tm, d), lambda i: (i, 0))],
        out_specs=pl.BlockSpec((tm, d), lambda i: (i, 0)),
        out_shape=jax.ShapeDtypeStruct((m, d), F32),
        compiler_params=_cparams(("arbitrary",)),
        name="attn_out",
    )(om, od, w_bf, g.reshape(1, d), x)


def _mlp_body(x_ref, g1_ref, wu_ref, wd_ref, g2_ref, y_ref, xn_sc, acc_sc):
    f = pl.program_id(1)

    @pl.when(f == 0)
    def _():
        xn_sc[...] = _rms(x_ref[...], g1_ref[...]).astype(BF16)
        acc_sc[...] = jnp.zeros_like(acc_sc)

    h = jnp.dot(xn_sc[...], wu_ref[...], preferred_element_type=F32)
    h = jnp.square(jnp.maximum(h, 0.0))
    acc_sc[...] += jnp.dot(h.astype(BF16), wd_ref[...], preferred_element_type=F32)

    @pl.when(f == pl.num_programs(1) - 1)
    def _():
        y_ref[...] = x_ref[...] + _rms(acc_sc[...], g2_ref[...])


def _mlp(x, g1, wu_bf, wd_bf, g2, tm, tf):
    m, d = x.shape
    ff = wu_bf.shape[1]
    return pl.pallas_call(
        _mlp_body,
        grid=(m // tm, ff // tf),
        in_specs=[pl.BlockSpec((tm, d), lambda i, f: (i, 0)),
                  pl.BlockSpec((1, d), lambda i, f: (0, 0)),
                  pl.BlockSpec((d, tf), lambda i, f: (0, f)),
                  pl.BlockSpec((tf, d), lambda i, f: (f, 0)),
                  pl.BlockSpec((1, d), lambda i, f: (0, 0))],
        out_specs=pl.BlockSpec((tm, d), lambda i, f: (i, 0)),
        out_shape=jax.ShapeDtypeStruct((m, d), F32),
        scratch_shapes=[pltpu.VMEM((tm, d), BF16), pltpu.VMEM((tm, d), F32)],
        compiler_params=_cparams(("arbitrary", "arbitrary")),
        name="mlp",
    )(x, g1.reshape(1, d), wu_bf, wd_bf, g2.reshape(1, d))


def _causal_mask(st):
    kidx = lax.broadcasted_iota(jnp.int32, st.shape, 0)
    qidx = lax.broadcasted_iota(jnp.int32, st.shape, 1)
    return jnp.where(kidx <= qidx, st, NEG)


def _tile_attention(chains, nblk, blk):
    nk = nblk * blk
    for kh, qs, _, s_ref, _, _ in chains:
        s_ref[0:nk, :] = lax.dot_general(kh, qs, NT_DIMS, preferred_element_type=F32)
    ms = []
    for _, _, _, s_ref, _, bias in chains:
        m = None
        for n in range(nblk):
            rows = slice(n * blk, (n + 1) * blk)
            piece = s_ref[rows, :]
            if n == nblk - 1:
                piece = _causal_mask(piece)
                s_ref[rows, :] = piece
            elif bias is not None:
                piece = piece + bias[n:n + 1, :]
                s_ref[rows, :] = piece
            pm = jnp.max(piece, axis=0, keepdims=True)
            m = pm if m is None else jnp.maximum(m, pm)
        ms.append(m)
    ls = []
    for (_, _, _, s_ref, p_ref, _), m in zip(chains, ms):
        l = None
        for n in range(nblk):
            rows = slice(n * blk, (n + 1) * blk)
            p = jnp.exp2(s_ref[rows, :] - m)
            ps = jnp.sum(p, axis=0, keepdims=True)
            l = ps if l is None else l + ps
            p_ref[rows, :] = p.astype(BF16)
        ls.append(l)
    accs = [jnp.dot(vt, p_ref[0:nk, :], preferred_element_type=F32) for _, _, vt, _, p_ref, _ in chains]
    return [acc / l for acc, l in zip(accs, ls)]


def _moba_select_bias(gt, k_eff):
    npast = gt.shape[0]
    rid = lax.broadcasted_iota(jnp.int32, gt.shape, 0)
    rank = jnp.zeros(gt.shape, jnp.int32)
    for mm in range(npast):
        gm = gt[mm:mm + 1, :]
        beats = (gm > gt) | ((gm == gt) & (rid > mm))
        rank = rank + beats.astype(jnp.int32)
    return jnp.where(rank < k_eff, 0.0, NEG)


def _moba_body(q_ref, k_ref, v_ref, o_ref, kmean_sc, kh_sc, vt_sc, qf_sc, qs_sc, s_sc, p_sc, ot_sc, *, nb):
    qi = pl.program_id(1)
    blk = MOBA_BLOCK
    hd = HEAD_DIM
    k_eff = min(MOBA_TOPK, nb)

    @pl.when(qi == 0)
    def _prep():
        for n in range(nb):
            rows = slice(n * blk, (n + 1) * blk)
            kblk = k_ref[rows, :]
            kmean = jnp.sum(kblk, axis=0, keepdims=True) * (1.0 / blk)
            vtb = v_ref[rows, :].T
            for h in range(MOBA_HEADS):
                hs = slice(h * hd, (h + 1) * hd)
                kmean_sc[h, n:n + 1, :] = kmean[:, hs]
                kh_sc[h, rows, :] = kblk[:, hs].astype(BF16)
                vt_sc[h, :, rows] = vtb[hs, :].astype(BF16)

    for h in range(MOBA_HEADS):
        qh = q_ref[:, h * hd:(h + 1) * hd]
        qf_sc[h] = qh
        qs_sc[h] = (qh * SCALE_LOG2E).astype(BF16)

    for qq in range(nb):
        @pl.when(qi == qq)
        def _tile(qq=qq):
            nk = (qq + 1) * blk

            def head_group(hg, carry):
                chains = []
                for i in range(ATTN_CHAINS):
                    h = ATTN_CHAINS * hg + i
                    bias = None
                    if qq > k_eff:
                        gt = lax.dot_general(kmean_sc[h, 0:qq, :], qf_sc[h], NT_DIMS, precision=HIGHEST,
                                             preferred_element_type=F32)
                        bias = _moba_select_bias(gt, k_eff)
                    chains.append((kh_sc[h, 0:nk, :], qs_sc[h], vt_sc[h, :, 0:nk], s_sc.at[i],
                                   p_sc.at[i], bias))
                o = _tile_attention(chains, qq + 1, blk)
                rows = ATTN_CHAINS * hd
                ot_sc[pl.ds(pl.multiple_of(hg * rows, rows), rows), :] = jnp.concatenate(o, axis=0)
                return carry

            lax.fori_loop(0, MOBA_HEADS // ATTN_CHAINS, head_group, 0)

    o_ref[...] = ot_sc[...].T


def _moba_prompt(q, k, v, batch, seq):
    w = q.shape[1]
    blk = MOBA_BLOCK
    nb = seq // blk
    return pl.pallas_call(
        functools.partial(_moba_body, nb=nb),
        grid=(batch, nb),
        in_specs=[pl.BlockSpec((blk, w), lambda b, i: (b * nb + i, 0)),
                  pl.BlockSpec((seq, w), lambda b, i: (b, 0)),
                  pl.BlockSpec((seq, w), lambda b, i: (b, 0))],
        out_specs=pl.BlockSpec((blk, w), lambda b, i: (b * nb + i, 0)),
        out_shape=jax.ShapeDtypeStruct(q.shape, F32),
        scratch_shapes=[pltpu.VMEM((MOBA_HEADS, nb, HEAD_DIM), F32),
                        pltpu.VMEM((MOBA_HEADS, seq, HEAD_DIM), BF16),
                        pltpu.VMEM((MOBA_HEADS, HEAD_DIM, seq), BF16),
                        pltpu.VMEM((MOBA_HEADS, blk, HEAD_DIM), F32),
                        pltpu.VMEM((MOBA_HEADS, blk, HEAD_DIM), BF16),
                        pltpu.VMEM((ATTN_CHAINS, seq, blk), F32),
                        pltpu.VMEM((ATTN_CHAINS, seq, blk), BF16),
                        pltpu.VMEM((w, blk), F32)],
        compiler_params=_cparams(("arbitrary", "arbitrary")),
        name="moba_prompt",
    )(q, k, v)


def _diff_lambda(lam_ref, lam_init):
    lv = lam_ref[...]
    s1 = jnp.sum(lv[0:1, :] * lv[1:2, :], axis=-1, keepdims=True)
    s2 = jnp.sum(lv[2:3, :] * lv[3:4, :], axis=-1, keepdims=True)
    return jnp.exp(s1) - jnp.exp(s2) + lam_init


def _diff_body(q_ref, k_ref, v_ref, lam_ref, gsub_ref, o_ref, kh_sc, vt_sc, qs_sc, s_sc, p_sc, ot_sc,
               *, nb, tq, lam_init):
    qi = pl.program_id(1)
    hd = HEAD_DIM
    nrow = 2 * DIFF_HEADS

    @pl.when(qi == 0)
    def _prep():
        for n in range(nb):
            rows = slice(n * tq, (n + 1) * tq)
            kblk = k_ref[rows, :]
            vtb = v_ref[rows, :].T
            for r in range(nrow):
                kh_sc[r, rows, :] = kblk[:, r * hd:(r + 1) * hd].astype(BF16)
            for h in range(DIFF_HEADS):
                vt_sc[h, :, rows] = vtb[h * DIFF_DIM:(h + 1) * DIFF_DIM, :].astype(BF16)

    for r in range(nrow):
        qs_sc[r] = (q_ref[:, r * hd:(r + 1) * hd] * SCALE_LOG2E).astype(BF16)
    lam = _diff_lambda(lam_ref, lam_init)

    for qq in range(nb):
        @pl.when(qi == qq)
        def _tile(qq=qq):
            nk = (qq + 1) * tq

            hpi = ATTN_CHAINS // 2

            def head_group(hg, carry):
                chains = []
                for i in range(ATTN_CHAINS):
                    r = ATTN_CHAINS * hg + i
                    chains.append((kh_sc[r, 0:nk, :], qs_sc[r], vt_sc[hpi * hg + i // 2, :, 0:nk],
                                   s_sc.at[i], p_sc.at[i], None))
                o = _tile_attention(chains, qq + 1, tq)
                for j in range(hpi):
                    od = o[2 * j] - lam * o[2 * j + 1]
                    ms = jnp.mean(od * od, axis=0, keepdims=True)
                    od = od * lax.rsqrt(ms + EPS) * gsub_ref[...]
                    row0 = pl.multiple_of((hpi * hg + j) * DIFF_DIM, DIFF_DIM)
                    ot_sc[pl.ds(row0, DIFF_DIM), :] = od * (1.0 - lam_init)
                return carry

            lax.fori_loop(0, 2 * DIFF_HEADS // ATTN_CHAINS, head_group, 0)

    o_ref[...] = ot_sc[...].T


def _diff_prompt(q, k, v, lam4, gsub, batch, seq, lam_init, tq=256):
    w = q.shape[1]
    nb = seq // tq
    return pl.pallas_call(
        functools.partial(_diff_body, nb=nb, tq=tq, lam_init=lam_init),
        grid=(batch, nb),
        in_specs=[pl.BlockSpec((tq, w), lambda b, i: (b * nb + i, 0)),
                  pl.BlockSpec((seq, w), lambda b, i: (b, 0)),
                  pl.BlockSpec((seq, w), lambda b, i: (b, 0)),
                  pl.BlockSpec((4, HEAD_DIM), lambda b, i: (0, 0)),
                  pl.BlockSpec((DIFF_DIM, 1), lambda b, i: (0, 0))],
        out_specs=pl.BlockSpec((tq, w), lambda b, i: (b * nb + i, 0)),
        out_shape=jax.ShapeDtypeStruct(q.shape, F32),
        scratch_shapes=[pltpu.VMEM((2 * DIFF_HEADS, seq, HEAD_DIM), BF16),
                        pltpu.VMEM((DIFF_HEADS, DIFF_DIM, seq), BF16),
                        pltpu.VMEM((2 * DIFF_HEADS, tq, HEAD_DIM), BF16),
                        pltpu.VMEM((ATTN_CHAINS, seq, tq), F32),
                        pltpu.VMEM((ATTN_CHAINS, seq, tq), BF16),
                        pltpu.VMEM((w, tq), F32)],
        compiler_params=_cparams(("arbitrary", "arbitrary")),
        name="diff_prompt",
    )(q, k, v, lam4, gsub.reshape(DIFF_DIM, 1))


def _gelu_tanh(x):
    return 0.5 * x * (1.0 + jnp.tanh(math.sqrt(2.0 / math.pi) * (x + 0.044715 * (x * x * x))))


def _log_sigmoid(x):
    return jnp.minimum(x, 0.0) - jnp.log1p(jnp.exp(-jnp.abs(x)))


def _lru_gates(xc, wa_ref, ba_ref, wx_ref, bx_ref, lam_ref):
    bw = wa_ref.shape[1]
    xcb = xc.astype(BF16)
    ra, rx = [], []
    for n in range(RNN_BLOCKS):
        xn = xcb[:, n * bw:(n + 1) * bw]
        ra.append(jnp.dot(xn, wa_ref[n], preferred_element_type=F32))
        rx.append(jnp.dot(xn, wx_ref[n], preferred_element_type=F32))
    r = jax.nn.sigmoid(jnp.concatenate(ra, axis=1) + ba_ref[...])
    ig = jax.nn.sigmoid(jnp.concatenate(rx, axis=1) + bx_ref[...])
    log_a = LRU_C * r * _log_sigmoid(lam_ref[...])
    a = jnp.exp(log_a)
    b = jnp.sqrt(-jnp.tanh(log_a) * (1.0 + a * a)) * (ig * xc)
    return a, b


def _scan_rows(a, b_in, h0, nb, tc):
    sub = lax.broadcasted_iota(jnp.int32, (8, a.shape[1]), 0)
    rows, last = [], []
    for b in range(nb):
        h_in = h0[b:b + 1, :]
        for g in range(tc // 8):
            r0 = b * tc + 8 * g
            ga, gb = a[r0:r0 + 8, :], b_in[r0:r0 + 8, :]
            for sh in (1, 2, 4):
                keep = sub >= sh
                a_sh = jnp.where(keep, pltpu.roll(ga, sh, 0), 1.0)
                b_sh = jnp.where(keep, pltpu.roll(gb, sh, 0), 0.0)
                gb = ga * b_sh + gb
                ga = ga * a_sh
            h = ga * h_in + gb
            rows.append(h)
            h_in = h[7:8, :]
        last.append(h_in)
    return jnp.concatenate(rows, axis=0), jnp.concatenate(last, axis=0)


def _rglru_body(gate_ref, xb_ref, x_ref, cw_ref, cb_ref, wa_ref, ba_ref, wx_ref, bx_ref, lam_ref,
                wo_ref, g_ref, y_ref, buf_ref, hl_ref, xpad_sc, hc_sc):
    c = pl.program_id(0)
    nb, tc, d = xb_ref.shape
    hist = CONV_W - 1

    @pl.when(c == 0)
    def _():
        xpad_sc[:, 0:8, :] = jnp.zeros((nb, 8, d), F32)
        hc_sc[...] = jnp.zeros_like(hc_sc)

    @pl.when(c > 0)
    def _():
        xpad_sc[:, 0:8, :] = xpad_sc[:, tc:tc + 8, :]

    xpad_sc[:, 8:8 + tc, :] = xb_ref[...]
    xcs = []
    for b in range(nb):
        xc = cb_ref[...] + xpad_sc[b, 8 - hist:8 - hist + tc, :] * cw_ref[0:1, :]
        for i in range(1, CONV_W):
            xc = xc + xpad_sc[b, 8 - hist + i:8 - hist + i + tc, :] * cw_ref[i:i + 1, :]
        xcs.append(xc)
    a, b_in = _lru_gates(jnp.concatenate(xcs, axis=0), wa_ref, ba_ref, wx_ref, bx_ref, lam_ref)
    hall, h = _scan_rows(a, b_in, hc_sc[...], nb, tc)
    hc_sc[...] = h
    y = (hall * _gelu_tanh(gate_ref[...].reshape(nb * tc, d))).astype(BF16)
    y = jnp.dot(y, wo_ref[...], preferred_element_type=F32)
    y_ref[...] = (x_ref[...].reshape(nb * tc, d) + _rms(y, g_ref[...])).reshape(nb, tc, d)

    @pl.when(c == pl.num_programs(0) - 1)
    def _():
        buf_ref[...] = xpad_sc[:, 8 + tc - hist:8 + tc, :]
        hl_ref[...] = h


def _rglru_prompt(gate, xb, x, cw, cb, wa_bf, ba, wx_bf, bx, lam, wo_bf, g, batch, seq, tc=32):
    m, d = x.shape
    bw = d // RNN_BLOCKS
    chunk = pl.BlockSpec((batch, tc, d), lambda c: (0, c, 0))
    const2 = lambda c: (0, 0)
    const3 = lambda c: (0, 0, 0)
    vec = pl.BlockSpec((1, d), const2)
    r3 = lambda t: t.reshape(batch, seq, d)
    y, buf, hl = pl.pallas_call(
        _rglru_body,
        grid=(seq // tc,),
        in_specs=[chunk, chunk, chunk,
                  pl.BlockSpec((CONV_W, d), const2), vec,
                  pl.BlockSpec((RNN_BLOCKS, bw, bw), const3), vec,
                  pl.BlockSpec((RNN_BLOCKS, bw, bw), const3), vec, vec,
                  pl.BlockSpec((d, d), const2), vec],
        out_specs=[chunk,
                   pl.BlockSpec((batch, CONV_W - 1, d), const3),
                   pl.BlockSpec((batch, d), const2)],
        out_shape=[jax.ShapeDtypeStruct((batch, seq, d), F32),
                   jax.ShapeDtypeStruct((batch, CONV_W - 1, d), F32),
                   jax.ShapeDtypeStruct((batch, d), F32)],
        scratch_shapes=[pltpu.VMEM((batch, tc + 8, d), F32), pltpu.VMEM((batch, d), F32)],
        compiler_params=_cparams(("arbitrary",)),
        name="rglru_prompt",
    )(r3(gate), r3(xb), r3(x), cw, cb.reshape(1, d), wa_bf, ba.reshape(1, d), wx_bf, bx.reshape(1, d),
      lam.reshape(1, d), wo_bf, g.reshape(1, d))
    return y.reshape(m, d), buf, hl


def _rglru_step_body(gate_ref, xb_ref, x_ref, conv_ref, h0_ref, cw_ref, cb_ref, wa_ref, ba_ref,
                     wx_ref, bx_ref, lam_ref, wo_ref, g_ref, y_ref, h_ref):
    xc = cb_ref[...] + xb_ref[...] * cw_ref[CONV_W - 1:CONV_W, :]
    for i in range(CONV_W - 1):
        xc = xc + conv_ref[i] * cw_ref[i:i + 1, :]
    a, b = _lru_gates(xc, wa_ref, ba_ref, wx_ref, bx_ref, lam_ref)
    h = a * h0_ref[...] + b
    h_ref[...] = h
    y = (h * _gelu_tanh(gate_ref[...])).astype(BF16)
    y = jnp.dot(y, wo_ref[...], preferred_element_type=F32)
    y_ref[...] = x_ref[...] + _rms(y, g_ref[...])


def _rglru_step(gate, xb, x, conv_t, h0, cw, cb, wa_bf, ba, wx_bf, bx, lam, wo_bf, g):
    m, d = x.shape
    return pl.pallas_call(
        _rglru_step_body,
        out_shape=[jax.ShapeDtypeStruct((m, d), F32), jax.ShapeDtypeStruct((m, d), F32)],
        compiler_params=pltpu.CompilerParams(vmem_limit_bytes=VMEM_LIMIT),
        name="rglru_step",
    )(gate, xb, x, conv_t, h0, cw, cb.reshape(1, d), wa_bf, ba.reshape(1, d), wx_bf, bx.reshape(1, d),
      lam.reshape(1, d), wo_bf, g.reshape(1, d))


def _key_minor(cache):
    nd = cache.ndim
    t = jnp.transpose(cache, (0, 1) + tuple(range(3, nd)) + (2,))
    return t.reshape(t.shape[:2] + (-1,) + t.shape[-2:])


def _page_scores(kt_ref, qb):
    return jnp.sum(kt_ref[...] * qb, axis=1)


def _page_stream(pt_ref, streams, li, n_pages, pp, compute):
    b = pl.program_id(0)
    nslots = streams[0][1].shape[0]
    nsteps = n_pages // pp
    ahead = nslots - 1
    assert nsteps % nslots == 0 and nsteps >= ahead

    def copies(row, s, slot):
        base = row * n_pages + s * pp
        return [pltpu.make_async_copy(hbm.at[li, pt_ref[base + j]], buf.at[slot, j], sem.at[slot, j])
                for hbm, buf, sem in streams for j in range(pp)]

    def start(row, s, slot):
        for c in copies(row, s, slot):
            c.start()

    @pl.when(b == 0)
    def _():
        for s0 in range(ahead):
            start(0, s0, s0)

    def ring(i, carry):
        for slot in range(nslots):
            s = nslots * i + slot
            tgt, tgt_slot = s + ahead, (slot + ahead) % nslots

            @pl.when(tgt < nsteps)
            def _():
                start(b, tgt, tgt_slot)

            @pl.when((tgt >= nsteps) & (b + 1 < pl.num_programs(0)))
            def _():
                start(b + 1, tgt - nsteps, tgt_slot)

            for c in copies(b, s, slot):
                c.wait()
            compute(s, slot)
        return carry

    lax.fori_loop(0, nsteps // nslots, ring, 0)


def _moba_gate_body(pt_ref, q_ref, kt_hbm, sel_ref, kbuf, ksem, qb_sc, g_sc, *, li, n_pages, pp, ppb):
    page = kbuf.shape[-1]
    nblk = g_sc.shape[0]
    qb_sc[...] = jnp.broadcast_to(q_ref[...], qb_sc.shape)

    def compute(s, slot):
        qb = qb_sc[...]
        for j in range(pp // ppb):
            tot = _page_scores(kbuf.at[slot, ppb * j], qb)
            for t in range(1, ppb):
                tot = tot + _page_scores(kbuf.at[slot, ppb * j + t], qb)
            g = jnp.sum(tot, axis=-1, keepdims=True) * (1.0 / (ppb * page))
            g_sc[s * (pp // ppb) + j] = jnp.broadcast_to(g, g_sc.shape[1:])

    _page_stream(pt_ref, [(kt_hbm, kbuf, ksem)], li, n_pages, pp, compute)

    g = g_sc[...]
    bid = lax.broadcasted_iota(jnp.int32, g.shape, 0)
    lid = lax.broadcasted_iota(jnp.int32, sel_ref.shape, 1)
    out = jnp.zeros(sel_ref.shape, jnp.int32)
    for t in range(MOBA_TOPK):
        mx = jnp.max(g, axis=0, keepdims=True)
        idx = jnp.min(jnp.where(g == mx, bid, nblk), axis=0, keepdims=True)
        out = jnp.where(lid == t, idx[0], out)
        g = jnp.where(bid == idx, -jnp.inf, g)
    sel_ref[...] = out


def _moba_gate(q_col, kt, pt_flat, li, batch, n_pages, pp=16):
    _, _, nh, hd, page = kt.shape
    ppb = MOBA_BLOCK // page
    nblk = n_pages // ppb
    return pl.pallas_call(
        functools.partial(_moba_gate_body, li=li, n_pages=n_pages, pp=pp, ppb=ppb),
        grid_spec=pltpu.PrefetchScalarGridSpec(
            num_scalar_prefetch=1, grid=(batch,),
            in_specs=[pl.BlockSpec((None, nh, hd, 1), lambda b, pt: (b, 0, 0, 0)),
                      pl.BlockSpec(memory_space=pl.ANY)],
            out_specs=pl.BlockSpec((None, nh, LANES), lambda b, pt: (b, 0, 0)),
            scratch_shapes=[pltpu.VMEM((PAGE_SLOTS, pp, nh, hd, page), F32),
                            pltpu.SemaphoreType.DMA((PAGE_SLOTS, pp)),
                            pltpu.VMEM((nh, hd, page), F32),
                            pltpu.VMEM((nblk, nh, LANES), F32)]),
        out_shape=jax.ShapeDtypeStruct((batch, nh, LANES), jnp.int32),
        compiler_params=_cparams(("arbitrary",)),
        name="moba_gate",
    )(pt_flat, q_col, kt)


def _moba_step_body(sel_ref, pt_ref, q_ref, kn_ref, vn_ref, kt_hbm, vt_hbm, o_ref, kbuf, vbuf, ksem, vsem,
                    *, li, n_pages, ppb):
    b = pl.program_id(0)
    _, nh, nsl = kbuf.shape[:3]
    slot = b % 2

    def copies(row, slot):
        out = []
        for h in range(nh):
            for t in range(MOBA_TOPK):
                blk = sel_ref[(row * nh + h) * MOBA_TOPK + t]
                for j in range(ppb):
                    page = pt_ref[row * n_pages + blk * ppb + j]
                    i = t * ppb + j
                    out.append(pltpu.make_async_copy(kt_hbm.at[li, page, h], kbuf.at[slot, h, i],
                                                     ksem.at[slot, h, i]))
                    out.append(pltpu.make_async_copy(vt_hbm.at[li, page, h], vbuf.at[slot, h, i],
                                                     vsem.at[slot, h, i]))
        return out

    @pl.when(b == 0)
    def _():
        for c in copies(0, 0):
            c.start()

    @pl.when(b + 1 < pl.num_programs(0))
    def _():
        for c in copies(b + 1, 1 - slot):
            c.start()

    for c in copies(b, slot):
        c.wait()

    for h in range(nh):
        q8 = jnp.broadcast_to(q_ref[h] * SCALE, (8, q_ref.shape[-1]))
        kt = jnp.concatenate([kbuf[slot, h, i] for i in range(nsl)], axis=1).astype(BF16)
        vt = jnp.concatenate([vbuf[slot, h, i] for i in range(nsl)], axis=1).astype(BF16)
        s = jnp.dot(q8.astype(BF16), kt, preferred_element_type=F32)
        s_self = jnp.sum(q8 * kn_ref[h], axis=-1, keepdims=True)
        m = jnp.maximum(jnp.max(s, axis=-1, keepdims=True), s_self)
        p = jnp.exp(s - m)
        p_self = jnp.exp(s_self - m)
        l = jnp.sum(p, axis=-1, keepdims=True) + p_self
        pv = lax.dot_general(p.astype(BF16), vt, NT_DIMS, preferred_element_type=F32)
        o_ref[h] = (pv + p_self * vn_ref[h]) / l


def _moba_step(q4, kn4, vn4, kt, vt, sel_flat, pt_flat, li, batch, n_pages):
    _, _, nh, hd, page = kt.shape
    ppb = MOBA_BLOCK // page
    nsl = MOBA_TOPK * ppb
    row = pl.BlockSpec((None, nh, 1, hd), lambda b, sel, pt: (b, 0, 0, 0))
    hbm = pl.BlockSpec(memory_space=pl.ANY)
    return pl.pallas_call(
        functools.partial(_moba_step_body, li=li, n_pages=n_pages, ppb=ppb),
        grid_spec=pltpu.PrefetchScalarGridSpec(
            num_scalar_prefetch=2, grid=(batch,),
            in_specs=[row, row, row, hbm, hbm],
            out_specs=pl.BlockSpec((None, nh, 8, hd), lambda b, sel, pt: (b, 0, 0, 0)),
            scratch_shapes=[pltpu.VMEM((2, nh, nsl, hd, page), F32), pltpu.VMEM((2, nh, nsl, hd, page), F32),
                            pltpu.SemaphoreType.DMA((2, nh, nsl)), pltpu.SemaphoreType.DMA((2, nh, nsl))]),
        out_shape=jax.ShapeDtypeStruct((batch, nh, 8, hd), F32),
        compiler_params=_cparams(("arbitrary",)),
        name="moba_step",
    )(sel_flat, pt_flat, q4, kn4, vn4, kt, vt)


def _diff_step_body(pt_ref, q_ref, kn_ref, vn_ref, lam_ref, gsub_ref, kt_hbm, v_hbm, o_ref,
                    kbuf, vbuf, ksem, vsem, qb_sc, m_sc, l_sc, acc_sc, *, li, n_pages, pp, lam_init):
    page = kbuf.shape[-1]
    qb_sc[...] = jnp.broadcast_to(q_ref[...] * SCALE, qb_sc.shape)
    m_sc[...] = jnp.full(m_sc.shape, NEG, F32)
    l_sc[...] = jnp.zeros_like(l_sc)
    acc_sc[...] = jnp.zeros_like(acc_sc)

    def compute(s, slot):
        del s
        qb = qb_sc[...]
        sc = jnp.concatenate([_page_scores(kbuf.at[slot, j], qb) for j in range(pp)], axis=1)
        m_old = m_sc[...]
        m_new = jnp.maximum(m_old, jnp.max(sc, axis=-1, keepdims=True))
        alpha = jnp.exp(m_old - m_new)
        p = jnp.exp(sc - m_new)
        l_sc[...] = alpha * l_sc[...] + jnp.sum(p, axis=-1, keepdims=True)
        m_sc[...] = m_new
        pb = p.astype(BF16)
        for h in range(DIFF_HEADS):
            vh = jnp.concatenate([vbuf[slot, j, pl.ds(h, page, stride=DIFF_HEADS), :] for j in range(pp)],
                                 axis=0)
            acc_sc[h] = alpha * acc_sc[h] + jnp.dot(pb, vh.astype(BF16), preferred_element_type=F32)

    _page_stream(pt_ref, [(kt_hbm, kbuf, ksem), (v_hbm, vbuf, vsem)], li, n_pages, pp, compute)

    s_self = jnp.sum(q_ref[...] * SCALE * kn_ref[...], axis=1)
    m_old = m_sc[...]
    m_new = jnp.maximum(m_old, s_self)
    alpha = jnp.exp(m_old - m_new)
    p_self = jnp.exp(s_self - m_new)
    l = alpha * l_sc[...] + p_self
    lam = _diff_lambda(lam_ref, lam_init)
    for h in range(DIFF_HEADS):
        vn = vn_ref[:, h * DIFF_DIM:(h + 1) * DIFF_DIM]
        o = (alpha * acc_sc[h] + p_self * vn) / l
        od = o[2 * h:2 * h + 1, :] - lam * o[2 * h + 1:2 * h + 2, :]
        o_ref[:, h * DIFF_DIM:(h + 1) * DIFF_DIM] = _rms(od, gsub_ref[...]) * (1.0 - lam_init)


def _diff_step(q_col, kn_col, vn3, lam4, gsub, kt, v2, pt_flat, li, batch, n_pages, lam_init, pp=8):
    _, _, nrow, hd, page = kt.shape
    vrows, dv = v2.shape[2:]
    w = vn3.shape[-1]
    col = pl.BlockSpec((None, nrow, hd, 1), lambda b, pt: (b, 0, 0, 0))
    row = pl.BlockSpec((None, 1, w), lambda b, pt: (b, 0, 0))
    hbm = pl.BlockSpec(memory_space=pl.ANY)
    return pl.pallas_call(
        functools.partial(_diff_step_body, li=li, n_pages=n_pages, pp=pp, lam_init=lam_init),
        grid_spec=pltpu.PrefetchScalarGridSpec(
            num_scalar_prefetch=1, grid=(batch,),
            in_specs=[col, col, row,
                      pl.BlockSpec((4, HEAD_DIM), lambda b, pt: (0, 0)),
                      pl.BlockSpec((1, DIFF_DIM), lambda b, pt: (0, 0)), hbm, hbm],
            out_specs=row,
            scratch_shapes=[pltpu.VMEM((PAGE_SLOTS, pp, nrow, hd, page), F32),
                            pltpu.VMEM((PAGE_SLOTS, pp, vrows, dv), F32),
                            pltpu.SemaphoreType.DMA((PAGE_SLOTS, pp)), pltpu.SemaphoreType.DMA((PAGE_SLOTS, pp)),
                            pltpu.VMEM((nrow, hd, page), F32), pltpu.VMEM((nrow, 1), F32),
                            pltpu.VMEM((nrow, 1), F32), pltpu.VMEM((DIFF_HEADS, nrow, dv), F32)]),
        out_shape=jax.ShapeDtypeStruct((batch, 1, w), F32),
        compiler_params=_cparams(("arbitrary",)),
        name="diff_step",
    )(pt_flat, q_col, kn_col, vn3, lam4, gsub.reshape(1, DIFF_DIM), kt, v2)


def kernel(x_prompt, x_sample, cache_moba_k, cache_moba_v, cache_diff_k, cache_diff_v, state_conv, state_rnn, page_table, attn_g_pre, attn_w_in, diff_lambda_q1, diff_lambda_k1, diff_lambda_q2, diff_lambda_k2, diff_g_sub, attn_w_out, attn_g_post, rnn_g_pre, rnn_w_in, rnn_conv_w, rnn_conv_b, rnn_w_a, rnn_b_a, rnn_w_x, rnn_b_x, rnn_lambda, rnn_w_out, rnn_g_post, mlp_g_pre, mlp_w_up, mlp_w_down, mlp_g_post):
    batch, seq, d = x_prompt.shape
    dec_batch, dec_seq, _ = x_sample.shape
    assert dec_seq == 1
    depth = mlp_w_up.shape[0]
    na, n_pool, page = cache_moba_k.shape[:3]
    n_pages = page_table.shape[1]
    past_len = n_pages * page
    mw = MOBA_HEADS * HEAD_DIM
    dw = DIFF_HEADS * DIFF_DIM

    xp = x_prompt.reshape(batch * seq, d)
    xs = x_sample.reshape(dec_batch, d)
    tab_p = _rope_tables(jnp.arange(seq, dtype=jnp.int32))
    tab_s = _rope_tables(jnp.full((dec_batch,), past_len, jnp.int32))
    pt_flat = page_table.reshape(-1)
    cmk_t = _key_minor(cache_moba_k)
    cmv_t = _key_minor(cache_moba_v)
    cdk_t = _key_minor(cache_diff_k)
    cdv2 = cache_diff_v.reshape(na, n_pool, page * DIFF_HEADS, DIFF_DIM)

    tm_p = 512
    attn_rope = (True, True, False, True, True, False)
    outs = {k: [] for k in ("pmk", "pmv", "pdk", "pdv", "pconv", "prnn",
                            "smk", "smv", "sdk", "sdv", "sconv", "srnn")}
    for layer in range(depth):
        li = layer // 2
        if layer % 2 == 0:
            lam_init = 0.8 - 0.6 * math.exp(-0.3 * layer)
            w_in = attn_w_in[li].astype(BF16)
            w_out = attn_w_out[li].astype(BF16)
            lam4 = jnp.stack([diff_lambda_q1[li], diff_lambda_k1[li], diff_lambda_q2[li], diff_lambda_k2[li]])
            mq, mk, mv, dq, dk, dv, mk_t, mv_t, dk_t = _in_proj(
                xp, attn_g_pre[li], w_in, tab_p, attn_rope, tm_p,
                t_flags=(False, True, True, False, True, False))
            o_m = _moba_prompt(mq, mk, mv, batch, seq)
            o_d = _diff_prompt(dq, dk, dv, lam4, diff_g_sub[li], batch, seq, lam_init)
            xp = _attn_out(o_m, o_d, w_out, attn_g_post[li], xp, tm_p)
            outs["pmk"].append(jnp.moveaxis(mk_t.reshape(batch, MOBA_HEADS, HEAD_DIM, seq), -1, 1))
            outs["pmv"].append(jnp.moveaxis(mv_t.reshape(batch, MOBA_HEADS, HEAD_DIM, seq), -1, 1))
            outs["pdk"].append(jnp.moveaxis(dk_t.reshape(batch, DIFF_HEADS, 2, HEAD_DIM, seq), -1, 1))
            outs["pdv"].append(dv.reshape(batch, seq, DIFF_HEADS, DIFF_DIM))
            mq, mk, mv, dq, dk, dv = _in_proj(xs, attn_g_pre[li], w_in, tab_s, attn_rope, dec_batch)
            col = lambda t: t.reshape(dec_batch, -1, HEAD_DIM, 1)
            hrow = lambda t: t.reshape(dec_batch, MOBA_HEADS, 1, HEAD_DIM)
            sel = _moba_gate(col(mq), cmk_t, pt_flat, li, dec_batch, n_pages)
            sel = sel[:, :, :MOBA_TOPK].reshape(-1)
            o_m = _moba_step(hrow(mq), hrow(mk), hrow(mv), cmk_t, cmv_t, sel, pt_flat, li,
                             dec_batch, n_pages)[:, :, 0, :]
            o_d = _diff_step(col(dq), col(dk), dv.reshape(dec_batch, 1, dw), lam4, diff_g_sub[li],
                             cdk_t, cdv2, pt_flat, li, dec_batch, n_pages, lam_init)
            xs = _attn_out(o_m.reshape(dec_batch, mw), o_d.reshape(dec_batch, dw), w_out,
                           attn_g_post[li], xs, dec_batch)
            outs["smk"].append(mk.reshape(dec_batch, 1, MOBA_HEADS, HEAD_DIM))
            outs["smv"].append(mv.reshape(dec_batch, 1, MOBA_HEADS, HEAD_DIM))
            outs["sdk"].append(dk.reshape(dec_batch, 1, DIFF_HEADS, 2, HEAD_DIM))
            outs["sdv"].append(dv.reshape(dec_batch, 1, DIFF_HEADS, DIFF_DIM))
        else:
            w_in = rnn_w_in[li].astype(BF16)
            wts = (rnn_conv_w[li], rnn_conv_b[li], rnn_w_a[li].astype(BF16), rnn_b_a[li].reshape(-1),
                   rnn_w_x[li].astype(BF16), rnn_b_x[li].reshape(-1), rnn_lambda[li],
                   rnn_w_out[li].astype(BF16), rnn_g_post[li])
            gate, xb = _in_proj(xp, rnn_g_pre[li], w_in, tab_p, (False, False), tm_p)
            xp, cbuf, hlast = _rglru_prompt(gate, xb, xp, *wts, batch, seq)
            outs["pconv"].append(cbuf)
            outs["prnn"].append(hlast.reshape(batch, d))
            gate, xb = _in_proj(xs, rnn_g_pre[li], w_in, tab_s, (False, False), dec_batch)
            conv_t = jnp.swapaxes(state_conv[li], 0, 1)
            xs, hnew = _rglru_step(gate, xb, xs, conv_t, state_rnn[li], *wts)
            outs["sconv"].append(jnp.concatenate([state_conv[li][:, 1:], xb[:, None, :]], axis=1))
            outs["srnn"].append(hnew)
        wu = mlp_w_up[layer].astype(BF16)
        wd = mlp_w_down[layer].astype(BF16)
        xp = _mlp(xp, mlp_g_pre[layer], wu, wd, mlp_g_post[layer], 1024, 1024)
        xs = _mlp(xs, mlp_g_pre[layer], wu, wd, mlp_g_post[layer], dec_batch, 512)

    st = lambda k: jnp.stack(outs[k])
    return (xp.reshape(batch, seq, d), xs.reshape(dec_batch, 1, d),
            st("pmk"), st("pmv"), st("pdk"), st("pdv"), st("pconv"), st("prnn"),
            st("smk"), st("smv"), st("sdk"), st("sdv"), st("sconv"), st("srnn"))
```

```python
import functools
import math

import jax
import jax.numpy as jnp
from jax import lax
from jax.experimental import pallas as pl
from jax.experimental.pallas import tpu as pltpu

F32 = jnp.float32
BF16 = jnp.bfloat16
HIGHEST = lax.Precision.HIGHEST

HEAD_DIM = 64
ROT_DIM = HEAD_DIM // 4
ROPE_THETA = 500000.0
MOBA_HEADS = 8
MOBA_BLOCK = 256
MOBA_TOPK = 3
DIFF_HEADS = 4
DIFF_DIM = 2 * HEAD_DIM
RNN_BLOCKS = 4
CONV_W = 4
LRU_C = 8.0
EPS = 1e-6
LANES = 128
NEG = -1e30
SCALE = HEAD_DIM ** -0.5
SCALE_LOG2E = SCALE * math.log2(math.e)
NT_DIMS = (((1,), (1,)), ((), ()))
ATTN_CHAINS = 4
PAGE_SLOTS = 4
VMEM_LIMIT = 52 * 1024 * 1024


def _cparams(sem):
    return pltpu.CompilerParams(dimension_semantics=sem, vmem_limit_bytes=VMEM_LIMIT)


def _rms(x, g):
    ms = jnp.mean(x * x, axis=-1, keepdims=True)
    return x * lax.rsqrt(ms + EPS) * g


def _rope_tables(pos):
    half = ROT_DIM // 2
    inv = jnp.exp(-math.log(ROPE_THETA) * jnp.arange(half, dtype=F32) * (2.0 / ROT_DIM))
    ang = pos.astype(F32)[:, None] * inv[None, :]
    cos, sin = jnp.cos(ang), jnp.sin(ang)
    n = pos.shape[0]
    pad = jnp.zeros((n, HEAD_DIM - ROT_DIM), F32)
    c64 = jnp.concatenate([cos, cos, pad + 1.0], axis=1)
    sa64 = jnp.concatenate([-sin, jnp.zeros((n, half), F32), pad], axis=1)
    sb64 = jnp.concatenate([jnp.zeros((n, half), F32), sin, pad], axis=1)
    rep = LANES // HEAD_DIM
    return tuple(jnp.tile(t, (1, rep)) for t in (c64, sa64, sb64))


def _in_proj_body(x_ref, g_ref, w_ref, c_ref, sa_ref, sb_ref, *outs, rope_flags, t_flags, tn):
    n_out = len(rope_flags)
    outs_t = iter(outs[n_out:])
    xn = _rms(x_ref[...], g_ref[...]).astype(BF16)
    for j, out_ref in enumerate(outs[:n_out]):
        u = jnp.dot(xn, w_ref[:, j * tn:(j + 1) * tn], preferred_element_type=F32)
        if rope_flags[j]:
            c, sa, sb = c_ref[...], sa_ref[...], sb_ref[...]
            u = jnp.concatenate(
                [uk * c + pltpu.roll(uk, LANES - ROT_DIM // 2, 1) * sa + pltpu.roll(uk, ROT_DIM // 2, 1) * sb
                 for uk in (u[:, k * LANES:(k + 1) * LANES] for k in range(tn // LANES))], axis=1)
        out_ref[...] = u
        if t_flags[j]:
            next(outs_t)[...] = u.T


def _in_proj(x, g, w_bf, tables, rope_flags, tm, t_flags=None):
    m, d = x.shape
    n_out = len(rope_flags)
    t_flags = t_flags or (False,) * n_out
    n = w_bf.shape[1]
    tn = n // n_out
    seq = tables[0].shape[0]
    npos = seq // tm
    tab_spec = pl.BlockSpec((tm, LANES), lambda i: (i % npos, 0))
    n_t = sum(t_flags)
    return pl.pallas_call(
        functools.partial(_in_proj_body, rope_flags=rope_flags, t_flags=t_flags, tn=tn),
        grid=(m // tm,),
        in_specs=[pl.BlockSpec((tm, d), lambda i: (i, 0)),
                  pl.BlockSpec((1, d), lambda i: (0, 0)),
                  pl.BlockSpec((d, n), lambda i: (0, 0)),
                  tab_spec, tab_spec, tab_spec],
        out_specs=[pl.BlockSpec((tm, tn), lambda i: (i, 0))] * n_out
        + [pl.BlockSpec((None, tn, tm), lambda i: (i // npos, 0, i % npos))] * n_t,
        out_shape=[jax.ShapeDtypeStruct((m, tn), F32)] * n_out
        + [jax.ShapeDtypeStruct((m // seq, tn, seq), F32)] * n_t,
        compiler_params=_cparams(("arbitrary",)),
        name="in_proj",
    )(x, g.reshape(1, d), w_bf, *tables)


def _attn_out_body(om_ref, od_ref, w_ref, g_ref, x_ref, y_ref):
    km = om_ref.shape[1]
    o = jnp.dot(om_ref[...].astype(BF16), w_ref[:km, :], preferred_element_type=F32)
    o = o + jnp.dot(od_ref[...].astype(BF16), w_ref[km:, :], preferred_element_type=F32)
    y_ref[...] = x_ref[...] + _rms(o, g_ref[...])


def _attn_out(om, od, w_bf, g, x, tm):
    m, d = x.shape
    km, kd = om.shape[1], od.shape[1]
    return pl.pallas_call(
        _attn_out_body,
        grid=(m // tm,),
        in_specs=[pl.BlockSpec((tm, km), lambda i: (i, 0)),
                  pl.BlockSpec((tm, kd), lambda i: (i, 0)),
                  pl.BlockSpec((km + kd, d), lambda i: (0, 0)),
                  pl.BlockSpec((1, d), lambda i: (0, 0)),
                  pl.BlockSpec((tm, d), lambda i: (i, 0))],
        out_specs=pl.BlockSpec((tm, d), lambda i: (i, 0)),
        out_shape=jax.ShapeDtypeStruct((m, d), F32),
        compiler_params=_cparams(("arbitrary",)),
        name="attn_out",
    )(om, od, w_bf, g.reshape(1, d), x)


def _mlp_body(x_ref, g1_ref, wu_ref, wd_ref, g2_ref, y_ref, xn_sc, acc_sc):
    f = pl.program_id(1)

    @pl.when(f == 0)
    def _():
        xn_sc[...] = _rms(x_ref[...], g1_ref[...]).astype(BF16)
        acc_sc[...] = jnp.zeros_like(acc_sc)

    h = jnp.dot(xn_sc[...], wu_ref[...], preferred_element_type=F32)
    h = jnp.square(jnp.maximum(h, 0.0))
    acc_sc[...] += jnp.dot(h.astype(BF16), wd_ref[...], preferred_element_type=F32)

    @pl.when(f == pl.num_programs(1) - 1)
    def _():
        y_ref[...] = x_ref[...] + _rms(acc_sc[...], g2_ref[...])


def _mlp(x, g1, wu_bf, wd_bf, g2, tm, tf):
    m, d = x.shape
    ff = wu_bf.shape[1]
    return pl.pallas_call(
        _mlp_body,
        grid=(m // tm, ff // tf),
        in_specs=[pl.BlockSpec((tm, d), lambda i, f: (i, 0)),
                  pl.BlockSpec((1, d), lambda i, f: (0, 0)),
                  pl.BlockSpec((d, tf), lambda i, f: (0, f)),
                  pl.BlockSpec((tf, d), lambda i, f: (f, 0)),
                  pl.BlockSpec((1, d), lambda i, f: (0, 0))],
        out_specs=pl.BlockSpec((tm, d), lambda i, f: (i, 0)),
        out_shape=jax.ShapeDtypeStruct((m, d), F32),
        scratch_shapes=[pltpu.VMEM((tm, d), BF16), pltpu.VMEM((tm, d), F32)],
        compiler_params=_cparams(("arbitrary", "arbitrary")),
        name="mlp",
    )(x, g1.reshape(1, d), wu_bf, wd_bf, g2.reshape(1, d))


def _causal_mask(st):
    kidx = lax.broadcasted_iota(jnp.int32, st.shape, 0)
    qidx = lax.broadcasted_iota(jnp.int32, st.shape, 1)
    return jnp.where(kidx <= qidx, st, NEG)


def _tile_attention(chains, nblk, blk):
    nk = nblk * blk
    for kh, qs, _, s_ref, _, _ in chains:
        s_ref[0:nk, :] = lax.dot_general(kh, qs, NT_DIMS, preferred_element_type=F32)
    ms = []
    for _, _, _, s_ref, _, bias in chains:
        m = None
        for n in range(nblk):
            rows = slice(n * blk, (n + 1) * blk)
            piece = s_ref[rows, :]
            if n == nblk - 1:
                piece = _causal_mask(piece)
                s_ref[rows, :] = piece
            elif bias is not None:
                piece = piece + bias[n:n + 1, :]
                s_ref[rows, :] = piece
            pm = jnp.max(piece, axis=0, keepdims=True)
            m = pm if m is None else jnp.maximum(m, pm)
        ms.append(m)
    ls = []
    for (_, _, _, s_ref, p_ref, _), m in zip(chains, ms):
        l = None
        for n in range(nblk):
            rows = slice(n * blk, (n + 1) * blk)
            p = jnp.exp2(s_ref[rows, :] - m)
            ps = jnp.sum(p, axis=0, keepdims=True)
            l = ps if l is None else l + ps
            p_ref[rows, :] = p.astype(BF16)
        ls.append(l)
    accs = [jnp.dot(vt, p_ref[0:nk, :], preferred_element_type=F32) for _, _, vt, _, p_ref, _ in chains]
    return [acc / l for acc, l in zip(accs, ls)]


def _moba_select_bias(gt, k_eff):
    npast = gt.shape[0]
    rid = lax.broadcasted_iota(jnp.int32, gt.shape, 0)
    rank = jnp.zeros(gt.shape, jnp.int32)
    for mm in range(npast):
        gm = gt[mm:mm + 1, :]
        beats = (gm > gt) | ((gm == gt) & (rid > mm))
        rank = rank + beats.astype(jnp.int32)
    return jnp.where(rank < k_eff, 0.0, NEG)


def _moba_body(q_ref, k_ref, v_ref, o_ref, kmean_sc, kh_sc, vt_sc, qf_sc, qs_sc, s_sc, p_sc, ot_sc, *, nb):
    qi = pl.program_id(1)
    blk = MOBA_BLOCK
    hd = HEAD_DIM
    k_eff = min(MOBA_TOPK, nb)

    @pl.when(qi == 0)
    def _prep():
        for n in range(nb):
            rows = slice(n * blk, (n + 1) * blk)
            kblk = k_ref[rows, :]
            kmean = jnp.sum(kblk, axis=0, keepdims=True) * (1.0 / blk)
            vtb = v_ref[rows, :].T
            for h in range(MOBA_HEADS):
                hs = slice(h * hd, (h + 1) * hd)
                kmean_sc[h, n:n + 1, :] = kmean[:, hs]
                kh_sc[h, rows, :] = kblk[:, hs].astype(BF16)
                vt_sc[h, :, rows] = vtb[hs, :].astype(BF16)

    for h in range(MOBA_HEADS):
        qh = q_ref[:, h * hd:(h + 1) * hd]
        qf_sc[h] = qh
        qs_sc[h] = (qh * SCALE_LOG2E).astype(BF16)

    for qq in range(nb):
        @pl.when(qi == qq)
        def _tile(qq=qq):
            nk = (qq + 1) * blk

            def head_group(hg, carry):
                chains = []
                for i in range(ATTN_CHAINS):
                    h = ATTN_CHAINS * hg + i
                    bias = None
                    if qq > k_eff:
                        gt = lax.dot_general(kmean_sc[h, 0:qq, :], qf_sc[h], NT_DIMS, precision=HIGHEST,
                                             preferred_element_type=F32)
                        bias = _moba_select_bias(gt, k_eff)
                    chains.append((kh_sc[h, 0:nk, :], qs_sc[h], vt_sc[h, :, 0:nk], s_sc.at[i],
                                   p_sc.at[i], bias))
                o = _tile_attention(chains, qq + 1, blk)
                rows = ATTN_CHAINS * hd
                ot_sc[pl.ds(pl.multiple_of(hg * rows, rows), rows), :] = jnp.concatenate(o, axis=0)
                return carry

            lax.fori_loop(0, MOBA_HEADS // ATTN_CHAINS, head_group, 0)

    o_ref[...] = ot_sc[...].T


def _moba_prompt(q, k, v, batch, seq):
    w = q.shape[1]
    blk = MOBA_BLOCK
    nb = seq // blk
    return pl.pallas_call(
        functools.partial(_moba_body, nb=nb),
        grid=(batch, nb),
        in_specs=[pl.BlockSpec((blk, w), lambda b, i: (b * nb + i, 0)),
                  pl.BlockSpec((seq, w), lambda b, i: (b, 0)),
                  pl.BlockSpec((seq, w), lambda b, i: (b, 0))],
        out_specs=pl.BlockSpec((blk, w), lambda b, i: (b * nb + i, 0)),
        out_shape=jax.ShapeDtypeStruct(q.shape, F32),
        scratch_shapes=[pltpu.VMEM((MOBA_HEADS, nb, HEAD_DIM), F32),
                        pltpu.VMEM((MOBA_HEADS, seq, HEAD_DIM), BF16),
                        pltpu.VMEM((MOBA_HEADS, HEAD_DIM, seq), BF16),
                        pltpu.VMEM((MOBA_HEADS, blk, HEAD_DIM), F32),
                        pltpu.VMEM((MOBA_HEADS, blk, HEAD_DIM), BF16),
                        pltpu.VMEM((ATTN_CHAINS, seq, blk), F32),
                        pltpu.VMEM((ATTN_CHAINS, seq, blk), BF16),
                        pltpu.VMEM((w, blk), F32)],
        compiler_params=_cparams(("arbitrary", "arbitrary")),
        name="moba_prompt",
    )(q, k, v)


def _diff_lambda(lam_ref, lam_init):
    lv = lam_ref[...]
    s1 = jnp.sum(lv[0:1, :] * lv[1:2, :], axis=-1, keepdims=True)
    s2 = jnp.sum(lv[2:3, :] * lv[3:4, :], axis=-1, keepdims=True)
    return jnp.exp(s1) - jnp.exp(s2) + lam_init


def _diff_body(q_ref, k_ref, v_ref, lam_ref, gsub_ref, o_ref, kh_sc, vt_sc, qs_sc, s_sc, p_sc, ot_sc,
               *, nb, tq, lam_init):
    qi = pl.program_id(1)
    hd = HEAD_DIM
    nrow = 2 * DIFF_HEADS

    @pl.when(qi == 0)
    def _prep():
        for n in range(nb):
            rows = slice(n * tq, (n + 1) * tq)
            kblk = k_ref[rows, :]
            vtb = v_ref[rows, :].T
            for r in range(nrow):
                kh_sc[r, rows, :] = kblk[:, r * hd:(r + 1) * hd].astype(BF16)
            for h in range(DIFF_HEADS):
                vt_sc[h, :, rows] = vtb[h * DIFF_DIM:(h + 1) * DIFF_DIM, :].astype(BF16)

    for r in range(nrow):
        qs_sc[r] = (q_ref[:, r * hd:(r + 1) * hd] * SCALE_LOG2E).astype(BF16)
    lam = _diff_lambda(lam_ref, lam_init)

    for qq in range(nb):
        @pl.when(qi == qq)
        def _tile(qq=qq):
            nk = (qq + 1) * tq

            hpi = ATTN_CHAINS // 2

            def head_group(hg, carry):
                chains = []
                for i in range(ATTN_CHAINS):
                    r = ATTN_CHAINS * hg + i
                    chains.append((kh_sc[r, 0:nk, :], qs_sc[r], vt_sc[hpi * hg + i // 2, :, 0:nk],
                                   s_sc.at[i], p_sc.at[i], None))
                o = _tile_attention(chains, qq + 1, tq)
                for j in range(hpi):
                    od = o[2 * j] - lam * o[2 * j + 1]
                    ms = jnp.mean(od * od, axis=0, keepdims=True)
                    od = od * lax.rsqrt(ms + EPS) * gsub_ref[...]
                    row0 = pl.multiple_of((hpi * hg + j) * DIFF_DIM, DIFF_DIM)
                    ot_sc[pl.ds(row0, DIFF_DIM), :] = od * (1.0 - lam_init)
                return carry

            lax.fori_loop(0, 2 * DIFF_HEADS // ATTN_CHAINS, head_group, 0)

    o_ref[...] = ot_sc[...].T


def _diff_prompt(q, k, v, lam4, gsub, batch, seq, lam_init, tq=256):
    w = q.shape[1]
    nb = seq // tq
    return pl.pallas_call(
        functools.partial(_diff_body, nb=nb, tq=tq, lam_init=lam_init),
        grid=(batch, nb),
        in_specs=[pl.BlockSpec((tq, w), lambda b, i: (b * nb + i, 0)),
                  pl.BlockSpec((seq, w), lambda b, i: (b, 0)),
                  pl.BlockSpec((seq, w), lambda b, i: (b, 0)),
                  pl.BlockSpec((4, HEAD_DIM), lambda b, i: (0, 0)),
                  pl.BlockSpec((DIFF_DIM, 1), lambda b, i: (0, 0))],
        out_specs=pl.BlockSpec((tq, w), lambda b, i: (b * nb + i, 0)),
        out_shape=jax.ShapeDtypeStruct(q.shape, F32),
        scratch_shapes=[pltpu.VMEM((2 * DIFF_HEADS, seq, HEAD_DIM), BF16),
                        pltpu.VMEM((DIFF_HEADS, DIFF_DIM, seq), BF16),
                        pltpu.VMEM((2 * DIFF_HEADS, tq, HEAD_DIM), BF16),
                        pltpu.VMEM((ATTN_CHAINS, seq, tq), F32),
                        pltpu.VMEM((ATTN_CHAINS, seq, tq), BF16),
                        pltpu.VMEM((w, tq), F32)],
        compiler_params=_cparams(("arbitrary", "arbitrary")),
        name="diff_prompt",
    )(q, k, v, lam4, gsub.reshape(DIFF_DIM, 1))


def _gelu_tanh(x):
    return 0.5 * x * (1.0 + jnp.tanh(math.sqrt(2.0 / math.pi) * (x + 0.044715 * (x * x * x))))


def _log_sigmoid(x):
    return jnp.minimum(x, 0.0) - jnp.log1p(jnp.exp(-jnp.abs(x)))


def _lru_gates(xc, wa_ref, ba_ref, wx_ref, bx_ref, lam_ref):
    bw = wa_ref.shape[1]
    xcb = xc.astype(BF16)
    ra, rx = [], []
    for n in range(RNN_BLOCKS):
        xn = xcb[:, n * bw:(n + 1) * bw]
        ra.append(jnp.dot(xn, wa_ref[n], preferred_element_type=F32))
        rx.append(jnp.dot(xn, wx_ref[n], preferred_element_type=F32))
    r = jax.nn.sigmoid(jnp.concatenate(ra, axis=1) + ba_ref[...])
    ig = jax.nn.sigmoid(jnp.concatenate(rx, axis=1) + bx_ref[...])
    log_a = LRU_C * r * _log_sigmoid(lam_ref[...])
    a = jnp.exp(log_a)
    b = jnp.sqrt(-jnp.tanh(log_a) * (1.0 + a * a)) * (ig * xc)
    return a, b


def _scan_rows(a, b_in, h0, nb, tc):
    sub = lax.broadcasted_iota(jnp.int32, (8, a.shape[1]), 0)
    rows, last = [], []
    for b in range(nb):
        h_in = h0[b:b + 1, :]
        for g in range(tc // 8):
            r0 = b * tc + 8 * g
            ga, gb = a[r0:r0 + 8, :], b_in[r0:r0 + 8, :]
            for sh in (1, 2, 4):
                keep = sub >= sh
                a_sh = jnp.where(keep, pltpu.roll(ga, sh, 0), 1.0)
                b_sh = jnp.where(keep, pltpu.roll(gb, sh, 0), 0.0)
                gb = ga * b_sh + gb
                ga = ga * a_sh
            h = ga * h_in + gb
            rows.append(h)
            h_in = h[7:8, :]
        last.append(h_in)
    return jnp.concatenate(rows, axis=0), jnp.concatenate(last, axis=0)


def _rglru_body(gate_ref, xb_ref, x_ref, cw_ref, cb_ref, wa_ref, ba_ref, wx_ref, bx_ref, lam_ref,
                wo_ref, g_ref, y_ref, buf_ref, hl_ref, xpad_sc, hc_sc):
    c = pl.program_id(0)
    nb, tc, d = xb_ref.shape
    hist = CONV_W - 1

    @pl.when(c == 0)
    def _():
        xpad_sc[:, 0:8, :] = jnp.zeros((nb, 8, d), F32)
        hc_sc[...] = jnp.zeros_like(hc_sc)

    @pl.when(c > 0)
    def _():
        xpad_sc[:, 0:8, :] = xpad_sc[:, tc:tc + 8, :]

    xpad_sc[:, 8:8 + tc, :] = xb_ref[...]
    xcs = []
    for b in range(nb):
        xc = cb_ref[...] + xpad_sc[b, 8 - hist:8 - hist + tc, :] * cw_ref[0:1, :]
        for i in range(1, CONV_W):
            xc = xc + xpad_sc[b, 8 - hist + i:8 - hist + i + tc, :] * cw_ref[i:i + 1, :]
        xcs.append(xc)
    a, b_in = _lru_gates(jnp.concatenate(xcs, axis=0), wa_ref, ba_ref, wx_ref, bx_ref, lam_ref)
    hall, h = _scan_rows(a, b_in, hc_sc[...], nb, tc)
    hc_sc[...] = h
    y = (hall * _gelu_tanh(gate_ref[...].reshape(nb * tc, d))).astype(BF16)
    y = jnp.dot(y, wo_ref[...], preferred_element_type=F32)
    y_ref[...] = (x_ref[...].reshape(nb * tc, d) + _rms(y, g_ref[...])).reshape(nb, tc, d)

    @pl.when(c == pl.num_programs(0) - 1)
    def _():
        buf_ref[...] = xpad_sc[:, 8 + tc - hist:8 + tc, :]
        hl_ref[...] = h


def _rglru_prompt(gate, xb, x, cw, cb, wa_bf, ba, wx_bf, bx, lam, wo_bf, g, batch, seq, tc=32):
    m, d = x.shape
    bw = d // RNN_BLOCKS
    chunk = pl.BlockSpec((batch, tc, d), lambda c: (0, c, 0))
    const2 = lambda c: (0, 0)
    const3 = lambda c: (0, 0, 0)
    vec = pl.BlockSpec((1, d), const2)
    r3 = lambda t: t.reshape(batch, seq, d)
    y, buf, hl = pl.pallas_call(
        _rglru_body,
        grid=(seq // tc,),
        in_specs=[chunk, chunk, chunk,
                  pl.BlockSpec((CONV_W, d), const2), vec,
                  pl.BlockSpec((RNN_BLOCKS, bw, bw), const3), vec,
                  pl.BlockSpec((RNN_BLOCKS, bw, bw), const3), vec, vec,
                  pl.BlockSpec((d, d), const2), vec],
        out_specs=[chunk,
                   pl.BlockSpec((batch, CONV_W - 1, d), const3),
                   pl.BlockSpec((batch, d), const2)],
        out_shape=[jax.ShapeDtypeStruct((batch, seq, d), F32),
                   jax.ShapeDtypeStruct((batch, CONV_W - 1, d), F32),
                   jax.ShapeDtypeStruct((batch, d), F32)],
        scratch_shapes=[pltpu.VMEM((batch, tc + 8, d), F32), pltpu.VMEM((batch, d), F32)],
        compiler_params=_cparams(("arbitrary",)),
        name="rglru_prompt",
    )(r3(gate), r3(xb), r3(x), cw, cb.reshape(1, d), wa_bf, ba.reshape(1, d), wx_bf, bx.reshape(1, d),
      lam.reshape(1, d), wo_bf, g.reshape(1, d))
    return y.reshape(m, d), buf, hl


def _rglru_step_body(gate_ref, xb_ref, x_ref, conv_ref, h0_ref, cw_ref, cb_ref, wa_ref, ba_ref,
                     wx_ref, bx_ref, lam_ref, wo_ref, g_ref, y_ref, h_ref):
    xc = cb_ref[...] + xb_ref[...] * cw_ref[CONV_W - 1:CONV_W, :]
    for i in range(CONV_W - 1):
        xc = xc + conv_ref[i] * cw_ref[i:i + 1, :]
    a, b = _lru_gates(xc, wa_ref, ba_ref, wx_ref, bx_ref, lam_ref)
    h = a * h0_ref[...] + b
    h_ref[...] = h
    y = (h * _gelu_tanh(gate_ref[...])).astype(BF16)
    y = jnp.dot(y, wo_ref[...], preferred_element_type=F32)
    y_ref[...] = x_ref[...] + _rms(y, g_ref[...])


def _rglru_step(gate, xb, x, conv_t, h0, cw, cb, wa_bf, ba, wx_bf, bx, lam, wo_bf, g):
    m, d = x.shape
    return pl.pallas_call(
        _rglru_step_body,
        out_shape=[jax.ShapeDtypeStruct((m, d), F32), jax.ShapeDtypeStruct((m, d), F32)],
        compiler_params=pltpu.CompilerParams(vmem_limit_bytes=VMEM_LIMIT),
        name="rglru_step",
    )(gate, xb, x, conv_t, h0, cw, cb.reshape(1, d), wa_bf, ba.reshape(1, d), wx_bf, bx.reshape(1, d),
      lam.reshape(1, d), wo_bf, g.reshape(1, d))


def _key_minor(cache):
    nd = cache.ndim
    t = jnp.transpose(cache, (0, 1) + tuple(range(3, nd)) + (2,))
    return t.reshape(t.shape[:2] + (-1,) + t.shape[-2:])


def _page_scores(kt_ref, qb):
    return jnp.sum(kt_ref[...] * qb, axis=1)


def _page_window(pt_ref, streams, li, n_pages, pp, n_rows, row, s0, nwin, prime, compute):
    nslots = streams[0][1].shape[0]
    nsteps = n_pages // pp
    ahead = nslots - 1
    assert nwin % nslots == 0 and nsteps % nwin == 0 and nsteps >= ahead

    def copies(r, s, slot):
        base = r * n_pages + s * pp
        return [pltpu.make_async_copy(hbm.at[li, pt_ref[base + j]], buf.at[slot, j], sem.at[slot, j])
                for hbm, buf, sem in streams for j in range(pp)]

    def start(r, s, slot):
        for c in copies(r, s, slot):
            c.start()

    @pl.when(prime)
    def _():
        for s_first in range(ahead):
            start(row, s_first, s_first)

    for u in range(nwin):
        s, slot = s0 + u, u % nslots
        tgt, tgt_slot = s + ahead, (slot + ahead) % nslots

        @pl.when(tgt < nsteps)
        def _():
            start(row, tgt, tgt_slot)

        @pl.when((tgt >= nsteps) & (row + 1 < n_rows))
        def _():
            start(row + 1, tgt - nsteps, tgt_slot)

        for c in copies(row, s, slot):
            c.wait()
        compute(s, slot)


def _page_stream(pt_ref, streams, li, n_pages, pp, compute):
    b = pl.program_id(0)
    nslots = streams[0][1].shape[0]

    def ring(i, carry):
        _page_window(pt_ref, streams, li, n_pages, pp, pl.num_programs(0), b, nslots * i, nslots,
                     (b == 0) & (i == 0), compute)
        return carry

    lax.fori_loop(0, n_pages // pp // nslots, ring, 0)


def _moba_gate_body(pt_ref, q_ref, kt_hbm, sel_ref, kbuf, ksem, qb_sc, g_sc, *, li, n_pages, pp, ppb):
    page = kbuf.shape[-1]
    nblk = g_sc.shape[0]
    qb_sc[...] = jnp.broadcast_to(q_ref[...], qb_sc.shape)

    def compute(s, slot):
        qb = qb_sc[...]
        for j in range(pp // ppb):
            tot = _page_scores(kbuf.at[slot, ppb * j], qb)
            for t in range(1, ppb):
                tot = tot + _page_scores(kbuf.at[slot, ppb * j + t], qb)
            g = jnp.sum(tot, axis=-1, keepdims=True) * (1.0 / (ppb * page))
            g_sc[s * (pp // ppb) + j] = jnp.broadcast_to(g, g_sc.shape[1:])

    _page_stream(pt_ref, [(kt_hbm, kbuf, ksem)], li, n_pages, pp, compute)

    g = g_sc[...]
    bid = lax.broadcasted_iota(jnp.int32, g.shape, 0)
    lid = lax.broadcasted_iota(jnp.int32, sel_ref.shape, 1)
    out = jnp.zeros(sel_ref.shape, jnp.int32)
    for t in range(MOBA_TOPK):
        mx = jnp.max(g, axis=0, keepdims=True)
        idx = jnp.min(jnp.where(g == mx, bid, nblk), axis=0, keepdims=True)
        out = jnp.where(lid == t, idx[0], out)
        g = jnp.where(bid == idx, -jnp.inf, g)
    sel_ref[...] = out


def _moba_gate(q_col, kt, pt_flat, li, batch, n_pages, pp=16):
    _, _, nh, hd, page = kt.shape
    ppb = MOBA_BLOCK // page
    nblk = n_pages // ppb
    return pl.pallas_call(
        functools.partial(_moba_gate_body, li=li, n_pages=n_pages, pp=pp, ppb=ppb),
        grid_spec=pltpu.PrefetchScalarGridSpec(
            num_scalar_prefetch=1, grid=(batch,),
            in_specs=[pl.BlockSpec((None, nh, hd, 1), lambda b, pt: (b, 0, 0, 0)),
                      pl.BlockSpec(memory_space=pl.ANY)],
            out_specs=pl.BlockSpec((None, nh, LANES), lambda b, pt: (b, 0, 0)),
            scratch_shapes=[pltpu.VMEM((PAGE_SLOTS, pp, nh, hd, page), F32),
                            pltpu.SemaphoreType.DMA((PAGE_SLOTS, pp)),
                            pltpu.VMEM((nh, hd, page), F32),
                            pltpu.VMEM((nblk, nh, LANES), F32)]),
        out_shape=jax.ShapeDtypeStruct((batch, nh, LANES), jnp.int32),
        compiler_params=_cparams(("arbitrary",)),
        name="moba_gate",
    )(pt_flat, q_col, kt)


def _moba_step_body(sel_ref, pt_ref, q_ref, kn_ref, vn_ref, kt_hbm, vt_hbm, o_ref, kbuf, vbuf, ksem, vsem,
                    *, li, n_pages, ppb):
    b = pl.program_id(0)
    _, nh, nsl = kbuf.shape[:3]
    slot = b % 2

    def copies(row, slot):
        out = []
        for h in range(nh):
            for t in range(MOBA_TOPK):
                blk = sel_ref[(row * nh + h) * MOBA_TOPK + t]
                for j in range(ppb):
                    page = pt_ref[row * n_pages + blk * ppb + j]
                    i = t * ppb + j
                    out.append(pltpu.make_async_copy(kt_hbm.at[li, page, h], kbuf.at[slot, h, i],
                                                     ksem.at[slot, h, i]))
                    out.append(pltpu.make_async_copy(vt_hbm.at[li, page, h], vbuf.at[slot, h, i],
                                                     vsem.at[slot, h, i]))
        return out

    @pl.when(b == 0)
    def _():
        for c in copies(0, 0):
            c.start()

    @pl.when(b + 1 < pl.num_programs(0))
    def _():
        for c in copies(b + 1, 1 - slot):
            c.start()

    for c in copies(b, slot):
        c.wait()

    for h in range(nh):
        q8 = jnp.broadcast_to(q_ref[h] * SCALE, (8, q_ref.shape[-1]))
        kt = jnp.concatenate([kbuf[slot, h, i] for i in range(nsl)], axis=1).astype(BF16)
        vt = jnp.concatenate([vbuf[slot, h, i] for i in range(nsl)], axis=1).astype(BF16)
        s = jnp.dot(q8.astype(BF16), kt, preferred_element_type=F32)
        s_self = jnp.sum(q8 * kn_ref[h], axis=-1, keepdims=True)
        m = jnp.maximum(jnp.max(s, axis=-1, keepdims=True), s_self)
        p = jnp.exp(s - m)
        p_self = jnp.exp(s_self - m)
        l = jnp.sum(p, axis=-1, keepdims=True) + p_self
        pv = lax.dot_general(p.astype(BF16), vt, NT_DIMS, preferred_element_type=F32)
        o_ref[h] = (pv + p_self * vn_ref[h]) / l


def _moba_step(q4, kn4, vn4, kt, vt, sel_flat, pt_flat, li, batch, n_pages):
    _, _, nh, hd, page = kt.shape
    ppb = MOBA_BLOCK // page
    nsl = MOBA_TOPK * ppb
    row = pl.BlockSpec((None, nh, 1, hd), lambda b, sel, pt: (b, 0, 0, 0))
    hbm = pl.BlockSpec(memory_space=pl.ANY)
    return pl.pallas_call(
        functools.partial(_moba_step_body, li=li, n_pages=n_pages, ppb=ppb),
        grid_spec=pltpu.PrefetchScalarGridSpec(
            num_scalar_prefetch=2, grid=(batch,),
            in_specs=[row, row, row, hbm, hbm],
            out_specs=pl.BlockSpec((None, nh, 8, hd), lambda b, sel, pt: (b, 0, 0, 0)),
            scratch_shapes=[pltpu.VMEM((2, nh, nsl, hd, page), F32), pltpu.VMEM((2, nh, nsl, hd, page), F32),
                            pltpu.SemaphoreType.DMA((2, nh, nsl)), pltpu.SemaphoreType.DMA((2, nh, nsl))]),
        out_shape=jax.ShapeDtypeStruct((batch, nh, 8, hd), F32),
        compiler_params=_cparams(("arbitrary",)),
        name="moba_step",
    )(sel_flat, pt_flat, q4, kn4, vn4, kt, vt)


def _diff_row_init(q_ref, qb_sc, m_sc, l_sc, acc_sc):
    qb_sc[...] = jnp.broadcast_to(q_ref[...] * SCALE, qb_sc.shape)
    m_sc[...] = jnp.full(m_sc.shape, NEG, F32)
    l_sc[...] = jnp.zeros_like(l_sc)
    acc_sc[...] = jnp.zeros_like(acc_sc)


def _diff_row_step(slot, kbuf, vbuf, qb_sc, m_sc, l_sc, acc_sc):
    pp, page = kbuf.shape[1], kbuf.shape[-1]
    qb = qb_sc[...]
    sc = jnp.concatenate([_page_scores(kbuf.at[slot, j], qb) for j in range(pp)], axis=1)
    m_old = m_sc[...]
    m_new = jnp.maximum(m_old, jnp.max(sc, axis=-1, keepdims=True))
    alpha = jnp.exp(m_old - m_new)
    p = jnp.exp(sc - m_new)
    l_sc[...] = alpha * l_sc[...] + jnp.sum(p, axis=-1, keepdims=True)
    m_sc[...] = m_new
    pb = p.astype(BF16)
    for h in range(DIFF_HEADS):
        vh = jnp.concatenate([vbuf[slot, j, pl.ds(h, page, stride=DIFF_HEADS), :] for j in range(pp)],
                             axis=0)
        acc_sc[h] = alpha * acc_sc[h] + jnp.dot(pb, vh.astype(BF16), preferred_element_type=F32)


def _diff_row_finish(q_ref, kn_ref, vn_ref, lam_ref, gsub_ref, o_ref, m_sc, l_sc, acc_sc, lam_init):
    s_self = jnp.sum(q_ref[...] * SCALE * kn_ref[...], axis=1)
    m_old = m_sc[...]
    m_new = jnp.maximum(m_old, s_self)
    alpha = jnp.exp(m_old - m_new)
    p_self = jnp.exp(s_self - m_new)
    l = alpha * l_sc[...] + p_self
    lam = _diff_lambda(lam_ref, lam_init)
    for h in range(DIFF_HEADS):
        vn = vn_ref[:, h * DIFF_DIM:(h + 1) * DIFF_DIM]
        o = (alpha * acc_sc[h] + p_self * vn) / l
        od = o[2 * h:2 * h + 1, :] - lam * o[2 * h + 1:2 * h + 2, :]
        o_ref[:, h * DIFF_DIM:(h + 1) * DIFF_DIM] = _rms(od, gsub_ref[...]) * (1.0 - lam_init)


def _mlp_diff_body(pt_ref, x_ref, g1_ref, wu_ref, wd_ref, g2_ref, q_ref, kn_ref, vn_ref, lam_ref, gsub_ref,
                   kt_hbm, v_hbm, y_ref, o_ref, xn_sc, acc_sc, kbuf, vbuf, ksem, vsem, qb_sc, m_sc, l_sc,
                   dacc_sc, *, li, n_pages, n_rows, nwin, lam_init):
    pp = kbuf.shape[1]
    g = pl.program_id(0) * pl.num_programs(1) + pl.program_id(1)
    gpr = n_pages // pp // nwin
    row, part = g // gpr, g % gpr
    state = (qb_sc, m_sc, l_sc, dacc_sc)

    @pl.when(part == 0)
    def _():
        _diff_row_init(q_ref, *state)

    _page_window(pt_ref, [(kt_hbm, kbuf, ksem), (v_hbm, vbuf, vsem)], li, n_pages, pp, n_rows, row,
                 part * nwin, nwin, g == 0, lambda s, slot: _diff_row_step(slot, kbuf, vbuf, *state))

    @pl.when(part == gpr - 1)
    def _():
        _diff_row_finish(q_ref, kn_ref, vn_ref, lam_ref, gsub_ref, o_ref, m_sc, l_sc, dacc_sc, lam_init)

    _mlp_body(x_ref, g1_ref, wu_ref, wd_ref, g2_ref, y_ref, xn_sc, acc_sc)


def _mlp_diff(x, g1, wu_bf, wd_bf, g2, q_col, kn_col, vn3, lam4, gsub, kt, v2, pt_flat, li, n_pages,
              lam_init, tm, tf, pp=8):
    m, d = x.shape
    ff = wu_bf.shape[1]
    n_rows, nrow, hd, _ = q_col.shape
    page = kt.shape[-1]
    vrows, dv = v2.shape[2:]
    w = vn3.shape[-1]
    nf = ff // tf
    n_grid = (m // tm) * nf
    nwin = n_rows * (n_pages // pp) // n_grid
    assert nwin * n_grid == n_rows * (n_pages // pp) and nwin % PAGE_SLOTS == 0
    gpr = n_pages // pp // nwin
    srow = lambda i, f, pt: ((i * nf + f) // gpr, 0, 0)
    col = pl.BlockSpec((None, nrow, hd, 1), lambda i, f, pt: ((i * nf + f) // gpr, 0, 0, 0))
    vec = pl.BlockSpec((1, d), lambda i, f, pt: (0, 0))
    hbm = pl.BlockSpec(memory_space=pl.ANY)
    return pl.pallas_call(
        functools.partial(_mlp_diff_body, li=li, n_pages=n_pages, n_rows=n_rows, nwin=nwin, lam_init=lam_init),
        grid_spec=pltpu.PrefetchScalarGridSpec(
            num_scalar_prefetch=1, grid=(m // tm, nf),
            in_specs=[pl.BlockSpec((tm, d), lambda i, f, pt: (i, 0)), vec,
                      pl.BlockSpec((d, tf), lambda i, f, pt: (0, f)),
                      pl.BlockSpec((tf, d), lambda i, f, pt: (f, 0)), vec,
                      col, col, pl.BlockSpec((None, 1, w), srow),
                      pl.BlockSpec((4, HEAD_DIM), lambda i, f, pt: (0, 0)),
                      pl.BlockSpec((1, DIFF_DIM), lambda i, f, pt: (0, 0)), hbm, hbm],
            out_specs=[pl.BlockSpec((tm, d), lambda i, f, pt: (i, 0)), pl.BlockSpec((None, 1, w), srow)],
            scratch_shapes=[pltpu.VMEM((tm, d), BF16), pltpu.VMEM((tm, d), F32),
                            pltpu.VMEM((PAGE_SLOTS, pp, nrow, hd, page), F32),
                            pltpu.VMEM((PAGE_SLOTS, pp, vrows, dv), F32),
                            pltpu.SemaphoreType.DMA((PAGE_SLOTS, pp)), pltpu.SemaphoreType.DMA((PAGE_SLOTS, pp)),
                            pltpu.VMEM((nrow, hd, page), F32), pltpu.VMEM((nrow, 1), F32),
                            pltpu.VMEM((nrow, 1), F32), pltpu.VMEM((DIFF_HEADS, nrow, dv), F32)]),
        out_shape=[jax.ShapeDtypeStruct((m, d), F32), jax.ShapeDtypeStruct((n_rows, 1, w), F32)],
        compiler_params=_cparams(("arbitrary", "arbitrary")),
        name="mlp_diff",
    )(pt_flat, x, g1.reshape(1, d), wu_bf, wd_bf, g2.reshape(1, d), q_col, kn_col, vn3, lam4,
      gsub.reshape(1, DIFF_DIM), kt, v2)


def kernel(x_prompt, x_sample, cache_moba_k, cache_moba_v, cache_diff_k, cache_diff_v, state_conv, state_rnn, page_table, attn_g_pre, attn_w_in, diff_lambda_q1, diff_lambda_k1, diff_lambda_q2, diff_lambda_k2, diff_g_sub, attn_w_out, attn_g_post, rnn_g_pre, rnn_w_in, rnn_conv_w, rnn_conv_b, rnn_w_a, rnn_b_a, rnn_w_x, rnn_b_x, rnn_lambda, rnn_w_out, rnn_g_post, mlp_g_pre, mlp_w_up, mlp_w_down, mlp_g_post):
    batch, seq, d = x_prompt.shape
    dec_batch, dec_seq, _ = x_sample.shape
    assert dec_seq == 1
    depth = mlp_w_up.shape[0]
    na, n_pool, page = cache_moba_k.shape[:3]
    n_pages = page_table.shape[1]
    past_len = n_pages * page
    mw = MOBA_HEADS * HEAD_DIM
    dw = DIFF_HEADS * DIFF_DIM

    xp = x_prompt.reshape(batch * seq, d)
    xs = x_sample.reshape(dec_batch, d)
    tab_p = _rope_tables(jnp.arange(seq, dtype=jnp.int32))
    tab_s = _rope_tables(jnp.full((dec_batch,), past_len, jnp.int32))
    pt_flat = page_table.reshape(-1)
    cmk_t = _key_minor(cache_moba_k)
    cmv_t = _key_minor(cache_moba_v)
    cdk_t = _key_minor(cache_diff_k)
    cdv2 = cache_diff_v.reshape(na, n_pool, page * DIFF_HEADS, DIFF_DIM)

    tm_p = 512
    attn_rope = (True, True, False, True, True, False)
    outs = {k: [] for k in ("pmk", "pmv", "pdk", "pdv", "pconv", "prnn",
                            "smk", "smv", "sdk", "sdv", "sconv", "srnn")}
    for layer in range(depth):
        li = layer // 2
        wu = mlp_w_up[layer].astype(BF16)
        wd = mlp_w_down[layer].astype(BF16)
        if layer % 2 == 0:
            lam_init = 0.8 - 0.6 * math.exp(-0.3 * layer)
            w_in = attn_w_in[li].astype(BF16)
            w_out = attn_w_out[li].astype(BF16)
            lam4 = jnp.stack([diff_lambda_q1[li], diff_lambda_k1[li], diff_lambda_q2[li], diff_lambda_k2[li]])
            mq, mk, mv, dq, dk, dv, mk_t, mv_t, dk_t = _in_proj(
                xp, attn_g_pre[li], w_in, tab_p, attn_rope, tm_p,
                t_flags=(False, True, True, False, True, False))
            o_m = _moba_prompt(mq, mk, mv, batch, seq)
            o_d = _diff_prompt(dq, dk, dv, lam4, diff_g_sub[li], batch, seq, lam_init)
            xp = _attn_out(o_m, o_d, w_out, attn_g_post[li], xp, tm_p)
            outs["pmk"].append(jnp.moveaxis(mk_t.reshape(batch, MOBA_HEADS, HEAD_DIM, seq), -1, 1))
            outs["pmv"].append(jnp.moveaxis(mv_t.reshape(batch, MOBA_HEADS, HEAD_DIM, seq), -1, 1))
            outs["pdk"].append(jnp.moveaxis(dk_t.reshape(batch, DIFF_HEADS, 2, HEAD_DIM, seq), -1, 1))
            outs["pdv"].append(dv.reshape(batch, seq, DIFF_HEADS, DIFF_DIM))
            mq, mk, mv, dq, dk, dv = _in_proj(xs, attn_g_pre[li], w_in, tab_s, attn_rope, dec_batch)
            col = lambda t: t.reshape(dec_batch, -1, HEAD_DIM, 1)
            hrow = lambda t: t.reshape(dec_batch, MOBA_HEADS, 1, HEAD_DIM)
            xp, o_d = _mlp_diff(xp, mlp_g_pre[layer], wu, wd, mlp_g_post[layer],
                                col(dq), col(dk), dv.reshape(dec_batch, 1, dw), lam4, diff_g_sub[li],
                                cdk_t, cdv2, pt_flat, li, n_pages, lam_init, 1024, 512)
            sel = _moba_gate(col(mq), cmk_t, pt_flat, li, dec_batch, n_pages)
            sel = sel[:, :, :MOBA_TOPK].reshape(-1)
            o_m = _moba_step(hrow(mq), hrow(mk), hrow(mv), cmk_t, cmv_t, sel, pt_flat, li,
                             dec_batch, n_pages)[:, :, 0, :]
            xs = _attn_out(o_m.reshape(dec_batch, mw), o_d.reshape(dec_batch, dw), w_out,
                           attn_g_post[li], xs, dec_batch)
            outs["smk"].append(mk.reshape(dec_batch, 1, MOBA_HEADS, HEAD_DIM))
            outs["smv"].append(mv.reshape(dec_batch, 1, MOBA_HEADS, HEAD_DIM))
            outs["sdk"].append(dk.reshape(dec_batch, 1, DIFF_HEADS, 2, HEAD_DIM))
            outs["sdv"].append(dv.reshape(dec_batch, 1, DIFF_HEADS, DIFF_DIM))
        else:
            w_in = rnn_w_in[li].astype(BF16)
            wts = (rnn_conv_w[li], rnn_conv_b[li], rnn_w_a[li].astype(BF16), rnn_b_a[li].reshape(-1),
                   rnn_w_x[li].astype(BF16), rnn_b_x[li].reshape(-1), rnn_lambda[li],
                   rnn_w_out[li].astype(BF16), rnn_g_post[li])
            gate, xb = _in_proj(xp, rnn_g_pre[li], w_in, tab_p, (False, False), tm_p)
            xp, cbuf, hlast = _rglru_prompt(gate, xb, xp, *wts, batch, seq)
            outs["pconv"].append(cbuf)
            outs["prnn"].append(hlast.reshape(batch, d))
            gate, xb = _in_proj(xs, rnn_g_pre[li], w_in, tab_s, (False, False), dec_batch)
            conv_t = jnp.swapaxes(state_conv[li], 0, 1)
            xs, hnew = _rglru_step(gate, xb, xs, conv_t, state_rnn[li], *wts)
            outs["sconv"].append(jnp.concatenate([state_conv[li][:, 1:], xb[:, None, :]], axis=1))
            outs["srnn"].append(hnew)
            xp = _mlp(xp, mlp_g_pre[layer], wu, wd, mlp_g_post[layer], 1024, 1024)
        xs = _mlp(xs, mlp_g_pre[layer], wu, wd, mlp_g_post[layer], dec_batch, 512)

    st = lambda k: jnp.stack(outs[k])
    return (xp.reshape(batch, seq, d), xs.reshape(dec_batch, 1, d),
            st("pmk"), st("pmv"), st("pdk"), st("pdv"), st("pconv"), st("prnn"),
            st("smk"), st("smv"), st("sdk"), st("sdv"), st("sconv"), st("srnn"))
```

```python
import functools
import math

import jax
import jax.numpy as jnp
from jax import lax
from jax.experimental import pallas as pl
from jax.experimental.pallas import tpu as pltpu

F32 = jnp.float32
BF16 = jnp.bfloat16
HIGHEST = lax.Precision.HIGHEST

HEAD_DIM = 64
ROT_DIM = HEAD_DIM // 4
ROPE_THETA = 500000.0
MOBA_HEADS = 8
MOBA_BLOCK = 256
MOBA_TOPK = 3
DIFF_HEADS = 4
DIFF_DIM = 2 * HEAD_DIM
RNN_BLOCKS = 4
CONV_W = 4
LRU_C = 8.0
EPS = 1e-6
LANES = 128
NEG = -1e30
SCALE = HEAD_DIM ** -0.5
SCALE_LOG2E = SCALE * math.log2(math.e)
NT_DIMS = (((1,), (1,)), ((), ()))
ATTN_CHAINS = 4
PAGE_SLOTS = 4
VMEM_LIMIT = 52 * 1024 * 1024


def _cparams(sem):
    return pltpu.CompilerParams(dimension_semantics=sem, vmem_limit_bytes=VMEM_LIMIT)


def _rms(x, g):
    ms = jnp.mean(x * x, axis=-1, keepdims=True)
    return x * lax.rsqrt(ms + EPS) * g


def _rope_tables(pos):
    half = ROT_DIM // 2
    inv = jnp.exp(-math.log(ROPE_THETA) * jnp.arange(half, dtype=F32) * (2.0 / ROT_DIM))
    ang = pos.astype(F32)[:, None] * inv[None, :]
    cos, sin = jnp.cos(ang), jnp.sin(ang)
    n = pos.shape[0]
    pad = jnp.zeros((n, HEAD_DIM - ROT_DIM), F32)
    c64 = jnp.concatenate([cos, cos, pad + 1.0], axis=1)
    sa64 = jnp.concatenate([-sin, jnp.zeros((n, half), F32), pad], axis=1)
    sb64 = jnp.concatenate([jnp.zeros((n, half), F32), sin, pad], axis=1)
    rep = LANES // HEAD_DIM
    return tuple(jnp.tile(t, (1, rep)) for t in (c64, sa64, sb64))


def _in_proj_body(x_ref, g_ref, w_ref, c_ref, sa_ref, sb_ref, *outs, rope_flags, t_flags, tn):
    n_out = len(rope_flags)
    outs_t = iter(outs[n_out:])
    xn = _rms(x_ref[...], g_ref[...]).astype(BF16)
    for j, out_ref in enumerate(outs[:n_out]):
        u = jnp.dot(xn, w_ref[:, j * tn:(j + 1) * tn], preferred_element_type=F32)
        if rope_flags[j]:
            c, sa, sb = c_ref[...], sa_ref[...], sb_ref[...]
            u = jnp.concatenate(
                [uk * c + pltpu.roll(uk, LANES - ROT_DIM // 2, 1) * sa + pltpu.roll(uk, ROT_DIM // 2, 1) * sb
                 for uk in (u[:, k * LANES:(k + 1) * LANES] for k in range(tn // LANES))], axis=1)
        out_ref[...] = u
        if t_flags[j]:
            next(outs_t)[...] = u.T


def _in_proj(x, g, w_bf, tables, rope_flags, tm, t_flags=None):
    m, d = x.shape
    n_out = len(rope_flags)
    t_flags = t_flags or (False,) * n_out
    n = w_bf.shape[1]
    tn = n // n_out
    seq = tables[0].shape[0]
    npos = seq // tm
    tab_spec = pl.BlockSpec((tm, LANES), lambda i: (i % npos, 0))
    n_t = sum(t_flags)
    return pl.pallas_call(
        functools.partial(_in_proj_body, rope_flags=rope_flags, t_flags=t_flags, tn=tn),
        grid=(m // tm,),
        in_specs=[pl.BlockSpec((tm, d), lambda i: (i, 0)),
                  pl.BlockSpec((1, d), lambda i: (0, 0)),
                  pl.BlockSpec((d, n), lambda i: (0, 0)),
                  tab_spec, tab_spec, tab_spec],
        out_specs=[pl.BlockSpec((tm, tn), lambda i: (i, 0))] * n_out
        + [pl.BlockSpec((None, tn, tm), lambda i: (i // npos, 0, i % npos))] * n_t,
        out_shape=[jax.ShapeDtypeStruct((m, tn), F32)] * n_out
        + [jax.ShapeDtypeStruct((m // seq, tn, seq), F32)] * n_t,
        compiler_params=_cparams(("arbitrary",)),
        name="in_proj",
    )(x, g.reshape(1, d), w_bf, *tables)


def _attn_out_body(om_ref, od_ref, w_ref, g_ref, x_ref, y_ref):
    km = om_ref.shape[1]
    o = jnp.dot(om_ref[...].astype(BF16), w_ref[:km, :], preferred_element_type=F32)
    o = o + jnp.dot(od_ref[...].astype(BF16), w_ref[km:, :], preferred_element_type=F32)
    y_ref[...] = x_ref[...] + _rms(o, g_ref[...])


def _attn_out(om, od, w_bf, g, x, tm):
    m, d = x.shape
    km, kd = om.shape[1], od.shape[1]
    return pl.pallas_call(
        _attn_out_body,
        grid=(m // tm,),
        in_specs=[pl.BlockSpec((tm, km), lambda i: (i, 0)),
                  pl.BlockSpec((tm, kd), lambda i: (i, 0)),
                  pl.BlockSpec((km + kd, d), lambda i: (0, 0)),
                  pl.BlockSpec((1, d), lambda i: (0, 0)),
                  pl.BlockSpec((tm, d), lambda i: (i, 0))],
        out_specs=pl.BlockSpec((tm, d), lambda i: (i, 0)),
        out_shape=jax.ShapeDtypeStruct((m, d), F32),
        compiler_params=_cparams(("arbitrary",)),
        name="attn_out",
    )(om, od, w_bf, g.reshape(1, d), x)


def _mlp_body(x_ref, g1_ref, wu_ref, wd_ref, g2_ref, y_ref, xn_sc, acc_sc):
    f = pl.program_id(1)

    @pl.when(f == 0)
    def _():
        xn_sc[...] = _rms(x_ref[...], g1_ref[...]).astype(BF16)
        acc_sc[...] = jnp.zeros_like(acc_sc)

    h = jnp.dot(xn_sc[...], wu_ref[...], preferred_element_type=F32)
    h = jnp.square(jnp.maximum(h, 0.0))
    acc_sc[...] += jnp.dot(h.astype(BF16), wd_ref[...], preferred_element_type=F32)

    @pl.when(f == pl.num_programs(1) - 1)
    def _():
        y_ref[...] = x_ref[...] + _rms(acc_sc[...], g2_ref[...])


def _mlp(x, g1, wu_bf, wd_bf, g2, tm, tf):
    m, d = x.shape
    ff = wu_bf.shape[1]
    return pl.pallas_call(
        _mlp_body,
        grid=(m // tm, ff // tf),
        in_specs=[pl.BlockSpec((tm, d), lambda i, f: (i, 0)),
                  pl.BlockSpec((1, d), lambda i, f: (0, 0)),
                  pl.BlockSpec((d, tf), lambda i, f: (0, f)),
                  pl.BlockSpec((tf, d), lambda i, f: (f, 0)),
                  pl.BlockSpec((1, d), lambda i, f: (0, 0))],
        out_specs=pl.BlockSpec((tm, d), lambda i, f: (i, 0)),
        out_shape=jax.ShapeDtypeStruct((m, d), F32),
        scratch_shapes=[pltpu.VMEM((tm, d), BF16), pltpu.VMEM((tm, d), F32)],
        compiler_params=_cparams(("arbitrary", "arbitrary")),
        name="mlp",
    )(x, g1.reshape(1, d), wu_bf, wd_bf, g2.reshape(1, d))


def _causal_mask(st):
    kidx = lax.broadcasted_iota(jnp.int32, st.shape, 0)
    qidx = lax.broadcasted_iota(jnp.int32, st.shape, 1)
    return jnp.where(kidx <= qidx, st, NEG)


def _tile_attention(chains, nblk, blk):
    nk = nblk * blk
    for kh, qs, _, s_ref, _, _ in chains:
        s_ref[0:nk, :] = lax.dot_general(kh, qs, NT_DIMS, preferred_element_type=F32)
    ms = []
    for _, _, _, s_ref, _, bias in chains:
        m = None
        for n in range(nblk):
            rows = slice(n * blk, (n + 1) * blk)
            piece = s_ref[rows, :]
            if n == nblk - 1:
                piece = _causal_mask(piece)
                s_ref[rows, :] = piece
            elif bias is not None:
                piece = piece + bias[n:n + 1, :]
                s_ref[rows, :] = piece
            pm = jnp.max(piece, axis=0, keepdims=True)
            m = pm if m is None else jnp.maximum(m, pm)
        ms.append(m)
    ls = []
    for (_, _, _, s_ref, p_ref, _), m in zip(chains, ms):
        l = None
        for n in range(nblk):
            rows = slice(n * blk, (n + 1) * blk)
            p = jnp.exp2(s_ref[rows, :] - m)
            ps = jnp.sum(p, axis=0, keepdims=True)
            l = ps if l is None else l + ps
            p_ref[rows, :] = p.astype(BF16)
        ls.append(l)
    accs = [jnp.dot(vt, p_ref[0:nk, :], preferred_element_type=F32) for _, _, vt, _, p_ref, _ in chains]
    return [acc / l for acc, l in zip(accs, ls)]


def _moba_select_bias(gt, k_eff):
    npast = gt.shape[0]
    rid = lax.broadcasted_iota(jnp.int32, gt.shape, 0)
    rank = jnp.zeros(gt.shape, jnp.int32)
    for mm in range(npast):
        gm = gt[mm:mm + 1, :]
        beats = (gm > gt) | ((gm == gt) & (rid > mm))
        rank = rank + beats.astype(jnp.int32)
    return jnp.where(rank < k_eff, 0.0, NEG)


def _moba_body(q_ref, k_ref, v_ref, o_ref, kmean_sc, kh_sc, vt_sc, qf_sc, qs_sc, s_sc, p_sc, ot_sc, *, nb):
    qi = pl.program_id(1)
    blk = MOBA_BLOCK
    hd = HEAD_DIM
    k_eff = min(MOBA_TOPK, nb)

    @pl.when(qi == 0)
    def _prep():
        for n in range(nb):
            rows = slice(n * blk, (n + 1) * blk)
            kblk = k_ref[rows, :]
            kmean = jnp.sum(kblk, axis=0, keepdims=True) * (1.0 / blk)
            vtb = v_ref[rows, :].T
            for h in range(MOBA_HEADS):
                hs = slice(h * hd, (h + 1) * hd)
                kmean_sc[h, n:n + 1, :] = kmean[:, hs]
                kh_sc[h, rows, :] = kblk[:, hs].astype(BF16)
                vt_sc[h, :, rows] = vtb[hs, :].astype(BF16)

    for h in range(MOBA_HEADS):
        qh = q_ref[:, h * hd:(h + 1) * hd]
        qf_sc[h] = qh
        qs_sc[h] = (qh * SCALE_LOG2E).astype(BF16)

    for qq in range(nb):
        @pl.when(qi == qq)
        def _tile(qq=qq):
            nk = (qq + 1) * blk

            def head_group(hg, carry):
                chains = []
                for i in range(ATTN_CHAINS):
                    h = ATTN_CHAINS * hg + i
                    bias = None
                    if qq > k_eff:
                        gt = lax.dot_general(kmean_sc[h, 0:qq, :], qf_sc[h], NT_DIMS, precision=HIGHEST,
                                             preferred_element_type=F32)
                        bias = _moba_select_bias(gt, k_eff)
                    chains.append((kh_sc[h, 0:nk, :], qs_sc[h], vt_sc[h, :, 0:nk], s_sc.at[i],
                                   p_sc.at[i], bias))
                o = _tile_attention(chains, qq + 1, blk)
                rows = ATTN_CHAINS * hd
                ot_sc[pl.ds(pl.multiple_of(hg * rows, rows), rows), :] = jnp.concatenate(o, axis=0)
                return carry

            lax.fori_loop(0, MOBA_HEADS // ATTN_CHAINS, head_group, 0)

    o_ref[...] = ot_sc[...].T


def _moba_prompt(q, k, v, batch, seq):
    w = q.shape[1]
    blk = MOBA_BLOCK
    nb = seq // blk
    return pl.pallas_call(
        functools.partial(_moba_body, nb=nb),
        grid=(batch, nb),
        in_specs=[pl.BlockSpec((blk, w), lambda b, i: (b * nb + i, 0)),
                  pl.BlockSpec((seq, w), lambda b, i: (b, 0)),
                  pl.BlockSpec((seq, w), lambda b, i: (b, 0))],
        out_specs=pl.BlockSpec((blk, w), lambda b, i: (b * nb + i, 0)),
        out_shape=jax.ShapeDtypeStruct(q.shape, F32),
        scratch_shapes=[pltpu.VMEM((MOBA_HEADS, nb, HEAD_DIM), F32),
                        pltpu.VMEM((MOBA_HEADS, seq, HEAD_DIM), BF16),
                        pltpu.VMEM((MOBA_HEADS, HEAD_DIM, seq), BF16),
                        pltpu.VMEM((MOBA_HEADS, blk, HEAD_DIM), F32),
                        pltpu.VMEM((MOBA_HEADS, blk, HEAD_DIM), BF16),
                        pltpu.VMEM((ATTN_CHAINS, seq, blk), F32),
                        pltpu.VMEM((ATTN_CHAINS, seq, blk), BF16),
                        pltpu.VMEM((w, blk), F32)],
        compiler_params=_cparams(("arbitrary", "arbitrary")),
        name="moba_prompt",
    )(q, k, v)


def _diff_lambda(lam_ref, lam_init):
    lv = lam_ref[...]
    s1 = jnp.sum(lv[0:1, :] * lv[1:2, :], axis=-1, keepdims=True)
    s2 = jnp.sum(lv[2:3, :] * lv[3:4, :], axis=-1, keepdims=True)
    return jnp.exp(s1) - jnp.exp(s2) + lam_init


def _diff_body(q_ref, k_ref, v_ref, lam_ref, gsub_ref, o_ref, kh_sc, vt_sc, qs_sc, s_sc, p_sc, ot_sc,
               *, nb, tq, lam_init):
    qi = pl.program_id(1)
    hd = HEAD_DIM
    nrow = 2 * DIFF_HEADS

    @pl.when(qi == 0)
    def _prep():
        for n in range(nb):
            rows = slice(n * tq, (n + 1) * tq)
            kblk = k_ref[rows, :]
            vtb = v_ref[rows, :].T
            for r in range(nrow):
                kh_sc[r, rows, :] = kblk[:, r * hd:(r + 1) * hd].astype(BF16)
            for h in range(DIFF_HEADS):
                vt_sc[h, :, rows] = vtb[h * DIFF_DIM:(h + 1) * DIFF_DIM, :].astype(BF16)

    for r in range(nrow):
        qs_sc[r] = (q_ref[:, r * hd:(r + 1) * hd] * SCALE_LOG2E).astype(BF16)
    lam = _diff_lambda(lam_ref, lam_init)

    for qq in range(nb):
        @pl.when(qi == qq)
        def _tile(qq=qq):
            nk = (qq + 1) * tq

            hpi = ATTN_CHAINS // 2

            def head_group(hg, carry):
                chains = []
                for i in range(ATTN_CHAINS):
                    r = ATTN_CHAINS * hg + i
                    chains.append((kh_sc[r, 0:nk, :], qs_sc[r], vt_sc[hpi * hg + i // 2, :, 0:nk],
                                   s_sc.at[i], p_sc.at[i], None))
                o = _tile_attention(chains, qq + 1, tq)
                for j in range(hpi):
                    od = o[2 * j] - lam * o[2 * j + 1]
                    ms = jnp.mean(od * od, axis=0, keepdims=True)
                    od = od * lax.rsqrt(ms + EPS) * gsub_ref[...]
                    row0 = pl.multiple_of((hpi * hg + j) * DIFF_DIM, DIFF_DIM)
                    ot_sc[pl.ds(row0, DIFF_DIM), :] = od * (1.0 - lam_init)
                return carry

            lax.fori_loop(0, 2 * DIFF_HEADS // ATTN_CHAINS, head_group, 0)

    o_ref[...] = ot_sc[...].T


def _diff_prompt(q, k, v, lam4, gsub, batch, seq, lam_init, tq=256):
    w = q.shape[1]
    nb = seq // tq
    return pl.pallas_call(
        functools.partial(_diff_body, nb=nb, tq=tq, lam_init=lam_init),
        grid=(batch, nb),
        in_specs=[pl.BlockSpec((tq, w), lambda b, i: (b * nb + i, 0)),
                  pl.BlockSpec((seq, w), lambda b, i: (b, 0)),
                  pl.BlockSpec((seq, w), lambda b, i: (b, 0)),
                  pl.BlockSpec((4, HEAD_DIM), lambda b, i: (0, 0)),
                  pl.BlockSpec((DIFF_DIM, 1), lambda b, i: (0, 0))],
        out_specs=pl.BlockSpec((tq, w), lambda b, i: (b * nb + i, 0)),
        out_shape=jax.ShapeDtypeStruct(q.shape, F32),
        scratch_shapes=[pltpu.VMEM((2 * DIFF_HEADS, seq, HEAD_DIM), BF16),
                        pltpu.VMEM((DIFF_HEADS, DIFF_DIM, seq), BF16),
                        pltpu.VMEM((2 * DIFF_HEADS, tq, HEAD_DIM), BF16),
                        pltpu.VMEM((ATTN_CHAINS, seq, tq), F32),
                        pltpu.VMEM((ATTN_CHAINS, seq, tq), BF16),
                        pltpu.VMEM((w, tq), F32)],
        compiler_params=_cparams(("arbitrary", "arbitrary")),
        name="diff_prompt",
    )(q, k, v, lam4, gsub.reshape(DIFF_DIM, 1))


def _gelu_tanh(x):
    return 0.5 * x * (1.0 + jnp.tanh(math.sqrt(2.0 / math.pi) * (x + 0.044715 * (x * x * x))))


def _log_sigmoid(x):
    return jnp.minimum(x, 0.0) - jnp.log1p(jnp.exp(-jnp.abs(x)))


def _lru_gates(xc, wa_ref, ba_ref, wx_ref, bx_ref, lam_ref):
    bw = wa_ref.shape[1]
    xcb = xc.astype(BF16)
    ra, rx = [], []
    for n in range(RNN_BLOCKS):
        xn = xcb[:, n * bw:(n + 1) * bw]
        ra.append(jnp.dot(xn, wa_ref[n], preferred_element_type=F32))
        rx.append(jnp.dot(xn, wx_ref[n], preferred_element_type=F32))
    r = jax.nn.sigmoid(jnp.concatenate(ra, axis=1) + ba_ref[...])
    ig = jax.nn.sigmoid(jnp.concatenate(rx, axis=1) + bx_ref[...])
    log_a = LRU_C * r * _log_sigmoid(lam_ref[...])
    a = jnp.exp(log_a)
    b = jnp.sqrt(-jnp.tanh(log_a) * (1.0 + a * a)) * (ig * xc)
    return a, b


def _scan_rows(a, b_in, h0, nb, tc):
    sub = lax.broadcasted_iota(jnp.int32, (8, a.shape[1]), 0)
    rows, last = [], []
    for b in range(nb):
        h_in = h0[b:b + 1, :]
        for g in range(tc // 8):
            r0 = b * tc + 8 * g
            ga, gb = a[r0:r0 + 8, :], b_in[r0:r0 + 8, :]
            for sh in (1, 2, 4):
                keep = sub >= sh
                a_sh = jnp.where(keep, pltpu.roll(ga, sh, 0), 1.0)
                b_sh = jnp.where(keep, pltpu.roll(gb, sh, 0), 0.0)
                gb = ga * b_sh + gb
                ga = ga * a_sh
            h = ga * h_in + gb
            rows.append(h)
            h_in = h[7:8, :]
        last.append(h_in)
    return jnp.concatenate(rows, axis=0), jnp.concatenate(last, axis=0)


def _rglru_body(gate_ref, xb_ref, x_ref, cw_ref, cb_ref, wa_ref, ba_ref, wx_ref, bx_ref, lam_ref,
                wo_ref, g_ref, y_ref, buf_ref, hl_ref, xpad_sc, hc_sc):
    c = pl.program_id(0)
    nb, tc, d = xb_ref.shape
    hist = CONV_W - 1

    @pl.when(c == 0)
    def _():
        xpad_sc[:, 0:8, :] = jnp.zeros((nb, 8, d), F32)
        hc_sc[...] = jnp.zeros_like(hc_sc)

    @pl.when(c > 0)
    def _():
        xpad_sc[:, 0:8, :] = xpad_sc[:, tc:tc + 8, :]

    xpad_sc[:, 8:8 + tc, :] = xb_ref[...]
    xcs = []
    for b in range(nb):
        xc = cb_ref[...] + xpad_sc[b, 8 - hist:8 - hist + tc, :] * cw_ref[0:1, :]
        for i in range(1, CONV_W):
            xc = xc + xpad_sc[b, 8 - hist + i:8 - hist + i + tc, :] * cw_ref[i:i + 1, :]
        xcs.append(xc)
    a, b_in = _lru_gates(jnp.concatenate(xcs, axis=0), wa_ref, ba_ref, wx_ref, bx_ref, lam_ref)
    hall, h = _scan_rows(a, b_in, hc_sc[...], nb, tc)
    hc_sc[...] = h
    y = (hall * _gelu_tanh(gate_ref[...].reshape(nb * tc, d))).astype(BF16)
    y = jnp.dot(y, wo_ref[...], preferred_element_type=F32)
    y_ref[...] = (x_ref[...].reshape(nb * tc, d) + _rms(y, g_ref[...])).reshape(nb, tc, d)

    @pl.when(c == pl.num_programs(0) - 1)
    def _():
        buf_ref[...] = xpad_sc[:, 8 + tc - hist:8 + tc, :]
        hl_ref[...] = h


def _rglru_prompt(gate, xb, x, cw, cb, wa_bf, ba, wx_bf, bx, lam, wo_bf, g, batch, seq, tc=32):
    m, d = x.shape
    bw = d // RNN_BLOCKS
    chunk = pl.BlockSpec((batch, tc, d), lambda c: (0, c, 0))
    const2 = lambda c: (0, 0)
    const3 = lambda c: (0, 0, 0)
    vec = pl.BlockSpec((1, d), const2)
    r3 = lambda t: t.reshape(batch, seq, d)
    y, buf, hl = pl.pallas_call(
        _rglru_body,
        grid=(seq // tc,),
        in_specs=[chunk, chunk, chunk,
                  pl.BlockSpec((CONV_W, d), const2), vec,
                  pl.BlockSpec((RNN_BLOCKS, bw, bw), const3), vec,
                  pl.BlockSpec((RNN_BLOCKS, bw, bw), const3), vec, vec,
                  pl.BlockSpec((d, d), const2), vec],
        out_specs=[chunk,
                   pl.BlockSpec((batch, CONV_W - 1, d), const3),
                   pl.BlockSpec((batch, d), const2)],
        out_shape=[jax.ShapeDtypeStruct((batch, seq, d), F32),
                   jax.ShapeDtypeStruct((batch, CONV_W - 1, d), F32),
                   jax.ShapeDtypeStruct((batch, d), F32)],
        scratch_shapes=[pltpu.VMEM((batch, tc + 8, d), F32), pltpu.VMEM((batch, d), F32)],
        compiler_params=_cparams(("arbitrary",)),
        name="rglru_prompt",
    )(r3(gate), r3(xb), r3(x), cw, cb.reshape(1, d), wa_bf, ba.reshape(1, d), wx_bf, bx.reshape(1, d),
      lam.reshape(1, d), wo_bf, g.reshape(1, d))
    return y.reshape(m, d), buf, hl


def _rglru_step_body(gate_ref, xb_ref, x_ref, conv_ref, h0_ref, cw_ref, cb_ref, wa_ref, ba_ref,
                     wx_ref, bx_ref, lam_ref, wo_ref, g_ref, y_ref, h_ref):
    xc = cb_ref[...] + xb_ref[...] * cw_ref[CONV_W - 1:CONV_W, :]
    for i in range(CONV_W - 1):
        xc = xc + conv_ref[i] * cw_ref[i:i + 1, :]
    a, b = _lru_gates(xc, wa_ref, ba_ref, wx_ref, bx_ref, lam_ref)
    h = a * h0_ref[...] + b
    h_ref[...] = h
    y = (h * _gelu_tanh(gate_ref[...])).astype(BF16)
    y = jnp.dot(y, wo_ref[...], preferred_element_type=F32)
    y_ref[...] = x_ref[...] + _rms(y, g_ref[...])


def _rglru_step(gate, xb, x, conv_t, h0, cw, cb, wa_bf, ba, wx_bf, bx, lam, wo_bf, g):
    m, d = x.shape
    return pl.pallas_call(
        _rglru_step_body,
        out_shape=[jax.ShapeDtypeStruct((m, d), F32), jax.ShapeDtypeStruct((m, d), F32)],
        compiler_params=pltpu.CompilerParams(vmem_limit_bytes=VMEM_LIMIT),
        name="rglru_step",
    )(gate, xb, x, conv_t, h0, cw, cb.reshape(1, d), wa_bf, ba.reshape(1, d), wx_bf, bx.reshape(1, d),
      lam.reshape(1, d), wo_bf, g.reshape(1, d))


def _key_minor(cache):
    nd = cache.ndim
    t = jnp.transpose(cache, (0, 1) + tuple(range(3, nd)) + (2,))
    return t.reshape(t.shape[:2] + (-1,) + t.shape[-2:])


def _page_scores(kt_ref, qb):
    return jnp.sum(kt_ref[...] * qb, axis=1)


def _page_window(pt_ref, streams, li, n_pages, pp, n_rows, row, s0, nwin, prime, compute):
    nslots = streams[0][1].shape[0]
    nsteps = n_pages // pp
    ahead = nslots - 1
    assert nwin % nslots == 0 and nsteps % nwin == 0 and nsteps >= ahead

    def copies(r, s, slot):
        base = r * n_pages + s * pp
        return [pltpu.make_async_copy(hbm.at[li, pt_ref[base + j]], buf.at[slot, j], sem.at[slot, j])
                for hbm, buf, sem in streams for j in range(pp)]

    def start(r, s, slot):
        for c in copies(r, s, slot):
            c.start()

    @pl.when(prime)
    def _():
        for s_first in range(ahead):
            start(row, s_first, s_first)

    for u in range(nwin):
        s, slot = s0 + u, u % nslots
        tgt, tgt_slot = s + ahead, (slot + ahead) % nslots

        @pl.when(tgt < nsteps)
        def _():
            start(row, tgt, tgt_slot)

        @pl.when((tgt >= nsteps) & (row + 1 < n_rows))
        def _():
            start(row + 1, tgt - nsteps, tgt_slot)

        for c in copies(row, s, slot):
            c.wait()
        compute(s, slot)


def _page_stream(pt_ref, streams, li, n_pages, pp, compute):
    b = pl.program_id(0)
    nslots = streams[0][1].shape[0]

    def ring(i, carry):
        _page_window(pt_ref, streams, li, n_pages, pp, pl.num_programs(0), b, nslots * i, nslots,
                     (b == 0) & (i == 0), compute)
        return carry

    lax.fori_loop(0, n_pages // pp // nslots, ring, 0)


def _gate_row_step(s, slot, kbuf, qb_sc, g_sc, ppb):
    pp, page = kbuf.shape[1], kbuf.shape[-1]
    qb = qb_sc[...]
    for j in range(pp // ppb):
        tot = _page_scores(kbuf.at[slot, ppb * j], qb)
        for t in range(1, ppb):
            tot = tot + _page_scores(kbuf.at[slot, ppb * j + t], qb)
        g = jnp.sum(tot, axis=-1, keepdims=True) * (1.0 / (ppb * page))
        g_sc[s * (pp // ppb) + j] = jnp.broadcast_to(g, g_sc.shape[1:])


def _gate_row_finish(g_sc, sel_ref):
    g = g_sc[...]
    nblk = g.shape[0]
    bid = lax.broadcasted_iota(jnp.int32, g.shape, 0)
    lid = lax.broadcasted_iota(jnp.int32, sel_ref.shape, 1)
    out = jnp.zeros(sel_ref.shape, jnp.int32)
    for t in range(MOBA_TOPK):
        mx = jnp.max(g, axis=0, keepdims=True)
        idx = jnp.min(jnp.where(g == mx, bid, nblk), axis=0, keepdims=True)
        out = jnp.where(lid == t, idx[0], out)
        g = jnp.where(bid == idx, -jnp.inf, g)
    sel_ref[...] = out


def _moba_gate_body(pt_ref, q_ref, kt_hbm, sel_ref, kbuf, ksem, qb_sc, g_sc, *, li, n_pages, ppb):
    qb_sc[...] = jnp.broadcast_to(q_ref[...], qb_sc.shape)
    _page_stream(pt_ref, [(kt_hbm, kbuf, ksem)], li, n_pages, kbuf.shape[1],
                 lambda s, slot: _gate_row_step(s, slot, kbuf, qb_sc, g_sc, ppb))
    _gate_row_finish(g_sc, sel_ref)


def _mlp_gate_body(pt_ref, x_ref, g1_ref, wu_ref, wd_ref, g2_ref, q_ref, kt_hbm, y_ref, sel_ref,
                   xn_sc, acc_sc, kbuf, ksem, qb_sc, g_sc, *, li, n_pages, n_rows, nwin, ppb):
    pp = kbuf.shape[1]
    g = pl.program_id(0) * pl.num_programs(1) + pl.program_id(1)
    gpr = n_pages // pp // nwin
    row, part = g // gpr, g % gpr

    @pl.when(part == 0)
    def _():
        qb_sc[...] = jnp.broadcast_to(q_ref[...], qb_sc.shape)

    _page_window(pt_ref, [(kt_hbm, kbuf, ksem)], li, n_pages, pp, n_rows, row, part * nwin, nwin, g == 0,
                 lambda s, slot: _gate_row_step(s, slot, kbuf, qb_sc, g_sc, ppb))

    @pl.when(part == gpr - 1)
    def _():
        _gate_row_finish(g_sc, sel_ref)

    _mlp_body(x_ref, g1_ref, wu_ref, wd_ref, g2_ref, y_ref, xn_sc, acc_sc)


def _mlp_gate(x, g1, wu_bf, wd_bf, g2, q_col, kt, pt_flat, li, n_pages, tm, tf, pp=8):
    m, d = x.shape
    ff = wu_bf.shape[1]
    n_rows = q_col.shape[0]
    _, _, nh, hd, page = kt.shape
    ppb = MOBA_BLOCK // page
    nf = ff // tf
    n_grid = (m // tm) * nf
    nwin = n_rows * (n_pages // pp) // n_grid
    assert nwin * n_grid == n_rows * (n_pages // pp) and nwin % PAGE_SLOTS == 0
    gpr = n_pages // pp // nwin
    vec = pl.BlockSpec((1, d), lambda i, f, pt: (0, 0))
    return pl.pallas_call(
        functools.partial(_mlp_gate_body, li=li, n_pages=n_pages, n_rows=n_rows, nwin=nwin, ppb=ppb),
        grid_spec=pltpu.PrefetchScalarGridSpec(
            num_scalar_prefetch=1, grid=(m // tm, nf),
            in_specs=[pl.BlockSpec((tm, d), lambda i, f, pt: (i, 0)), vec,
                      pl.BlockSpec((d, tf), lambda i, f, pt: (0, f)),
                      pl.BlockSpec((tf, d), lambda i, f, pt: (f, 0)), vec,
                      pl.BlockSpec((None, nh, hd, 1), lambda i, f, pt: ((i * nf + f) // gpr, 0, 0, 0)),
                      pl.BlockSpec(memory_space=pl.ANY)],
            out_specs=[pl.BlockSpec((tm, d), lambda i, f, pt: (i, 0)),
                       pl.BlockSpec((None, nh, LANES), lambda i, f, pt: ((i * nf + f) // gpr, 0, 0))],
            scratch_shapes=[pltpu.VMEM((tm, d), BF16), pltpu.VMEM((tm, d), F32),
                            pltpu.VMEM((PAGE_SLOTS, pp, nh, hd, page), F32),
                            pltpu.SemaphoreType.DMA((PAGE_SLOTS, pp)),
                            pltpu.VMEM((nh, hd, page), F32),
                            pltpu.VMEM((n_pages // ppb, nh, LANES), F32)]),
        out_shape=[jax.ShapeDtypeStruct((m, d), F32), jax.ShapeDtypeStruct((n_rows, nh, LANES), jnp.int32)],
        compiler_params=_cparams(("arbitrary", "arbitrary")),
        name="mlp_gate",
    )(pt_flat, x, g1.reshape(1, d), wu_bf, wd_bf, g2.reshape(1, d), q_col, kt)


def _moba_gate(q_col, kt, pt_flat, li, batch, n_pages, pp=16):
    _, _, nh, hd, page = kt.shape
    ppb = MOBA_BLOCK // page
    nblk = n_pages // ppb
    return pl.pallas_call(
        functools.partial(_moba_gate_body, li=li, n_pages=n_pages, ppb=ppb),
        grid_spec=pltpu.PrefetchScalarGridSpec(
            num_scalar_prefetch=1, grid=(batch,),
            in_specs=[pl.BlockSpec((None, nh, hd, 1), lambda b, pt: (b, 0, 0, 0)),
                      pl.BlockSpec(memory_space=pl.ANY)],
            out_specs=pl.BlockSpec((None, nh, LANES), lambda b, pt: (b, 0, 0)),
            scratch_shapes=[pltpu.VMEM((PAGE_SLOTS, pp, nh, hd, page), F32),
                            pltpu.SemaphoreType.DMA((PAGE_SLOTS, pp)),
                            pltpu.VMEM((nh, hd, page), F32),
                            pltpu.VMEM((nblk, nh, LANES), F32)]),
        out_shape=jax.ShapeDtypeStruct((batch, nh, LANES), jnp.int32),
        compiler_params=_cparams(("arbitrary",)),
        name="moba_gate",
    )(pt_flat, q_col, kt)


def _moba_step_body(sel_ref, pt_ref, q_ref, kn_ref, vn_ref, kt_hbm, vt_hbm, o_ref, kbuf, vbuf, ksem, vsem,
                    *, li, n_pages, ppb):
    b = pl.program_id(0)
    _, nh, nsl = kbuf.shape[:3]
    slot = b % 2

    def copies(row, slot):
        out = []
        for h in range(nh):
            for t in range(MOBA_TOPK):
                blk = sel_ref[(row * nh + h) * MOBA_TOPK + t]
                for j in range(ppb):
                    page = pt_ref[row * n_pages + blk * ppb + j]
                    i = t * ppb + j
                    out.append(pltpu.make_async_copy(kt_hbm.at[li, page, h], kbuf.at[slot, h, i],
                                                     ksem.at[slot, h, i]))
                    out.append(pltpu.make_async_copy(vt_hbm.at[li, page, h], vbuf.at[slot, h, i],
                                                     vsem.at[slot, h, i]))
        return out

    @pl.when(b == 0)
    def _():
        for c in copies(0, 0):
            c.start()

    @pl.when(b + 1 < pl.num_programs(0))
    def _():
        for c in copies(b + 1, 1 - slot):
            c.start()

    for c in copies(b, slot):
        c.wait()

    for h in range(nh):
        q8 = jnp.broadcast_to(q_ref[h] * SCALE, (8, q_ref.shape[-1]))
        kt = jnp.concatenate([kbuf[slot, h, i] for i in range(nsl)], axis=1).astype(BF16)
        vt = jnp.concatenate([vbuf[slot, h, i] for i in range(nsl)], axis=1).astype(BF16)
        s = jnp.dot(q8.astype(BF16), kt, preferred_element_type=F32)
        s_self = jnp.sum(q8 * kn_ref[h], axis=-1, keepdims=True)
        m = jnp.maximum(jnp.max(s, axis=-1, keepdims=True), s_self)
        p = jnp.exp(s - m)
        p_self = jnp.exp(s_self - m)
        l = jnp.sum(p, axis=-1, keepdims=True) + p_self
        pv = lax.dot_general(p.astype(BF16), vt, NT_DIMS, preferred_element_type=F32)
        o_ref[h] = (pv + p_self * vn_ref[h]) / l


def _moba_step(q4, kn4, vn4, kt, vt, sel_flat, pt_flat, li, batch, n_pages):
    _, _, nh, hd, page = kt.shape
    ppb = MOBA_BLOCK // page
    nsl = MOBA_TOPK * ppb
    row = pl.BlockSpec((None, nh, 1, hd), lambda b, sel, pt: (b, 0, 0, 0))
    hbm = pl.BlockSpec(memory_space=pl.ANY)
    return pl.pallas_call(
        functools.partial(_moba_step_body, li=li, n_pages=n_pages, ppb=ppb),
        grid_spec=pltpu.PrefetchScalarGridSpec(
            num_scalar_prefetch=2, grid=(batch,),
            in_specs=[row, row, row, hbm, hbm],
            out_specs=pl.BlockSpec((None, nh, 8, hd), lambda b, sel, pt: (b, 0, 0, 0)),
            scratch_shapes=[pltpu.VMEM((2, nh, nsl, hd, page), F32), pltpu.VMEM((2, nh, nsl, hd, page), F32),
                            pltpu.SemaphoreType.DMA((2, nh, nsl)), pltpu.SemaphoreType.DMA((2, nh, nsl))]),
        out_shape=jax.ShapeDtypeStruct((batch, nh, 8, hd), F32),
        compiler_params=_cparams(("arbitrary",)),
        name="moba_step",
    )(sel_flat, pt_flat, q4, kn4, vn4, kt, vt)


def _diff_row_init(q_ref, qb_sc, m_sc, l_sc, acc_sc):
    qb_sc[...] = jnp.broadcast_to(q_ref[...] * SCALE, qb_sc.shape)
    m_sc[...] = jnp.full(m_sc.shape, NEG, F32)
    l_sc[...] = jnp.zeros_like(l_sc)
    acc_sc[...] = jnp.zeros_like(acc_sc)


def _diff_row_step(slot, kbuf, vbuf, qb_sc, m_sc, l_sc, acc_sc):
    pp, page = kbuf.shape[1], kbuf.shape[-1]
    qb = qb_sc[...]
    sc = jnp.concatenate([_page_scores(kbuf.at[slot, j], qb) for j in range(pp)], axis=1)
    m_old = m_sc[...]
    m_new = jnp.maximum(m_old, jnp.max(sc, axis=-1, keepdims=True))
    alpha = jnp.exp(m_old - m_new)
    p = jnp.exp(sc - m_new)
    l_sc[...] = alpha * l_sc[...] + jnp.sum(p, axis=-1, keepdims=True)
    m_sc[...] = m_new
    pb = p.astype(BF16)
    for h in range(DIFF_HEADS):
        vh = jnp.concatenate([vbuf[slot, j, pl.ds(h, page, stride=DIFF_HEADS), :] for j in range(pp)],
                             axis=0)
        acc_sc[h] = alpha * acc_sc[h] + jnp.dot(pb, vh.astype(BF16), preferred_element_type=F32)


def _diff_row_finish(q_ref, kn_ref, vn_ref, lam_ref, gsub_ref, o_ref, m_sc, l_sc, acc_sc, lam_init):
    s_self = jnp.sum(q_ref[...] * SCALE * kn_ref[...], axis=1)
    m_old = m_sc[...]
    m_new = jnp.maximum(m_old, s_self)
    alpha = jnp.exp(m_old - m_new)
    p_self = jnp.exp(s_self - m_new)
    l = alpha * l_sc[...] + p_self
    lam = _diff_lambda(lam_ref, lam_init)
    for h in range(DIFF_HEADS):
        vn = vn_ref[:, h * DIFF_DIM:(h + 1) * DIFF_DIM]
        o = (alpha * acc_sc[h] + p_self * vn) / l
        od = o[2 * h:2 * h + 1, :] - lam * o[2 * h + 1:2 * h + 2, :]
        o_ref[:, h * DIFF_DIM:(h + 1) * DIFF_DIM] = _rms(od, gsub_ref[...]) * (1.0 - lam_init)


def _mlp_diff_body(pt_ref, x_ref, g1_ref, wu_ref, wd_ref, g2_ref, q_ref, kn_ref, vn_ref, lam_ref, gsub_ref,
                   kt_hbm, v_hbm, y_ref, o_ref, xn_sc, acc_sc, kbuf, vbuf, ksem, vsem, qb_sc, m_sc, l_sc,
                   dacc_sc, *, li, n_pages, n_rows, nwin, lam_init):
    pp = kbuf.shape[1]
    g = pl.program_id(0) * pl.num_programs(1) + pl.program_id(1)
    gpr = n_pages // pp // nwin
    row, part = g // gpr, g % gpr
    state = (qb_sc, m_sc, l_sc, dacc_sc)

    @pl.when(part == 0)
    def _():
        _diff_row_init(q_ref, *state)

    _page_window(pt_ref, [(kt_hbm, kbuf, ksem), (v_hbm, vbuf, vsem)], li, n_pages, pp, n_rows, row,
                 part * nwin, nwin, g == 0, lambda s, slot: _diff_row_step(slot, kbuf, vbuf, *state))

    @pl.when(part == gpr - 1)
    def _():
        _diff_row_finish(q_ref, kn_ref, vn_ref, lam_ref, gsub_ref, o_ref, m_sc, l_sc, dacc_sc, lam_init)

    _mlp_body(x_ref, g1_ref, wu_ref, wd_ref, g2_ref, y_ref, xn_sc, acc_sc)


def _mlp_diff(x, g1, wu_bf, wd_bf, g2, q_col, kn_col, vn3, lam4, gsub, kt, v2, pt_flat, li, n_pages,
              lam_init, tm, tf, pp=8):
    m, d = x.shape
    ff = wu_bf.shape[1]
    n_rows, nrow, hd, _ = q_col.shape
    page = kt.shape[-1]
    vrows, dv = v2.shape[2:]
    w = vn3.shape[-1]
    nf = ff // tf
    n_grid = (m // tm) * nf
    nwin = n_rows * (n_pages // pp) // n_grid
    assert nwin * n_grid == n_rows * (n_pages // pp) and nwin % PAGE_SLOTS == 0
    gpr = n_pages // pp // nwin
    srow = lambda i, f, pt: ((i * nf + f) // gpr, 0, 0)
    col = pl.BlockSpec((None, nrow, hd, 1), lambda i, f, pt: ((i * nf + f) // gpr, 0, 0, 0))
    vec = pl.BlockSpec((1, d), lambda i, f, pt: (0, 0))
    hbm = pl.BlockSpec(memory_space=pl.ANY)
    return pl.pallas_call(
        functools.partial(_mlp_diff_body, li=li, n_pages=n_pages, n_rows=n_rows, nwin=nwin, lam_init=lam_init),
        grid_spec=pltpu.PrefetchScalarGridSpec(
            num_scalar_prefetch=1, grid=(m // tm, nf),
            in_specs=[pl.BlockSpec((tm, d), lambda i, f, pt: (i, 0)), vec,
                      pl.BlockSpec((d, tf), lambda i, f, pt: (0, f)),
                      pl.BlockSpec((tf, d), lambda i, f, pt: (f, 0)), vec,
                      col, col, pl.BlockSpec((None, 1, w), srow),
                      pl.BlockSpec((4, HEAD_DIM), lambda i, f, pt: (0, 0)),
                      pl.BlockSpec((1, DIFF_DIM), lambda i, f, pt: (0, 0)), hbm, hbm],
            out_specs=[pl.BlockSpec((tm, d), lambda i, f, pt: (i, 0)), pl.BlockSpec((None, 1, w), srow)],
            scratch_shapes=[pltpu.VMEM((tm, d), BF16), pltpu.VMEM((tm, d), F32),
                            pltpu.VMEM((PAGE_SLOTS, pp, nrow, hd, page), F32),
                            pltpu.VMEM((PAGE_SLOTS, pp, vrows, dv), F32),
                            pltpu.SemaphoreType.DMA((PAGE_SLOTS, pp)), pltpu.SemaphoreType.DMA((PAGE_SLOTS, pp)),
                            pltpu.VMEM((nrow, hd, page), F32), pltpu.VMEM((nrow, 1), F32),
                            pltpu.VMEM((nrow, 1), F32), pltpu.VMEM((DIFF_HEADS, nrow, dv), F32)]),
        out_shape=[jax.ShapeDtypeStruct((m, d), F32), jax.ShapeDtypeStruct((n_rows, 1, w), F32)],
        compiler_params=_cparams(("arbitrary", "arbitrary")),
        name="mlp_diff",
    )(pt_flat, x, g1.reshape(1, d), wu_bf, wd_bf, g2.reshape(1, d), q_col, kn_col, vn3, lam4,
      gsub.reshape(1, DIFF_DIM), kt, v2)


def kernel(x_prompt, x_sample, cache_moba_k, cache_moba_v, cache_diff_k, cache_diff_v, state_conv, state_rnn, page_table, attn_g_pre, attn_w_in, diff_lambda_q1, diff_lambda_k1, diff_lambda_q2, diff_lambda_k2, diff_g_sub, attn_w_out, attn_g_post, rnn_g_pre, rnn_w_in, rnn_conv_w, rnn_conv_b, rnn_w_a, rnn_b_a, rnn_w_x, rnn_b_x, rnn_lambda, rnn_w_out, rnn_g_post, mlp_g_pre, mlp_w_up, mlp_w_down, mlp_g_post):
    batch, seq, d = x_prompt.shape
    dec_batch, dec_seq, _ = x_sample.shape
    assert dec_seq == 1
    depth = mlp_w_up.shape[0]
    na, n_pool, page = cache_moba_k.shape[:3]
    n_pages = page_table.shape[1]
    past_len = n_pages * page
    mw = MOBA_HEADS * HEAD_DIM
    dw = DIFF_HEADS * DIFF_DIM

    xp = x_prompt.reshape(batch * seq, d)
    xs = x_sample.reshape(dec_batch, d)
    tab_p = _rope_tables(jnp.arange(seq, dtype=jnp.int32))
    tab_s = _rope_tables(jnp.full((dec_batch,), past_len, jnp.int32))
    pt_flat = page_table.reshape(-1)
    cmk_t = _key_minor(cache_moba_k)
    cmv_t = _key_minor(cache_moba_v)
    cdk_t = _key_minor(cache_diff_k)
    cdv2 = cache_diff_v.reshape(na, n_pool, page * DIFF_HEADS, DIFF_DIM)

    tm_p = 512
    attn_rope = (True, True, False, True, True, False)
    outs = {k: [] for k in ("pmk", "pmv", "pdk", "pdv", "pconv", "prnn",
                            "smk", "smv", "sdk", "sdv", "sconv", "srnn")}
    pending = None
    for layer in range(depth):
        li = layer // 2
        wu = mlp_w_up[layer].astype(BF16)
        wd = mlp_w_down[layer].astype(BF16)
        if layer % 2 == 0:
            lam_init = 0.8 - 0.6 * math.exp(-0.3 * layer)
            w_in = attn_w_in[li].astype(BF16)
            w_out = attn_w_out[li].astype(BF16)
            lam4 = jnp.stack([diff_lambda_q1[li], diff_lambda_k1[li], diff_lambda_q2[li], diff_lambda_k2[li]])
            mq, mk, mv, dq, dk, dv, mk_t, mv_t, dk_t = _in_proj(
                xp, attn_g_pre[li], w_in, tab_p, attn_rope, tm_p,
                t_flags=(False, True, True, False, True, False))
            o_m = _moba_prompt(mq, mk, mv, batch, seq)
            o_d = _diff_prompt(dq, dk, dv, lam4, diff_g_sub[li], batch, seq, lam_init)
            xp = _attn_out(o_m, o_d, w_out, attn_g_post[li], xp, tm_p)
            outs["pmk"].append(jnp.moveaxis(mk_t.reshape(batch, MOBA_HEADS, HEAD_DIM, seq), -1, 1))
            outs["pmv"].append(jnp.moveaxis(mv_t.reshape(batch, MOBA_HEADS, HEAD_DIM, seq), -1, 1))
            outs["pdk"].append(jnp.moveaxis(dk_t.reshape(batch, DIFF_HEADS, 2, HEAD_DIM, seq), -1, 1))
            outs["pdv"].append(dv.reshape(batch, seq, DIFF_HEADS, DIFF_DIM))
            mq, mk, mv, dq, dk, dv = _in_proj(xs, attn_g_pre[li], w_in, tab_s, attn_rope, dec_batch)
            col = lambda t: t.reshape(dec_batch, -1, HEAD_DIM, 1)
            xp, o_d = _mlp_diff(xp, mlp_g_pre[layer], wu, wd, mlp_g_post[layer],
                                col(dq), col(dk), dv.reshape(dec_batch, 1, dw), lam4, diff_g_sub[li],
                                cdk_t, cdv2, pt_flat, li, n_pages, lam_init, 1024, 512)
            outs["smk"].append(mk.reshape(dec_batch, 1, MOBA_HEADS, HEAD_DIM))
            outs["smv"].append(mv.reshape(dec_batch, 1, MOBA_HEADS, HEAD_DIM))
            outs["sdk"].append(dk.reshape(dec_batch, 1, DIFF_HEADS, 2, HEAD_DIM))
            outs["sdv"].append(dv.reshape(dec_batch, 1, DIFF_HEADS, DIFF_DIM))

            def finish_sample(sel, xs, li=li, layer=layer, mq=mq, mk=mk, mv=mv, o_d=o_d, w_out=w_out,
                              wu=wu, wd=wd):
                hrow = lambda t: t.reshape(dec_batch, MOBA_HEADS, 1, HEAD_DIM)
                sel = sel[:, :, :MOBA_TOPK].reshape(-1)
                o_m = _moba_step(hrow(mq), hrow(mk), hrow(mv), cmk_t, cmv_t, sel, pt_flat, li,
                                 dec_batch, n_pages)[:, :, 0, :]
                xs = _attn_out(o_m.reshape(dec_batch, mw), o_d.reshape(dec_batch, dw), w_out,
                               attn_g_post[li], xs, dec_batch)
                return _mlp(xs, mlp_g_pre[layer], wu, wd, mlp_g_post[layer], dec_batch, 512)

            if layer + 1 < depth:
                pending = (col(mq), li, finish_sample)
            else:
                xs = finish_sample(_moba_gate(col(mq), cmk_t, pt_flat, li, dec_batch, n_pages), xs)
        else:
            w_in = rnn_w_in[li].astype(BF16)
            wts = (rnn_conv_w[li], rnn_conv_b[li], rnn_w_a[li].astype(BF16), rnn_b_a[li].reshape(-1),
                   rnn_w_x[li].astype(BF16), rnn_b_x[li].reshape(-1), rnn_lambda[li],
                   rnn_w_out[li].astype(BF16), rnn_g_post[li])
            gate, xb = _in_proj(xp, rnn_g_pre[li], w_in, tab_p, (False, False), tm_p)
            xp, cbuf, hlast = _rglru_prompt(gate, xb, xp, *wts, batch, seq)
            outs["pconv"].append(cbuf)
            outs["prnn"].append(hlast.reshape(batch, d))
            if pending is None:
                xp = _mlp(xp, mlp_g_pre[layer], wu, wd, mlp_g_post[layer], 1024, 1024)
            else:
                q_gate, li_gate, finish_sample = pending
                xp, sel = _mlp_gate(xp, mlp_g_pre[layer], wu, wd, mlp_g_post[layer], q_gate, cmk_t,
                                    pt_flat, li_gate, n_pages, 1024, 1024)
                xs = finish_sample(sel, xs)
                pending = None
            gate, xb = _in_proj(xs, rnn_g_pre[li], w_in, tab_s, (False, False), dec_batch)
            conv_t = jnp.swapaxes(state_conv[li], 0, 1)
            xs, hnew = _rglru_step(gate, xb, xs, conv_t, state_rnn[li], *wts)
            outs["sconv"].append(jnp.concatenate([state_conv[li][:, 1:], xb[:, None, :]], axis=1))
            outs["srnn"].append(hnew)
            xs = _mlp(xs, mlp_g_pre[layer], wu, wd, mlp_g_post[layer], dec_batch, 512)

    st = lambda k: jnp.stack(outs[k])
    return (xp.reshape(batch, seq, d), xs.reshape(dec_batch, 1, d),
            st("pmk"), st("pmv"), st("pdk"), st("pdv"), st("pconv"), st("prnn"),
            st("smk"), st("smv"), st("sdk"), st("sdv"), st("sconv"), st("srnn"))
```

```python
import functools
import math

import jax
import jax.numpy as jnp
from jax import lax
from jax.experimental import pallas as pl
from jax.experimental.pallas import tpu as pltpu

F32 = jnp.float32
BF16 = jnp.bfloat16
HIGHEST = lax.Precision.HIGHEST

HEAD_DIM = 64
ROT_DIM = HEAD_DIM // 4
ROPE_THETA = 500000.0
MOBA_HEADS = 8
MOBA_BLOCK = 256
MOBA_TOPK = 3
DIFF_HEADS = 4
DIFF_DIM = 2 * HEAD_DIM
RNN_BLOCKS = 4
CONV_W = 4
LRU_C = 8.0
EPS = 1e-6
LANES = 128
NEG = -1e30
SCALE = HEAD_DIM ** -0.5
SCALE_LOG2E = SCALE * math.log2(math.e)
NT_DIMS = (((1,), (1,)), ((), ()))
ATTN_CHAINS = 4
PAGE_SLOTS = 4
VMEM_LIMIT = 52 * 1024 * 1024


def _cparams(sem):
    return pltpu.CompilerParams(dimension_semantics=sem, vmem_limit_bytes=VMEM_LIMIT)


def _rms(x, g):
    ms = jnp.mean(x * x, axis=-1, keepdims=True)
    return x * lax.rsqrt(ms + EPS) * g


def _rope_tables(pos):
    half = ROT_DIM // 2
    inv = jnp.exp(-math.log(ROPE_THETA) * jnp.arange(half, dtype=F32) * (2.0 / ROT_DIM))
    ang = pos.astype(F32)[:, None] * inv[None, :]
    cos, sin = jnp.cos(ang), jnp.sin(ang)
    n = pos.shape[0]
    pad = jnp.zeros((n, HEAD_DIM - ROT_DIM), F32)
    c64 = jnp.concatenate([cos, cos, pad + 1.0], axis=1)
    sa64 = jnp.concatenate([-sin, jnp.zeros((n, half), F32), pad], axis=1)
    sb64 = jnp.concatenate([jnp.zeros((n, half), F32), sin, pad], axis=1)
    rep = LANES // HEAD_DIM
    return tuple(jnp.tile(t, (1, rep)) for t in (c64, sa64, sb64))


def _in_proj_body(x_ref, g_ref, w_ref, c_ref, sa_ref, sb_ref, *outs, rope_flags, t_flags, tn):
    n_out = len(rope_flags)
    outs_t = iter(outs[n_out:])
    xn = _rms(x_ref[...], g_ref[...]).astype(BF16)
    for j, out_ref in enumerate(outs[:n_out]):
        u = jnp.dot(xn, w_ref[:, j * tn:(j + 1) * tn], preferred_element_type=F32)
        if rope_flags[j]:
            c, sa, sb = c_ref[...], sa_ref[...], sb_ref[...]
            u = jnp.concatenate(
                [uk * c + pltpu.roll(uk, LANES - ROT_DIM // 2, 1) * sa + pltpu.roll(uk, ROT_DIM // 2, 1) * sb
                 for uk in (u[:, k * LANES:(k + 1) * LANES] for k in range(tn // LANES))], axis=1)
        out_ref[...] = u
        if t_flags[j]:
            next(outs_t)[...] = u.T


def _in_proj(x, g, w_bf, tables, rope_flags, tm, t_flags=None):
    m, d = x.shape
    n_out = len(rope_flags)
    t_flags = t_flags or (False,) * n_out
    n = w_bf.shape[1]
    tn = n // n_out
    seq = tables[0].shape[0]
    npos = seq // tm
    tab_spec = pl.BlockSpec((tm, LANES), lambda i: (i % npos, 0))
    n_t = sum(t_flags)
    return pl.pallas_call(
        functools.partial(_in_proj_body, rope_flags=rope_flags, t_flags=t_flags, tn=tn),
        grid=(m // tm,),
        in_specs=[pl.BlockSpec((tm, d), lambda i: (i, 0)),
                  pl.BlockSpec((1, d), lambda i: (0, 0)),
                  pl.BlockSpec((d, n), lambda i: (0, 0)),
                  tab_spec, tab_spec, tab_spec],
        out_specs=[pl.BlockSpec((tm, tn), lambda i: (i, 0))] * n_out
        + [pl.BlockSpec((None, tn, tm), lambda i: (i // npos, 0, i % npos))] * n_t,
        out_shape=[jax.ShapeDtypeStruct((m, tn), F32)] * n_out
        + [jax.ShapeDtypeStruct((m // seq, tn, seq), F32)] * n_t,
        compiler_params=_cparams(("arbitrary",)),
        name="in_proj",
    )(x, g.reshape(1, d), w_bf, *tables)


def _attn_out_body(om_ref, od_ref, w_ref, g_ref, x_ref, y_ref):
    km = om_ref.shape[1]
    o = jnp.dot(om_ref[...].astype(BF16), w_ref[:km, :], preferred_element_type=F32)
    o = o + jnp.dot(od_ref[...].astype(BF16), w_ref[km:, :], preferred_element_type=F32)
    y_ref[...] = x_ref[...] + _rms(o, g_ref[...])


def _attn_out(om, od, w_bf, g, x, tm):
    m, d = x.shape
    km, kd = om.shape[1], od.shape[1]
    return pl.pallas_call(
        _attn_out_body,
        grid=(m // tm,),
        in_specs=[pl.BlockSpec((tm, km), lambda i: (i, 0)),
                  pl.BlockSpec((tm, kd), lambda i: (i, 0)),
                  pl.BlockSpec((km + kd, d), lambda i: (0, 0)),
                  pl.BlockSpec((1, d), lambda i: (0, 0)),
                  pl.BlockSpec((tm, d), lambda i: (i, 0))],
        out_specs=pl.BlockSpec((tm, d), lambda i: (i, 0)),
        out_shape=jax.ShapeDtypeStruct((m, d), F32),
        compiler_params=_cparams(("arbitrary",)),
        name="attn_out",
    )(om, od, w_bf, g.reshape(1, d), x)


def _mlp_up(x_ref, g1_ref, wu_ref, xn_sc, acc_sc):
    @pl.when(pl.program_id(1) == 0)
    def _():
        xn_sc[...] = _rms(x_ref[...], g1_ref[...]).astype(BF16)
        acc_sc[...] = jnp.zeros_like(acc_sc)

    h = jnp.dot(xn_sc[...], wu_ref[...], preferred_element_type=F32)
    return jnp.square(jnp.maximum(h, 0.0)).astype(BF16)


def _mlp_down(h, wd_ref, x_ref, g2_ref, y_ref, acc_sc):
    acc_sc[...] += jnp.dot(h, wd_ref[...], preferred_element_type=F32)

    @pl.when(pl.program_id(1) == pl.num_programs(1) - 1)
    def _():
        y_ref[...] = x_ref[...] + _rms(acc_sc[...], g2_ref[...])


def _mlp_body(x_ref, g1_ref, wu_ref, wd_ref, g2_ref, y_ref, xn_sc, acc_sc):
    _mlp_down(_mlp_up(x_ref, g1_ref, wu_ref, xn_sc, acc_sc), wd_ref, x_ref, g2_ref, y_ref, acc_sc)


def _mlp(x, g1, wu_bf, wd_bf, g2, tm, tf):
    m, d = x.shape
    ff = wu_bf.shape[1]
    return pl.pallas_call(
        _mlp_body,
        grid=(m // tm, ff // tf),
        in_specs=[pl.BlockSpec((tm, d), lambda i, f: (i, 0)),
                  pl.BlockSpec((1, d), lambda i, f: (0, 0)),
                  pl.BlockSpec((d, tf), lambda i, f: (0, f)),
                  pl.BlockSpec((tf, d), lambda i, f: (f, 0)),
                  pl.BlockSpec((1, d), lambda i, f: (0, 0))],
        out_specs=pl.BlockSpec((tm, d), lambda i, f: (i, 0)),
        out_shape=jax.ShapeDtypeStruct((m, d), F32),
        scratch_shapes=[pltpu.VMEM((tm, d), BF16), pltpu.VMEM((tm, d), F32)],
        compiler_params=_cparams(("arbitrary", "arbitrary")),
        name="mlp",
    )(x, g1.reshape(1, d), wu_bf, wd_bf, g2.reshape(1, d))


def _causal_mask(st):
    kidx = lax.broadcasted_iota(jnp.int32, st.shape, 0)
    qidx = lax.broadcasted_iota(jnp.int32, st.shape, 1)
    return jnp.where(kidx <= qidx, st, NEG)


def _tile_attention(chains, nblk, blk):
    nk = nblk * blk
    for kh, qs, _, s_ref, _, _ in chains:
        s_ref[0:nk, :] = lax.dot_general(kh, qs, NT_DIMS, preferred_element_type=F32)
    ms = []
    for _, _, _, s_ref, _, bias in chains:
        m = None
        for n in range(nblk):
            rows = slice(n * blk, (n + 1) * blk)
            piece = s_ref[rows, :]
            if n == nblk - 1:
                piece = _causal_mask(piece)
                s_ref[rows, :] = piece
            elif bias is not None:
                piece = piece + bias[n:n + 1, :]
                s_ref[rows, :] = piece
            pm = jnp.max(piece, axis=0, keepdims=True)
            m = pm if m is None else jnp.maximum(m, pm)
        ms.append(m)
    ls = []
    for (_, _, _, s_ref, p_ref, _), m in zip(chains, ms):
        l = None
        for n in range(nblk):
            rows = slice(n * blk, (n + 1) * blk)
            p = jnp.exp2(s_ref[rows, :] - m)
            ps = jnp.sum(p, axis=0, keepdims=True)
            l = ps if l is None else l + ps
            p_ref[rows, :] = p.astype(BF16)
        ls.append(l)
    accs = [jnp.dot(vt, p_ref[0:nk, :], preferred_element_type=F32) for _, _, vt, _, p_ref, _ in chains]
    return [acc / l for acc, l in zip(accs, ls)]


def _moba_select_bias(gt, k_eff):
    npast = gt.shape[0]
    rid = lax.broadcasted_iota(jnp.int32, gt.shape, 0)
    rank = jnp.zeros(gt.shape, jnp.int32)
    for mm in range(npast):
        gm = gt[mm:mm + 1, :]
        beats = (gm > gt) | ((gm == gt) & (rid > mm))
        rank = rank + beats.astype(jnp.int32)
    return jnp.where(rank < k_eff, 0.0, NEG)


def _moba_body(q_ref, k_ref, v_ref, o_ref, kmean_sc, kh_sc, vt_sc, qf_sc, qs_sc, s_sc, p_sc, ot_sc, *, nb):
    qi = pl.program_id(1)
    blk = MOBA_BLOCK
    hd = HEAD_DIM
    k_eff = min(MOBA_TOPK, nb)

    @pl.when(qi == 0)
    def _prep():
        for n in range(nb):
            rows = slice(n * blk, (n + 1) * blk)
            kblk = k_ref[rows, :]
            kmean = jnp.sum(kblk, axis=0, keepdims=True) * (1.0 / blk)
            vtb = v_ref[rows, :].T
            for h in range(MOBA_HEADS):
                hs = slice(h * hd, (h + 1) * hd)
                kmean_sc[h, n:n + 1, :] = kmean[:, hs]
                kh_sc[h, rows, :] = kblk[:, hs].astype(BF16)
                vt_sc[h, :, rows] = vtb[hs, :].astype(BF16)

    for h in range(MOBA_HEADS):
        qh = q_ref[:, h * hd:(h + 1) * hd]
        qf_sc[h] = qh
        qs_sc[h] = (qh * SCALE_LOG2E).astype(BF16)

    for qq in range(nb):
        @pl.when(qi == qq)
        def _tile(qq=qq):
            nk = (qq + 1) * blk

            def head_group(hg, carry):
                chains = []
                for i in range(ATTN_CHAINS):
                    h = ATTN_CHAINS * hg + i
                    bias = None
                    if qq > k_eff:
                        gt = lax.dot_general(kmean_sc[h, 0:qq, :], qf_sc[h], NT_DIMS, precision=HIGHEST,
                                             preferred_element_type=F32)
                        bias = _moba_select_bias(gt, k_eff)
                    chains.append((kh_sc[h, 0:nk, :], qs_sc[h], vt_sc[h, :, 0:nk], s_sc.at[i],
                                   p_sc.at[i], bias))
                o = _tile_attention(chains, qq + 1, blk)
                rows = ATTN_CHAINS * hd
                ot_sc[pl.ds(pl.multiple_of(hg * rows, rows), rows), :] = jnp.concatenate(o, axis=0)
                return carry

            lax.fori_loop(0, MOBA_HEADS // ATTN_CHAINS, head_group, 0)

    o_ref[...] = ot_sc[...].T


def _moba_prompt(q, k, v, batch, seq):
    w = q.shape[1]
    blk = MOBA_BLOCK
    nb = seq // blk
    return pl.pallas_call(
        functools.partial(_moba_body, nb=nb),
        grid=(batch, nb),
        in_specs=[pl.BlockSpec((blk, w), lambda b, i: (b * nb + i, 0)),
                  pl.BlockSpec((seq, w), lambda b, i: (b, 0)),
                  pl.BlockSpec((seq, w), lambda b, i: (b, 0))],
        out_specs=pl.BlockSpec((blk, w), lambda b, i: (b * nb + i, 0)),
        out_shape=jax.ShapeDtypeStruct(q.shape, F32),
        scratch_shapes=[pltpu.VMEM((MOBA_HEADS, nb, HEAD_DIM), F32),
                        pltpu.VMEM((MOBA_HEADS, seq, HEAD_DIM), BF16),
                        pltpu.VMEM((MOBA_HEADS, HEAD_DIM, seq), BF16),
                        pltpu.VMEM((MOBA_HEADS, blk, HEAD_DIM), F32),
                        pltpu.VMEM((MOBA_HEADS, blk, HEAD_DIM), BF16),
                        pltpu.VMEM((ATTN_CHAINS, seq, blk), F32),
                        pltpu.VMEM((ATTN_CHAINS, seq, blk), BF16),
                        pltpu.VMEM((w, blk), F32)],
        compiler_params=_cparams(("arbitrary", "arbitrary")),
        name="moba_prompt",
    )(q, k, v)


def _diff_lambda(lam_ref, lam_init):
    lv = lam_ref[...]
    s1 = jnp.sum(lv[0:1, :] * lv[1:2, :], axis=-1, keepdims=True)
    s2 = jnp.sum(lv[2:3, :] * lv[3:4, :], axis=-1, keepdims=True)
    return jnp.exp(s1) - jnp.exp(s2) + lam_init


def _diff_body(q_ref, k_ref, v_ref, lam_ref, gsub_ref, o_ref, kh_sc, vt_sc, qs_sc, s_sc, p_sc, ot_sc,
               *, nb, tq, lam_init):
    qi = pl.program_id(1)
    hd = HEAD_DIM
    nrow = 2 * DIFF_HEADS

    @pl.when(qi == 0)
    def _prep():
        for n in range(nb):
            rows = slice(n * tq, (n + 1) * tq)
            kblk = k_ref[rows, :]
            vtb = v_ref[rows, :].T
            for r in range(nrow):
                kh_sc[r, rows, :] = kblk[:, r * hd:(r + 1) * hd].astype(BF16)
            for h in range(DIFF_HEADS):
                vt_sc[h, :, rows] = vtb[h * DIFF_DIM:(h + 1) * DIFF_DIM, :].astype(BF16)

    for r in range(nrow):
        qs_sc[r] = (q_ref[:, r * hd:(r + 1) * hd] * SCALE_LOG2E).astype(BF16)
    lam = _diff_lambda(lam_ref, lam_init)

    for qq in range(nb):
        @pl.when(qi == qq)
        def _tile(qq=qq):
            nk = (qq + 1) * tq

            hpi = ATTN_CHAINS // 2

            def head_group(hg, carry):
                chains = []
                for i in range(ATTN_CHAINS):
                    r = ATTN_CHAINS * hg + i
                    chains.append((kh_sc[r, 0:nk, :], qs_sc[r], vt_sc[hpi * hg + i // 2, :, 0:nk],
                                   s_sc.at[i], p_sc.at[i], None))
                o = _tile_attention(chains, qq + 1, tq)
                for j in range(hpi):
                    od = o[2 * j] - lam * o[2 * j + 1]
                    ms = jnp.mean(od * od, axis=0, keepdims=True)
                    od = od * lax.rsqrt(ms + EPS) * gsub_ref[...]
                    row0 = pl.multiple_of((hpi * hg + j) * DIFF_DIM, DIFF_DIM)
                    ot_sc[pl.ds(row0, DIFF_DIM), :] = od * (1.0 - lam_init)
                return carry

            lax.fori_loop(0, 2 * DIFF_HEADS // ATTN_CHAINS, head_group, 0)

    o_ref[...] = ot_sc[...].T


def _diff_prompt(q, k, v, lam4, gsub, batch, seq, lam_init, tq=256):
    w = q.shape[1]
    nb = seq // tq
    return pl.pallas_call(
        functools.partial(_diff_body, nb=nb, tq=tq, lam_init=lam_init),
        grid=(batch, nb),
        in_specs=[pl.BlockSpec((tq, w), lambda b, i: (b * nb + i, 0)),
                  pl.BlockSpec((seq, w), lambda b, i: (b, 0)),
                  pl.BlockSpec((seq, w), lambda b, i: (b, 0)),
                  pl.BlockSpec((4, HEAD_DIM), lambda b, i: (0, 0)),
                  pl.BlockSpec((DIFF_DIM, 1), lambda b, i: (0, 0))],
        out_specs=pl.BlockSpec((tq, w), lambda b, i: (b * nb + i, 0)),
        out_shape=jax.ShapeDtypeStruct(q.shape, F32),
        scratch_shapes=[pltpu.VMEM((2 * DIFF_HEADS, seq, HEAD_DIM), BF16),
                        pltpu.VMEM((DIFF_HEADS, DIFF_DIM, seq), BF16),
                        pltpu.VMEM((2 * DIFF_HEADS, tq, HEAD_DIM), BF16),
                        pltpu.VMEM((ATTN_CHAINS, seq, tq), F32),
                        pltpu.VMEM((ATTN_CHAINS, seq, tq), BF16),
                        pltpu.VMEM((w, tq), F32)],
        compiler_params=_cparams(("arbitrary", "arbitrary")),
        name="diff_prompt",
    )(q, k, v, lam4, gsub.reshape(DIFF_DIM, 1))


def _gelu_tanh(x):
    return 0.5 * x * (1.0 + jnp.tanh(math.sqrt(2.0 / math.pi) * (x + 0.044715 * (x * x * x))))


def _log_sigmoid(x):
    return jnp.minimum(x, 0.0) - jnp.log1p(jnp.exp(-jnp.abs(x)))


def _lru_gates(xc, wa_ref, ba_ref, wx_ref, bx_ref, lam_ref):
    bw = wa_ref.shape[1]
    xcb = xc.astype(BF16)
    ra, rx = [], []
    for n in range(RNN_BLOCKS):
        xn = xcb[:, n * bw:(n + 1) * bw]
        ra.append(jnp.dot(xn, wa_ref[n], preferred_element_type=F32))
        rx.append(jnp.dot(xn, wx_ref[n], preferred_element_type=F32))
    r = jax.nn.sigmoid(jnp.concatenate(ra, axis=1) + ba_ref[...])
    ig = jax.nn.sigmoid(jnp.concatenate(rx, axis=1) + bx_ref[...])
    log_a = LRU_C * r * _log_sigmoid(lam_ref[...])
    a = jnp.exp(log_a)
    b = jnp.sqrt(-jnp.tanh(log_a) * (1.0 + a * a)) * (ig * xc)
    return a, b


def _scan_rows(a, b_in, h0, nb, tc):
    sub = lax.broadcasted_iota(jnp.int32, (8, a.shape[1]), 0)
    rows, last = [], []
    for b in range(nb):
        h_in = h0[b:b + 1, :]
        for g in range(tc // 8):
            r0 = b * tc + 8 * g
            ga, gb = a[r0:r0 + 8, :], b_in[r0:r0 + 8, :]
            for sh in (1, 2, 4):
                keep = sub >= sh
                a_sh = jnp.where(keep, pltpu.roll(ga, sh, 0), 1.0)
                b_sh = jnp.where(keep, pltpu.roll(gb, sh, 0), 0.0)
                gb = ga * b_sh + gb
                ga = ga * a_sh
            h = ga * h_in + gb
            rows.append(h)
            h_in = h[7:8, :]
        last.append(h_in)
    return jnp.concatenate(rows, axis=0), jnp.concatenate(last, axis=0)


def _rglru_body(gate_ref, xb_ref, x_ref, cw_ref, cb_ref, wa_ref, ba_ref, wx_ref, bx_ref, lam_ref,
                wo_ref, g_ref, y_ref, buf_ref, hl_ref, xpad_sc, hc_sc):
    c = pl.program_id(0)
    nb, tc, d = xb_ref.shape
    hist = CONV_W - 1

    @pl.when(c == 0)
    def _():
        xpad_sc[:, 0:8, :] = jnp.zeros((nb, 8, d), F32)
        hc_sc[...] = jnp.zeros_like(hc_sc)

    @pl.when(c > 0)
    def _():
        xpad_sc[:, 0:8, :] = xpad_sc[:, tc:tc + 8, :]

    xpad_sc[:, 8:8 + tc, :] = xb_ref[...]
    xcs = []
    for b in range(nb):
        xc = cb_ref[...] + xpad_sc[b, 8 - hist:8 - hist + tc, :] * cw_ref[0:1, :]
        for i in range(1, CONV_W):
            xc = xc + xpad_sc[b, 8 - hist + i:8 - hist + i + tc, :] * cw_ref[i:i + 1, :]
        xcs.append(xc)
    a, b_in = _lru_gates(jnp.concatenate(xcs, axis=0), wa_ref, ba_ref, wx_ref, bx_ref, lam_ref)
    hall, h = _scan_rows(a, b_in, hc_sc[...], nb, tc)
    hc_sc[...] = h
    y = (hall * _gelu_tanh(gate_ref[...].reshape(nb * tc, d))).astype(BF16)
    y = jnp.dot(y, wo_ref[...], preferred_element_type=F32)
    y_ref[...] = (x_ref[...].reshape(nb * tc, d) + _rms(y, g_ref[...])).reshape(nb, tc, d)

    @pl.when(c == pl.num_programs(0) - 1)
    def _():
        buf_ref[...] = xpad_sc[:, 8 + tc - hist:8 + tc, :]
        hl_ref[...] = h


def _rglru_prompt(gate, xb, x, cw, cb, wa_bf, ba, wx_bf, bx, lam, wo_bf, g, batch, seq, tc=32):
    m, d = x.shape
    bw = d // RNN_BLOCKS
    chunk = pl.BlockSpec((batch, tc, d), lambda c: (0, c, 0))
    const2 = lambda c: (0, 0)
    const3 = lambda c: (0, 0, 0)
    vec = pl.BlockSpec((1, d), const2)
    r3 = lambda t: t.reshape(batch, seq, d)
    y, buf, hl = pl.pallas_call(
        _rglru_body,
        grid=(seq // tc,),
        in_specs=[chunk, chunk, chunk,
                  pl.BlockSpec((CONV_W, d), const2), vec,
                  pl.BlockSpec((RNN_BLOCKS, bw, bw), const3), vec,
                  pl.BlockSpec((RNN_BLOCKS, bw, bw), const3), vec, vec,
                  pl.BlockSpec((d, d), const2), vec],
        out_specs=[chunk,
                   pl.BlockSpec((batch, CONV_W - 1, d), const3),
                   pl.BlockSpec((batch, d), const2)],
        out_shape=[jax.ShapeDtypeStruct((batch, seq, d), F32),
                   jax.ShapeDtypeStruct((batch, CONV_W - 1, d), F32),
                   jax.ShapeDtypeStruct((batch, d), F32)],
        scratch_shapes=[pltpu.VMEM((batch, tc + 8, d), F32), pltpu.VMEM((batch, d), F32)],
        compiler_params=_cparams(("arbitrary",)),
        name="rglru_prompt",
    )(r3(gate), r3(xb), r3(x), cw, cb.reshape(1, d), wa_bf, ba.reshape(1, d), wx_bf, bx.reshape(1, d),
      lam.reshape(1, d), wo_bf, g.reshape(1, d))
    return y.reshape(m, d), buf, hl


def _rglru_step_body(gate_ref, xb_ref, x_ref, conv_ref, h0_ref, cw_ref, cb_ref, wa_ref, ba_ref,
                     wx_ref, bx_ref, lam_ref, wo_ref, g_ref, y_ref, h_ref):
    xc = cb_ref[...] + xb_ref[...] * cw_ref[CONV_W - 1:CONV_W, :]
    for i in range(CONV_W - 1):
        xc = xc + conv_ref[i] * cw_ref[i:i + 1, :]
    a, b = _lru_gates(xc, wa_ref, ba_ref, wx_ref, bx_ref, lam_ref)
    h = a * h0_ref[...] + b
    h_ref[...] = h
    y = (h * _gelu_tanh(gate_ref[...])).astype(BF16)
    y = jnp.dot(y, wo_ref[...], preferred_element_type=F32)
    y_ref[...] = x_ref[...] + _rms(y, g_ref[...])


def _rglru_step(gate, xb, x, conv_t, h0, cw, cb, wa_bf, ba, wx_bf, bx, lam, wo_bf, g):
    m, d = x.shape
    return pl.pallas_call(
        _rglru_step_body,
        out_shape=[jax.ShapeDtypeStruct((m, d), F32), jax.ShapeDtypeStruct((m, d), F32)],
        compiler_params=pltpu.CompilerParams(vmem_limit_bytes=VMEM_LIMIT),
        name="rglru_step",
    )(gate, xb, x, conv_t, h0, cw, cb.reshape(1, d), wa_bf, ba.reshape(1, d), wx_bf, bx.reshape(1, d),
      lam.reshape(1, d), wo_bf, g.reshape(1, d))


def _key_minor(cache):
    nd = cache.ndim
    t = jnp.transpose(cache, (0, 1) + tuple(range(3, nd)) + (2,))
    return t.reshape(t.shape[:2] + (-1,) + t.shape[-2:])


def _page_scores(kt_ref, qb):
    return jnp.sum(kt_ref[...] * qb, axis=1)


def _page_window(pt_ref, streams, li, n_pages, pp, n_rows, row, s0, nwin, prime, compute, slot0=0):
    nslots = streams[0][1].shape[0]
    nsteps = n_pages // pp
    ahead = nslots - 1
    assert (slot0 + nwin) % nslots in (0, nwin) and nsteps % nwin == 0 and nsteps >= ahead

    def copies(r, s, slot):
        base = r * n_pages + s * pp
        return [pltpu.make_async_copy(hbm.at[li, pt_ref[base + j]], buf.at[slot, j], sem.at[slot, j])
                for hbm, buf, sem in streams for j in range(pp)]

    def start(r, s, slot):
        for c in copies(r, s, slot):
            c.start()

    @pl.when(prime)
    def _():
        for s_first in range(ahead):
            start(row, s_first, s_first)

    for u in range(nwin):
        s, slot = s0 + u, (slot0 + u) % nslots
        tgt, tgt_slot = s + ahead, (slot + ahead) % nslots

        @pl.when(tgt < nsteps)
        def _():
            start(row, tgt, tgt_slot)

        @pl.when((tgt >= nsteps) & (row + 1 < n_rows))
        def _():
            start(row + 1, tgt - nsteps, tgt_slot)

        for c in copies(row, s, slot):
            c.wait()
        compute(s, slot)


def _page_stream(pt_ref, streams, li, n_pages, pp, compute):
    b = pl.program_id(0)
    nslots = streams[0][1].shape[0]

    def ring(i, carry):
        _page_window(pt_ref, streams, li, n_pages, pp, pl.num_programs(0), b, nslots * i, nslots,
                     (b == 0) & (i == 0), compute)
        return carry

    lax.fori_loop(0, n_pages // pp // nslots, ring, 0)


def _moba_gate_body(pt_ref, q_ref, kt_hbm, sel_ref, kbuf, ksem, qb_sc, g_sc, *, li, n_pages, pp, ppb):
    page = kbuf.shape[-1]
    nblk = g_sc.shape[0]
    qb_sc[...] = jnp.broadcast_to(q_ref[...], qb_sc.shape)

    def compute(s, slot):
        qb = qb_sc[...]
        for j in range(pp // ppb):
            tot = _page_scores(kbuf.at[slot, ppb * j], qb)
            for t in range(1, ppb):
                tot = tot + _page_scores(kbuf.at[slot, ppb * j + t], qb)
            g = jnp.sum(tot, axis=-1, keepdims=True) * (1.0 / (ppb * page))
            g_sc[s * (pp // ppb) + j] = jnp.broadcast_to(g, g_sc.shape[1:])

    _page_stream(pt_ref, [(kt_hbm, kbuf, ksem)], li, n_pages, pp, compute)

    g = g_sc[...]
    bid = lax.broadcasted_iota(jnp.int32, g.shape, 0)
    lid = lax.broadcasted_iota(jnp.int32, sel_ref.shape, 1)
    out = jnp.zeros(sel_ref.shape, jnp.int32)
    for t in range(MOBA_TOPK):
        mx = jnp.max(g, axis=0, keepdims=True)
        idx = jnp.min(jnp.where(g == mx, bid, nblk), axis=0, keepdims=True)
        out = jnp.where(lid == t, idx[0], out)
        g = jnp.where(bid == idx, -jnp.inf, g)
    sel_ref[...] = out


def _moba_gate(q_col, kt, pt_flat, li, batch, n_pages, pp=16):
    _, _, nh, hd, page = kt.shape
    ppb = MOBA_BLOCK // page
    nblk = n_pages // ppb
    return pl.pallas_call(
        functools.partial(_moba_gate_body, li=li, n_pages=n_pages, pp=pp, ppb=ppb),
        grid_spec=pltpu.PrefetchScalarGridSpec(
            num_scalar_prefetch=1, grid=(batch,),
            in_specs=[pl.BlockSpec((None, nh, hd, 1), lambda b, pt: (b, 0, 0, 0)),
                      pl.BlockSpec(memory_space=pl.ANY)],
            out_specs=pl.BlockSpec((None, nh, LANES), lambda b, pt: (b, 0, 0)),
            scratch_shapes=[pltpu.VMEM((PAGE_SLOTS, pp, nh, hd, page), F32),
                            pltpu.SemaphoreType.DMA((PAGE_SLOTS, pp)),
                            pltpu.VMEM((nh, hd, page), F32),
                            pltpu.VMEM((nblk, nh, LANES), F32)]),
        out_shape=jax.ShapeDtypeStruct((batch, nh, LANES), jnp.int32),
        compiler_params=_cparams(("arbitrary",)),
        name="moba_gate",
    )(pt_flat, q_col, kt)


def _moba_step_body(sel_ref, pt_ref, q_ref, kn_ref, vn_ref, kt_hbm, vt_hbm, o_ref, kbuf, vbuf, ksem, vsem,
                    *, li, n_pages, ppb):
    b = pl.program_id(0)
    _, nh, nsl = kbuf.shape[:3]
    slot = b % 2

    def copies(row, slot):
        out = []
        for h in range(nh):
            for t in range(MOBA_TOPK):
                blk = sel_ref[(row * nh + h) * MOBA_TOPK + t]
                for j in range(ppb):
                    page = pt_ref[row * n_pages + blk * ppb + j]
                    i = t * ppb + j
                    out.append(pltpu.make_async_copy(kt_hbm.at[li, page, h], kbuf.at[slot, h, i],
                                                     ksem.at[slot, h, i]))
                    out.append(pltpu.make_async_copy(vt_hbm.at[li, page, h], vbuf.at[slot, h, i],
                                                     vsem.at[slot, h, i]))
        return out

    @pl.when(b == 0)
    def _():
        for c in copies(0, 0):
            c.start()

    @pl.when(b + 1 < pl.num_programs(0))
    def _():
        for c in copies(b + 1, 1 - slot):
            c.start()

    for c in copies(b, slot):
        c.wait()

    for h in range(nh):
        q8 = jnp.broadcast_to(q_ref[h] * SCALE, (8, q_ref.shape[-1]))
        kt = jnp.concatenate([kbuf[slot, h, i] for i in range(nsl)], axis=1).astype(BF16)
        vt = jnp.concatenate([vbuf[slot, h, i] for i in range(nsl)], axis=1).astype(BF16)
        s = jnp.dot(q8.astype(BF16), kt, preferred_element_type=F32)
        s_self = jnp.sum(q8 * kn_ref[h], axis=-1, keepdims=True)
        m = jnp.maximum(jnp.max(s, axis=-1, keepdims=True), s_self)
        p = jnp.exp(s - m)
        p_self = jnp.exp(s_self - m)
        l = jnp.sum(p, axis=-1, keepdims=True) + p_self
        pv = lax.dot_general(p.astype(BF16), vt, NT_DIMS, preferred_element_type=F32)
        o_ref[h] = (pv + p_self * vn_ref[h]) / l


def _moba_step(q4, kn4, vn4, kt, vt, sel_flat, pt_flat, li, batch, n_pages):
    _, _, nh, hd, page = kt.shape
    ppb = MOBA_BLOCK // page
    nsl = MOBA_TOPK * ppb
    row = pl.BlockSpec((None, nh, 1, hd), lambda b, sel, pt: (b, 0, 0, 0))
    hbm = pl.BlockSpec(memory_space=pl.ANY)
    return pl.pallas_call(
        functools.partial(_moba_step_body, li=li, n_pages=n_pages, ppb=ppb),
        grid_spec=pltpu.PrefetchScalarGridSpec(
            num_scalar_prefetch=2, grid=(batch,),
            in_specs=[row, row, row, hbm, hbm],
            out_specs=pl.BlockSpec((None, nh, 8, hd), lambda b, sel, pt: (b, 0, 0, 0)),
            scratch_shapes=[pltpu.VMEM((2, nh, nsl, hd, page), F32), pltpu.VMEM((2, nh, nsl, hd, page), F32),
                            pltpu.SemaphoreType.DMA((2, nh, nsl)), pltpu.SemaphoreType.DMA((2, nh, nsl))]),
        out_shape=jax.ShapeDtypeStruct((batch, nh, 8, hd), F32),
        compiler_params=_cparams(("arbitrary",)),
        name="moba_step",
    )(sel_flat, pt_flat, q4, kn4, vn4, kt, vt)


def _diff_row_init(q_ref, qb_sc, m_sc, l_sc, acc_sc):
    qb_sc[...] = jnp.broadcast_to(q_ref[...] * SCALE, qb_sc.shape)
    m_sc[...] = jnp.full(m_sc.shape, NEG, F32)
    l_sc[...] = jnp.zeros_like(l_sc)
    acc_sc[...] = jnp.zeros_like(acc_sc)


def _diff_row_step(slot, kbuf, vbuf, qb_sc, m_sc, l_sc, acc_sc):
    pp, page = kbuf.shape[1], kbuf.shape[-1]
    qb = qb_sc[...]
    sc = jnp.concatenate([_page_scores(kbuf.at[slot, j], qb) for j in range(pp)], axis=1)
    m_old = m_sc[...]
    m_new = jnp.maximum(m_old, jnp.max(sc, axis=-1, keepdims=True))
    alpha = jnp.exp(m_old - m_new)
    p = jnp.exp(sc - m_new)
    l_sc[...] = alpha * l_sc[...] + jnp.sum(p, axis=-1, keepdims=True)
    m_sc[...] = m_new
    pb = p.astype(BF16)
    for h in range(DIFF_HEADS):
        vh = jnp.concatenate([vbuf[slot, j, pl.ds(h, page, stride=DIFF_HEADS), :] for j in range(pp)],
                             axis=0)
        acc_sc[h] = alpha * acc_sc[h] + jnp.dot(pb, vh.astype(BF16), preferred_element_type=F32)


def _diff_row_finish(q_ref, kn_ref, vn_ref, lam_ref, gsub_ref, o_ref, m_sc, l_sc, acc_sc, lam_init):
    s_self = jnp.sum(q_ref[...] * SCALE * kn_ref[...], axis=1)
    m_old = m_sc[...]
    m_new = jnp.maximum(m_old, s_self)
    alpha = jnp.exp(m_old - m_new)
    p_self = jnp.exp(s_self - m_new)
    l = alpha * l_sc[...] + p_self
    lam = _diff_lambda(lam_ref, lam_init)
    for h in range(DIFF_HEADS):
        vn = vn_ref[:, h * DIFF_DIM:(h + 1) * DIFF_DIM]
        o = (alpha * acc_sc[h] + p_self * vn) / l
        od = o[2 * h:2 * h + 1, :] - lam * o[2 * h + 1:2 * h + 2, :]
        o_ref[:, h * DIFF_DIM:(h + 1) * DIFF_DIM] = _rms(od, gsub_ref[...]) * (1.0 - lam_init)


def _mlp_diff_body(pt_ref, x_ref, g1_ref, wu_ref, wd_ref, g2_ref, q_ref, kn_ref, vn_ref, lam_ref, gsub_ref,
                   kt_hbm, v_hbm, y_ref, o_ref, xn_sc, acc_sc, kbuf, vbuf, ksem, vsem, qb_sc, m_sc, l_sc,
                   dacc_sc, *, li, n_pages, n_rows, nwin, lam_init):
    pp = kbuf.shape[1]
    g = pl.program_id(0) * pl.num_programs(1) + pl.program_id(1)
    gpr = n_pages // pp // nwin
    row, part = g // gpr, g % gpr
    state = (qb_sc, m_sc, l_sc, dacc_sc)

    @pl.when(part == 0)
    def _():
        _diff_row_init(q_ref, *state)

    half = nwin // 2

    def window(first, prime):
        _page_window(pt_ref, [(kt_hbm, kbuf, ksem), (v_hbm, vbuf, vsem)], li, n_pages, pp, n_rows, row,
                     part * nwin + first, half, prime, lambda s, slot: _diff_row_step(slot, kbuf, vbuf, *state),
                     slot0=first % kbuf.shape[0])

    window(0, g == 0)
    h = _mlp_up(x_ref, g1_ref, wu_ref, xn_sc, acc_sc)
    window(half, False)

    @pl.when(part == gpr - 1)
    def _():
        _diff_row_finish(q_ref, kn_ref, vn_ref, lam_ref, gsub_ref, o_ref, m_sc, l_sc, dacc_sc, lam_init)

    _mlp_down(h, wd_ref, x_ref, g2_ref, y_ref, acc_sc)


def _mlp_diff(x, g1, wu_bf, wd_bf, g2, q_col, kn_col, vn3, lam4, gsub, kt, v2, pt_flat, li, n_pages,
              lam_init, tm, tf, pp=8):
    m, d = x.shape
    ff = wu_bf.shape[1]
    n_rows, nrow, hd, _ = q_col.shape
    page = kt.shape[-1]
    vrows, dv = v2.shape[2:]
    w = vn3.shape[-1]
    nf = ff // tf
    n_grid = (m // tm) * nf
    nwin = n_rows * (n_pages // pp) // n_grid
    assert nwin * n_grid == n_rows * (n_pages // pp) and nwin % PAGE_SLOTS == 0 and nwin % 2 == 0
    gpr = n_pages // pp // nwin
    srow = lambda i, f, pt: ((i * nf + f) // gpr, 0, 0)
    col = pl.BlockSpec((None, nrow, hd, 1), lambda i, f, pt: ((i * nf + f) // gpr, 0, 0, 0))
    vec = pl.BlockSpec((1, d), lambda i, f, pt: (0, 0))
    hbm = pl.BlockSpec(memory_space=pl.ANY)
    return pl.pallas_call(
        functools.partial(_mlp_diff_body, li=li, n_pages=n_pages, n_rows=n_rows, nwin=nwin, lam_init=lam_init),
        grid_spec=pltpu.PrefetchScalarGridSpec(
            num_scalar_prefetch=1, grid=(m // tm, nf),
            in_specs=[pl.BlockSpec((tm, d), lambda i, f, pt: (i, 0)), vec,
                      pl.BlockSpec((d, tf), lambda i, f, pt: (0, f)),
                      pl.BlockSpec((tf, d), lambda i, f, pt: (f, 0)), vec,
                      col, col, pl.BlockSpec((None, 1, w), srow),
                      pl.BlockSpec((4, HEAD_DIM), lambda i, f, pt: (0, 0)),
                      pl.BlockSpec((1, DIFF_DIM), lambda i, f, pt: (0, 0)), hbm, hbm],
            out_specs=[pl.BlockSpec((tm, d), lambda i, f, pt: (i, 0)), pl.BlockSpec((None, 1, w), srow)],
            scratch_shapes=[pltpu.VMEM((tm, d), BF16), pltpu.VMEM((tm, d), F32),
                            pltpu.VMEM((PAGE_SLOTS, pp, nrow, hd, page), F32),
                            pltpu.VMEM((PAGE_SLOTS, pp, vrows, dv), F32),
                            pltpu.SemaphoreType.DMA((PAGE_SLOTS, pp)), pltpu.SemaphoreType.DMA((PAGE_SLOTS, pp)),
                            pltpu.VMEM((nrow, hd, page), F32), pltpu.VMEM((nrow, 1), F32),
                            pltpu.VMEM((nrow, 1), F32), pltpu.VMEM((DIFF_HEADS, nrow, dv), F32)]),
        out_shape=[jax.ShapeDtypeStruct((m, d), F32), jax.ShapeDtypeStruct((n_rows, 1, w), F32)],
        compiler_params=_cparams(("arbitrary", "arbitrary")),
        name="mlp_diff",
    )(pt_flat, x, g1.reshape(1, d), wu_bf, wd_bf, g2.reshape(1, d), q_col, kn_col, vn3, lam4,
      gsub.reshape(1, DIFF_DIM), kt, v2)


def kernel(x_prompt, x_sample, cache_moba_k, cache_moba_v, cache_diff_k, cache_diff_v, state_conv, state_rnn, page_table, attn_g_pre, attn_w_in, diff_lambda_q1, diff_lambda_k1, diff_lambda_q2, diff_lambda_k2, diff_g_sub, attn_w_out, attn_g_post, rnn_g_pre, rnn_w_in, rnn_conv_w, rnn_conv_b, rnn_w_a, rnn_b_a, rnn_w_x, rnn_b_x, rnn_lambda, rnn_w_out, rnn_g_post, mlp_g_pre, mlp_w_up, mlp_w_down, mlp_g_post):
    batch, seq, d = x_prompt.shape
    dec_batch, dec_seq, _ = x_sample.shape
    assert dec_seq == 1
    depth = mlp_w_up.shape[0]
    na, n_pool, page = cache_moba_k.shape[:3]
    n_pages = page_table.shape[1]
    past_len = n_pages * page
    mw = MOBA_HEADS * HEAD_DIM
    dw = DIFF_HEADS * DIFF_DIM

    xp = x_prompt.reshape(batch * seq, d)
    xs = x_sample.reshape(dec_batch, d)
    tab_p = _rope_tables(jnp.arange(seq, dtype=jnp.int32))
    tab_s = _rope_tables(jnp.full((dec_batch,), past_len, jnp.int32))
    pt_flat = page_table.reshape(-1)
    cmk_t = _key_minor(cache_moba_k)
    cmv_t = _key_minor(cache_moba_v)
    cdk_t = _key_minor(cache_diff_k)
    cdv2 = cache_diff_v.reshape(na, n_pool, page * DIFF_HEADS, DIFF_DIM)

    tm_p = 512
    attn_rope = (True, True, False, True, True, False)
    outs = {k: [] for k in ("pmk", "pmv", "pdk", "pdv", "pconv", "prnn",
                            "smk", "smv", "sdk", "sdv", "sconv", "srnn")}
    for layer in range(depth):
        li = layer // 2
        wu = mlp_w_up[layer].astype(BF16)
        wd = mlp_w_down[layer].astype(BF16)
        if layer % 2 == 0:
            lam_init = 0.8 - 0.6 * math.exp(-0.3 * layer)
            w_in = attn_w_in[li].astype(BF16)
            w_out = attn_w_out[li].astype(BF16)
            lam4 = jnp.stack([diff_lambda_q1[li], diff_lambda_k1[li], diff_lambda_q2[li], diff_lambda_k2[li]])
            mq, mk, mv, dq, dk, dv, mk_t, mv_t, dk_t = _in_proj(
                xp, attn_g_pre[li], w_in, tab_p, attn_rope, tm_p,
                t_flags=(False, True, True, False, True, False))
            o_m = _moba_prompt(mq, mk, mv, batch, seq)
            o_d = _diff_prompt(dq, dk, dv, lam4, diff_g_sub[li], batch, seq, lam_init)
            xp = _attn_out(o_m, o_d, w_out, attn_g_post[li], xp, tm_p)
            outs["pmk"].append(jnp.moveaxis(mk_t.reshape(batch, MOBA_HEADS, HEAD_DIM, seq), -1, 1))
            outs["pmv"].append(jnp.moveaxis(mv_t.reshape(batch, MOBA_HEADS, HEAD_DIM, seq), -1, 1))
            outs["pdk"].append(jnp.moveaxis(dk_t.reshape(batch, DIFF_HEADS, 2, HEAD_DIM, seq), -1, 1))
            outs["pdv"].append(dv.reshape(batch, seq, DIFF_HEADS, DIFF_DIM))
            mq, mk, mv, dq, dk, dv = _in_proj(xs, attn_g_pre[li], w_in, tab_s, attn_rope, dec_batch)
            col = lambda t: t.reshape(dec_batch, -1, HEAD_DIM, 1)
            hrow = lambda t: t.reshape(dec_batch, MOBA_HEADS, 1, HEAD_DIM)
            xp, o_d = _mlp_diff(xp, mlp_g_pre[layer], wu, wd, mlp_g_post[layer],
                                col(dq), col(dk), dv.reshape(dec_batch, 1, dw), lam4, diff_g_sub[li],
                                cdk_t, cdv2, pt_flat, li, n_pages, lam_init, 1024, 512)
            sel = _moba_gate(col(mq), cmk_t, pt_flat, li, dec_batch, n_pages)
            sel = sel[:, :, :MOBA_TOPK].reshape(-1)
            o_m = _moba_step(hrow(mq), hrow(mk), hrow(mv), cmk_t, cmv_t, sel, pt_flat, li,
                             dec_batch, n_pages)[:, :, 0, :]
            xs = _attn_out(o_m.reshape(dec_batch, mw), o_d.reshape(dec_batch, dw), w_out,
                           attn_g_post[li], xs, dec_batch)
            outs["smk"].append(mk.reshape(dec_batch, 1, MOBA_HEADS, HEAD_DIM))
            outs["smv"].append(mv.reshape(dec_batch, 1, MOBA_HEADS, HEAD_DIM))
            outs["sdk"].append(dk.reshape(dec_batch, 1, DIFF_HEADS, 2, HEAD_DIM))
            outs["sdv"].append(dv.reshape(dec_batch, 1, DIFF_HEADS, DIFF_DIM))
        else:
            w_in = rnn_w_in[li].astype(BF16)
            wts = (rnn_conv_w[li], rnn_conv_b[li], rnn_w_a[li].astype(BF16), rnn_b_a[li].reshape(-1),
                   rnn_w_x[li].astype(BF16), rnn_b_x[li].reshape(-1), rnn_lambda[li],
                   rnn_w_out[li].astype(BF16), rnn_g_post[li])
            gate, xb = _in_proj(xp, rnn_g_pre[li], w_in, tab_p, (False, False), tm_p)
            xp, cbuf, hlast = _rglru_prompt(gate, xb, xp, *wts, batch, seq)
            outs["pconv"].append(cbuf)
            outs["prnn"].append(hlast.reshape(batch, d))
            gate, xb = _in_proj(xs, rnn_g_pre[li], w_in, tab_s, (False, False), dec_batch)
            conv_t = jnp.swapaxes(state_conv[li], 0, 1)
            xs, hnew = _rglru_step(gate, xb, xs, conv_t, state_rnn[li], *wts)
            outs["sconv"].append(jnp.concatenate([state_conv[li][:, 1:], xb[:, None, :]], axis=1))
            outs["srnn"].append(hnew)
            xp = _mlp(xp, mlp_g_pre[layer], wu, wd, mlp_g_post[layer], 1024, 1024)
        xs = _mlp(xs, mlp_g_pre[layer], wu, wd, mlp_g_post[layer], dec_batch, 512)

    st = lambda k: jnp.stack(outs[k])
    return (xp.reshape(batch, seq, d), xs.reshape(dec_batch, 1, d),
            st("pmk"), st("pmv"), st("pdk"), st("pdv"), st("pconv"), st("prnn"),
            st("smk"), st("smv"), st("sdk"), st("sdv"), st("sconv"), st("srnn"))
```
